```python
import jax, jax.numpy as jnp
from jax import lax
import numpy as np

D_MODEL = 2048
BATCH = 1
SEQ = 16384
DEPTH = 4
DEC_BATCH = 32
DEC_SEQ = 16
PAST_LEN = 1024

CHUNK = 64
N_MIXERS = 3
N_A_LAYERS = (DEPTH + 2) // 3
N_B_LAYERS = (DEPTH + 1) // 3
N_C_LAYERS = DEPTH // 3
NORM_EPS = 1e-6

A_WIDTH = 2 * D_MODEL
A_GROUPS = 16
A_GROUP_DIM = A_WIDTH // A_GROUPS
A_CHUNK = 128

B_D_INNER = 2 * D_MODEL
B_HEAD_DIM = 64
B_HEADS = B_D_INNER // B_HEAD_DIM
B_GROUPS = 8
B_HEADS_PER_GROUP = B_HEADS // B_GROUPS
B_STATE = 128
B_CONV = 4
B_CONV_DIM = B_D_INNER + 2 * B_GROUPS * B_STATE
B_IN_DIM = B_D_INNER + B_CONV_DIM + B_HEADS
B_SCAN_CHUNK = CHUNK

C_HEADS = 16
C_HEAD_DIM = D_MODEL // C_HEADS
C_WIDTH = C_HEADS * C_HEAD_DIM
C_BLOCK = 128

kernel_name = 'hybrid_gmlp_ssd_stickbreak_stream_step'


def rms_norm(x, w):
    xf = x.astype(jnp.float32)
    y = xf * lax.rsqrt(jnp.mean(xf * xf, axis=-1, keepdims=True) + NORM_EPS)
    return (y * w.astype(jnp.float32)).astype(x.dtype)


def gmlp_mixer(h, w_in, ln_g, ln_b, w_s, b_s, w_out):
    bsz, L, _ = h.shape
    u, v, z = jnp.split(h @ w_in, 3, axis=-1)
    u = jax.nn.gelu(u, approximate=False)
    vf = jax.nn.gelu(v, approximate=False).astype(jnp.float32)
    vc = vf - jnp.mean(vf, axis=-1, keepdims=True)
    var = jnp.mean(vc * vc, axis=-1, keepdims=True)
    v = (vc * lax.rsqrt(var + NORM_EPS) * ln_g + ln_b).astype(h.dtype)
    lc = min(L, A_CHUNK)
    pos = jnp.arange(lc)
    mask = (pos[None, :] // CHUNK) <= (pos[:, None] // CHUNK)
    w_pos = jnp.where(mask[None], w_s[:, :lc, :lc], 0.0)
    vg = v.reshape(bsz, L // lc, lc, A_GROUPS, A_GROUP_DIM)
    s = jnp.einsum('gts,bcsgk->bctgk', w_pos, vg) + b_s[:, :lc].T[None, None, :, :, None]
    y = u * s.reshape(bsz, L, A_WIDTH) * jax.nn.silu(z)
    return y @ w_out, v


def ssd_scan(x, dt, a, bm, cm, s0):
    bsz, L = x.shape[:2]
    lc = min(L, B_SCAN_CHUNK)
    nc = L // lc

    def to_chunks(t):
        return jnp.moveaxis(t.reshape(bsz, nc, lc, *t.shape[2:]), 1, 0)

    tri = jnp.tril(jnp.ones((lc, lc), dtype=bool))

    def step(s, inp):
        xc, dtc, bc, cc = inp
        cum = jnp.cumsum(dtc * a, axis=1)
        seg = cum[:, :, None] - cum[:, None, :]
        decay = jnp.exp(jnp.where(tri[None, :, :, None, None], seg, -jnp.inf))
        cb = jnp.einsum('btgn,bsgn->btsg', cc, bc)
        y = jnp.einsum('btsg,btsgr,bsgr,bsgrp->btgrp', cb, decay, dtc, xc)
        y = y + jnp.einsum('btgn,bgrpn,btgr->btgrp', cc, s, jnp.exp(cum))
        w_end = jnp.exp(cum[:, -1:] - cum) * dtc
        s = s * jnp.exp(cum[:, -1])[..., None, None] + jnp.einsum('bsgn,bsgr,bsgrp->bgrpn', bc, w_end, xc)
        return s, y

    s_final, ys = lax.scan(step, s0, (to_chunks(x), to_chunks(dt), to_chunks(bm), to_chunks(cm)))
    y = jnp.moveaxis(ys, 0, 1).reshape(x.shape)
    return y, s_final


def mamba_mixer(h, ssm_state, conv_state, w_in, conv_w, conv_b, dt_bias, a_log, d_skip, norm_w, w_out):
    bsz, L, _ = h.shape
    proj = h @ w_in
    z = proj[..., :B_D_INNER]
    xbc = proj[..., B_D_INNER:B_D_INNER + B_CONV_DIM]
    dt = proj[..., B_D_INNER + B_CONV_DIM:]
    xpad = jnp.concatenate([conv_state.astype(xbc.dtype), xbc], axis=1)
    new_conv = xpad[:, -(B_CONV - 1):]
    conv = conv_b
    for k in range(B_CONV):
        conv = conv + xpad[:, k:k + L] * conv_w[k]
    xbc = jax.nn.silu(conv)
    x = xbc[..., :B_D_INNER]
    bm = xbc[..., B_D_INNER:B_D_INNER + B_GROUPS * B_STATE]
    cm = xbc[..., B_D_INNER + B_GROUPS * B_STATE:]
    f32 = jnp.float32
    dt = jax.nn.softplus(dt.astype(f32) + dt_bias.astype(f32))
    a = -jnp.exp(a_log.astype(f32)).reshape(B_GROUPS, B_HEADS_PER_GROUP)
    xh = x.astype(f32).reshape(bsz, L, B_GROUPS, B_HEADS_PER_GROUP, B_HEAD_DIM)
    y, s_new = ssd_scan(
        xh,
        dt.reshape(bsz, L, B_GROUPS, B_HEADS_PER_GROUP),
        a,
        bm.astype(f32).reshape(bsz, L, B_GROUPS, B_STATE),
        cm.astype(f32).reshape(bsz, L, B_GROUPS, B_STATE),
        ssm_state.astype(f32).reshape(bsz, B_GROUPS, B_HEADS_PER_GROUP, B_HEAD_DIM, B_STATE))
    y = y + d_skip.astype(f32).reshape(B_GROUPS, B_HEADS_PER_GROUP)[..., None] * xh
    y = y.reshape(bsz, L, B_D_INNER) * jax.nn.silu(z.astype(f32))
    yg = y.reshape(bsz, L, B_GROUPS, B_D_INNER // B_GROUPS)
    yg = yg * lax.rsqrt(jnp.mean(yg * yg, axis=-1, keepdims=True) + NORM_EPS)
    y = (yg.reshape(bsz, L, B_D_INNER) * norm_w.astype(f32)).astype(h.dtype)
    s_new = s_new.reshape(bsz, B_HEADS, B_HEAD_DIM, B_STATE).astype(h.dtype)
    return y @ w_out, s_new, new_conv


def sb_attend(q, k, v, q_pos, k_pos):
    z = jnp.einsum('bhtd,bhsd->bhts', q.astype(jnp.float32), k.astype(jnp.float32)) * (C_HEAD_DIM ** -0.5)
    visible = k_pos[None, :] < q_pos[:, None]
    log_keep = jnp.where(visible, jax.nn.log_sigmoid(-z), 0.0)
    log_surv = lax.cumsum(log_keep, axis=3, reverse=True) - log_keep
    w = jnp.where(visible, jnp.exp(jax.nn.log_sigmoid(z) + log_surv), 0.0)
    return jnp.einsum('bhts,bhsd->bhtd', w, v.astype(jnp.float32)).astype(q.dtype)


def sb_mixer(h, k_past, v_past, w_in, w_out):
    bsz, L, _ = h.shape
    q, k, v, z = jnp.split(h @ w_in, 4, axis=-1)

    def heads(t):
        return t.reshape(bsz, L, C_HEADS, C_HEAD_DIM).transpose(0, 2, 1, 3)

    q, k, v = heads(q), heads(k), heads(v)
    past = k_past.shape[2]
    k_all = jnp.concatenate([k_past.astype(k.dtype), k], axis=2)
    v_all = jnp.concatenate([v_past.astype(v.dtype), v], axis=2)
    k_pos = jnp.arange(past + L)
    blk = min(L, C_BLOCK)
    nb = L // blk
    qb = jnp.moveaxis(q.reshape(bsz, C_HEADS, nb, blk, C_HEAD_DIM), 2, 0)
    starts = past + jnp.arange(nb) * blk
    o = lax.map(lambda a: sb_attend(a[0], k_all, v_all, a[1] + jnp.arange(blk), k_pos), (qb, starts))
    o = jnp.moveaxis(o, 0, 2).reshape(bsz, C_HEADS, L, C_HEAD_DIM).transpose(0, 2, 1, 3).reshape(bsz, L, C_WIDTH)
    y = (o * jax.nn.silu(z)) @ w_out
    return y, k, v


def setup_inputs(seed: int = 0) -> dict:
    key = jax.random.key(seed)
    ks = jax.random.split(key, 26)
    f32 = jnp.float32

    def nrm(k, shape, scale):
        return jax.random.normal(k, shape, f32) * scale

    dt0 = jnp.exp(jax.random.uniform(ks[16], (N_B_LAYERS, B_HEADS), f32, np.log(1e-3), np.log(1e-1)))
    return {
        'x_prompt': nrm(ks[0], (BATCH, SEQ, D_MODEL), 1.0),
        'x_sample': nrm(ks[1], (DEC_BATCH, DEC_SEQ, D_MODEL), 1.0),
        'state_ssm': nrm(ks[2], (N_B_LAYERS, DEC_BATCH, B_HEADS, B_HEAD_DIM, B_STATE), 0.5),
        'state_conv': nrm(ks[3], (N_B_LAYERS, DEC_BATCH, B_CONV - 1, B_CONV_DIM), 1.0),
        'cache_k': nrm(ks[4], (N_C_LAYERS, DEC_BATCH, C_HEADS, PAST_LEN, C_HEAD_DIM), 1.0),
        'cache_v': nrm(ks[5], (N_C_LAYERS, DEC_BATCH, C_HEADS, PAST_LEN, C_HEAD_DIM), 1.0),
        'norm_w': 1.0 + nrm(ks[6], (DEPTH, D_MODEL), 0.02),
        'final_norm_w': 1.0 + nrm(ks[7], (D_MODEL,), 0.02),
        'a_w_in': nrm(ks[8], (N_A_LAYERS, D_MODEL, 3 * A_WIDTH), D_MODEL ** -0.5),
        'a_ln_g': 1.0 + nrm(ks[9], (N_A_LAYERS, A_WIDTH), 0.02),
        'a_ln_b': nrm(ks[10], (N_A_LAYERS, A_WIDTH), 0.02),
        'a_w_s': nrm(ks[11], (N_A_LAYERS, A_GROUPS, A_CHUNK, A_CHUNK), A_CHUNK ** -0.5),
        'a_b_s': 1.0 + nrm(ks[12], (N_A_LAYERS, A_GROUPS, A_CHUNK), 0.01),
        'a_w_out': nrm(ks[13], (N_A_LAYERS, A_WIDTH, D_MODEL), A_WIDTH ** -0.5),
        'b_w_in': nrm(ks[14], (N_B_LAYERS, D_MODEL, B_IN_DIM), D_MODEL ** -0.5),
        'b_conv_w': nrm(ks[15], (N_B_LAYERS, B_CONV, B_CONV_DIM), B_CONV ** -0.5),
        'b_conv_b': nrm(ks[17], (N_B_LAYERS, B_CONV_DIM), 0.02),
        'b_dt_bias': dt0 + jnp.log(-jnp.expm1(-dt0)),
        'b_a_log': jnp.log(jax.random.uniform(ks[18], (N_B_LAYERS, B_HEADS), f32, 1.0, 16.0)),
        'b_d': 1.0 + nrm(ks[19], (N_B_LAYERS, B_HEADS), 0.02),
        'b_norm_w': 1.0 + nrm(ks[20], (N_B_LAYERS, B_D_INNER), 0.02),
        'b_w_out': nrm(ks[21], (N_B_LAYERS, B_D_INNER, D_MODEL), B_D_INNER ** -0.5),
        'c_w_in': nrm(ks[22], (N_C_LAYERS, D_MODEL, 4 * C_WIDTH), D_MODEL ** -0.5),
        'c_w_out': nrm(ks[23], (N_C_LAYERS, C_WIDTH, D_MODEL), C_WIDTH ** -0.5),
    }


def reference(x_prompt, x_sample, state_ssm, state_conv, cache_k, cache_v,
              norm_w, final_norm_w,
              a_w_in, a_ln_g, a_ln_b, a_w_s, a_b_s, a_w_out,
              b_w_in, b_conv_w, b_conv_b, b_dt_bias, b_a_log, b_d, b_norm_w, b_w_out,
              c_w_in, c_w_out):
    xp, xs = x_prompt, x_sample
    gmlp_v_s, ssm_p, conv_p, ssm_s, conv_s = [], [], [], [], []
    k_p, v_p, k_s, v_s = [], [], [], []
    bp = xp.shape[0]
    for i in range(DEPTH):
        kind, j = i % N_MIXERS, i // N_MIXERS
        hp = rms_norm(xp, norm_w[i])
        hs = rms_norm(xs, norm_w[i])
        if kind == 0:
            a_par = (a_w_in[j], a_ln_g[j], a_ln_b[j], a_w_s[j], a_b_s[j], a_w_out[j])
            dp, _ = gmlp_mixer(hp, *a_par)
            ds, v_new = gmlp_mixer(hs, *a_par)
            gmlp_v_s.append(v_new)
        elif kind == 1:
            b_par = (b_w_in[j], b_conv_w[j], b_conv_b[j], b_dt_bias[j], b_a_log[j], b_d[j], b_norm_w[j], b_w_out[j])
            s0 = jnp.zeros((bp, B_HEADS, B_HEAD_DIM, B_STATE), xp.dtype)
            c0 = jnp.zeros((bp, B_CONV - 1, B_CONV_DIM), xp.dtype)
            dp, s1, c1 = mamba_mixer(hp, s0, c0, *b_par)
            ds, s2, c2 = mamba_mixer(hs, state_ssm[j], state_conv[j], *b_par)
            ssm_p.append(s1)
            conv_p.append(c1)
            ssm_s.append(s2)
            conv_s.append(c2)
        else:
            empty = jnp.zeros((bp, C_HEADS, 0, C_HEAD_DIM), xp.dtype)
            dp, k1, v1 = sb_mixer(hp, empty, empty, c_w_in[j], c_w_out[j])
            ds, k2, v2 = sb_mixer(hs, cache_k[j], cache_v[j], c_w_in[j], c_w_out[j])
            k_p.append(k1)
            v_p.append(v1)
            k_s.append(k2)
            v_s.append(v2)
        xp = xp + dp
        xs = xs + ds
    y_prompt = rms_norm(xp, final_norm_w)
    y_sample = rms_norm(xs, final_norm_w)
    return (y_prompt, y_sample, jnp.stack(gmlp_v_s), jnp.stack(ssm_p), jnp.stack(conv_p),
            jnp.stack(ssm_s), jnp.stack(conv_s), jnp.stack(k_p), jnp.stack(v_p),
            jnp.stack(k_s), jnp.stack(v_s))
```

```python
import functools

import jax
import jax.numpy as jnp
from jax import lax
from jax.experimental import pallas as pl
from jax.experimental.pallas import tpu as pltpu

F32 = jnp.float32
BF16 = jnp.bfloat16

NORM_EPS = 1e-6
D_MODEL = 2048
CHUNK = 64
A_GROUPS = 16
A_CHUNK = 128
B_HEADS = 64
B_HEAD_DIM = 64
B_GROUPS = 8
B_STATE = 128
B_CONV = 4
C_HEADS = 16
C_HEAD_DIM = 128

LANES = 128
MIB = 1024 * 1024


def _cparams(sem, vmem_mib):
    return pltpu.CompilerParams(dimension_semantics=sem, vmem_limit_bytes=vmem_mib * MIB)


def _rms_kernel(x_ref, w_ref, h_ref):
    x = x_ref[...]
    ms = jnp.mean(x * x, axis=-1, keepdims=True)
    h_ref[...] = (x * lax.rsqrt(ms + NORM_EPS) * w_ref[...]).astype(h_ref.dtype)


def rms_norm_bf16(x, w, *, tm):
    m, d = x.shape
    return pl.pallas_call(
        _rms_kernel,
        grid=(m // tm,),
        in_specs=[pl.BlockSpec((tm, d), lambda i: (i, 0)), pl.BlockSpec((1, d), lambda i: (0, 0))],
        out_specs=pl.BlockSpec((tm, d), lambda i: (i, 0)),
        out_shape=jax.ShapeDtypeStruct((m, d), BF16),
        compiler_params=_cparams(("parallel",), 32),
        name="rms_norm",
    )(x, w.reshape(1, d))


def _mm_kernel(x_ref, w_ref, o_ref):
    o_ref[...] = jnp.dot(x_ref[...], w_ref[...], preferred_element_type=F32).astype(o_ref.dtype)


def matmul(x, w, *, tm, tn, out_dtype=F32, name="proj"):
    m, k = x.shape
    n = w.shape[1]
    return pl.pallas_call(
        _mm_kernel,
        grid=(m // tm, n // tn),
        in_specs=[pl.BlockSpec((tm, k), lambda i, j: (i, 0)), pl.BlockSpec((k, tn), lambda i, j: (0, j))],
        out_specs=pl.BlockSpec((tm, tn), lambda i, j: (i, j)),
        out_shape=jax.ShapeDtypeStruct((m, n), out_dtype),
        compiler_params=_cparams(("parallel", "parallel"), 48),
        name=name,
    )(x, w)


def _mm_res_norm_kernel(y_ref, w_ref, x_ref, nw_ref, *out_refs, final):
    xn = x_ref[...] + jnp.dot(y_ref[...], w_ref[...], preferred_element_type=F32)
    ms = jnp.mean(xn * xn, axis=-1, keepdims=True)
    h = xn * lax.rsqrt(ms + NORM_EPS) * nw_ref[...]
    if final:
        out_refs[0][...] = h
    else:
        out_refs[0][...] = xn
        out_refs[1][...] = h.astype(BF16)


def matmul_residual_norm(y, w, x, nw, *, tm, final=False, name="out_proj"):
    m, k = y.shape
    d = w.shape[1]
    row = lambda i: (i, 0)
    if final:
        out_shape = jax.ShapeDtypeStruct((m, d), F32)
        out_specs = pl.BlockSpec((tm, d), row)
    else:
        out_shape = (jax.ShapeDtypeStruct((m, d), F32), jax.ShapeDtypeStruct((m, d), BF16))
        out_specs = (pl.BlockSpec((tm, d), row), pl.BlockSpec((tm, d), row))
    return pl.pallas_call(
        functools.partial(_mm_res_norm_kernel, final=final),
        grid=(m // tm,),
        in_specs=[
            pl.BlockSpec((tm, k), row),
            pl.BlockSpec((k, d), lambda i: (0, 0)),
            pl.BlockSpec((tm, d), row),
            pl.BlockSpec((1, d), lambda i: (0, 0)),
        ],
        out_specs=out_specs,
        out_shape=out_shape,
        compiler_params=_cparams(("parallel",), 56),
        name=name,
    )(y, w, x, nw.reshape(1, d))


def _gelu(x):
    return 0.5 * x * (1.0 + lax.erf(x * (2.0 ** -0.5)))


def _silu(x):
    return x * (1.0 / (1.0 + jnp.exp(-x)))


A_GW = 256


def _gmlp_kernel(u_ref, v_ref, z_ref, wpos_ref, bias_ref, g_ref, b_ref, *refs, emit_v):
    if emit_v:
        y_ref, vout_ref, gv_scr, vn_scr = refs
    else:
        y_ref, gv_scr, vn_scr = refs
    rows, width = gv_scr.shape
    ngroups = width // A_GW
    acc = jnp.zeros((rows, LANES), F32)
    for g in range(ngroups):
        sl = slice(g * A_GW, (g + 1) * A_GW)
        gv = _gelu(v_ref[:, sl])
        gv_scr[:, sl] = gv
        acc = acc + gv[:, :LANES] + gv[:, LANES:]
    mean = jnp.sum(acc, axis=-1, keepdims=True) * (1.0 / width)
    acc = jnp.zeros((rows, LANES), F32)
    for g in range(ngroups):
        sl = slice(g * A_GW, (g + 1) * A_GW)
        vc = gv_scr[:, sl] - mean
        sq = vc * vc
        acc = acc + sq[:, :LANES] + sq[:, LANES:]
    var = jnp.sum(acc, axis=-1, keepdims=True) * (1.0 / width)
    rstd = lax.rsqrt(var + NORM_EPS)
    for g in range(ngroups):
        sl = slice(g * A_GW, (g + 1) * A_GW)
        vn = (gv_scr[:, sl] - mean) * rstd * g_ref[:, sl] + b_ref[:, sl]
        if emit_v:
            vout_ref[:, sl] = vn
        vn_scr[:, sl] = vn.astype(BF16)
    for g in range(ngroups):
        sl = slice(g * A_GW, (g + 1) * A_GW)
        s = jnp.dot(wpos_ref[g], vn_scr[:, sl], preferred_element_type=F32) + bias_ref[:, sl]
        y_ref[:, sl] = (_gelu(u_ref[:, sl]) * s * _silu(z_ref[:, sl])).astype(BF16)


def gmlp_core(p, wpos, bias, ln_g, ln_b, *, emit_v):
    m, w3 = p.shape
    w = w3 // 3
    t = A_CHUNK
    col = lambda c: pl.BlockSpec((t, w), lambda i, c=c: (i, c))
    const2 = lambda shape: pl.BlockSpec(shape, lambda i: (0, 0))
    out_shape = [jax.ShapeDtypeStruct((m, w), BF16)]
    out_specs = [pl.BlockSpec((t, w), lambda i: (i, 0))]
    if emit_v:
        out_shape.append(jax.ShapeDtypeStruct((m, w), F32))
        out_specs.append(pl.BlockSpec((t, w), lambda i: (i, 0)))
    res = pl.pallas_call(
        functools.partial(_gmlp_kernel, emit_v=emit_v),
        grid=(m // t,),
        in_specs=[
            col(0), col(1), col(2),
            pl.BlockSpec(wpos.shape, lambda i: (0, 0, 0)),
            const2((t, w)), const2((1, w)), const2((1, w)),
        ],
        out_specs=out_specs,
        out_shape=out_shape,
        scratch_shapes=[pltpu.VMEM((t, w), F32), pltpu.VMEM((t, w), BF16)],
        compiler_params=_cparams(("parallel",), 40),
        name="gmlp_core",
    )(p, p, p, wpos, bias, ln_g.reshape(1, w), ln_b.reshape(1, w))
    return res if emit_v else res[0]


def _gmlp_pos_params(w_s, b_s, lc):
    pos = jnp.arange(lc)
    mask = (pos[None, :] // CHUNK) <= (pos[:, None] // CHUNK)
    wp = jnp.where(mask[None], w_s[:, :lc, :lc], 0.0)
    reps = A_CHUNK // lc
    if reps > 1:
        eye = jnp.eye(reps, dtype=wp.dtype)
        wp = jnp.einsum("ab,gts->gatbs", eye, wp).reshape(w_s.shape[0], A_CHUNK, A_CHUNK)
    bias_t = jnp.tile(b_s[:, :lc].T, (reps, 1))
    bias = jnp.repeat(bias_t, A_GW, axis=1)
    return wp.astype(BF16), bias.astype(F32)


C_TK = 128


def _suffix_sum_matrix():
    j = jnp.arange(C_TK)
    tri = (j[:, None] >= j[None, :]).astype(BF16)
    half = jnp.concatenate([tri, jnp.ones((C_TK, C_TK), BF16)], axis=1)
    return jnp.concatenate([half, half], axis=0)


def _sb_block(q, kb, vb, carry, tmat, vis):
    z = lax.dot_general(q, kb, (((1,), (1,)), ((), ())), preferred_element_type=F32)
    lk = -(jnp.maximum(z, 0.0) + jnp.log1p(jnp.exp(-jnp.abs(z))))
    if vis is not None:
        lk = jnp.where(vis, lk, 0.0)
    hi = lk.astype(BF16)
    lo = (lk - hi.astype(F32)).astype(BF16)
    c = jnp.dot(jnp.concatenate([hi, lo], axis=1), tmat, preferred_element_type=F32)
    w = jnp.exp(z + c[:, :C_TK] + carry)
    if vis is not None:
        w = jnp.where(vis, w, 0.0)
    pv = jnp.dot(w.astype(BF16), vb, preferred_element_type=F32)
    return pv, carry + c[:, C_TK:]


def _attn_prompt_kernel(q_ref, k_ref, v_ref, zg_ref, tmat_ref, o_ref, ko_ref, vo_ref,
                        kb_scr, vb_scr, acc_scr, carry_scr, *, tq):
    i = pl.program_id(1)

    @pl.when(i == 0)
    def _():
        kb_scr[...] = k_ref[...].astype(BF16)
        vb_scr[...] = v_ref[...].astype(BF16)

    r0 = pl.multiple_of(i * tq, tq)
    ko_ref[...] = k_ref[pl.ds(r0, tq), :]
    vo_ref[...] = v_ref[pl.ds(r0, tq), :]
    q = (q_ref[...] * (C_HEAD_DIM ** -0.5)).astype(BF16)
    tmat = tmat_ref[...]
    acc_scr[...] = jnp.zeros_like(acc_scr)
    carry_scr[...] = jnp.zeros_like(carry_scr)

    def step(j, vis):
        k0 = pl.multiple_of(j * C_TK, C_TK)
        pv, carry = _sb_block(q, kb_scr[pl.ds(k0, C_TK), :], vb_scr[pl.ds(k0, C_TK), :], carry_scr[...], tmat, vis)
        acc_scr[...] += pv
        carry_scr[...] = carry

    nsub = tq // C_TK
    qpos = lax.broadcasted_iota(jnp.int32, (tq, C_TK), 0)
    kpos = lax.broadcasted_iota(jnp.int32, (tq, C_TK), 1)
    for s in range(nsub - 1, -1, -1):
        step(i * nsub + s, (kpos + s * C_TK) < qpos)

    def body(it, c):
        step(i * nsub - 1 - it, None)
        return c

    lax.fori_loop(0, i * nsub, body, 0)
    o_ref[...] = (acc_scr[...] * _silu(zg_ref[...])).astype(o_ref.dtype)


def attn_prompt(p, *, tq):
    seq, w4 = p.shape
    w = w4 // 4
    nh = w // C_HEAD_DIM
    d = C_HEAD_DIM
    blk = lambda c: pl.BlockSpec((tq, d), lambda h, i, c=c: (i, c * nh + h))
    whole = lambda c: pl.BlockSpec((seq, d), lambda h, i, c=c: (0, c * nh + h))
    tmat = _suffix_sum_matrix()
    return pl.pallas_call(
        functools.partial(_attn_prompt_kernel, tq=tq),
        grid=(nh, seq // tq),
        in_specs=[blk(0), whole(1), whole(2), blk(3), pl.BlockSpec(tmat.shape, lambda h, i: (0, 0))],
        out_specs=[
            pl.BlockSpec((tq, d), lambda h, i: (i, h)),
            pl.BlockSpec((None, tq, d), lambda h, i: (h, i, 0)),
            pl.BlockSpec((None, tq, d), lambda h, i: (h, i, 0)),
        ],
        out_shape=[
            jax.ShapeDtypeStruct((seq, w), BF16),
            jax.ShapeDtypeStruct((nh, seq, d), F32),
            jax.ShapeDtypeStruct((nh, seq, d), F32),
        ],
        scratch_shapes=[
            pltpu.VMEM((seq, d), BF16), pltpu.VMEM((seq, d), BF16),
            pltpu.VMEM((tq, d), F32), pltpu.VMEM((tq, C_TK), F32),
        ],
        compiler_params=_cparams(("arbitrary", "arbitrary"), 56),
        name="attn_prompt",
    )(p, p, p, p, tmat)


def _attn_sample_kernel(q_ref, kn_ref, vn_ref, zg_ref, kc_ref, vc_ref, tmat_ref, o_ref, ko_ref, vo_ref, *, hb, past):
    t = q_ref.shape[0]
    d = C_HEAD_DIM
    tmat = tmat_ref[...]
    nblk = past // C_TK
    qpos = lax.broadcasted_iota(jnp.int32, (t, C_TK), 0)
    kpos = lax.broadcasted_iota(jnp.int32, (t, C_TK), 1)
    vis_new = kpos < qpos
    pad = jnp.zeros((C_TK - t, d), BF16)
    for h in range(hb):
        sl = slice(h * d, (h + 1) * d)
        kn = kn_ref[:, sl]
        vn = vn_ref[:, sl]
        ko_ref[h] = kn
        vo_ref[h] = vn
        q = (q_ref[:, sl] * (C_HEAD_DIM ** -0.5)).astype(BF16)
        carry = jnp.zeros((t, C_TK), F32)
        acc, carry = _sb_block(q, jnp.concatenate([kn.astype(BF16), pad], axis=0),
                               jnp.concatenate([vn.astype(BF16), pad], axis=0), carry, tmat, vis_new)
        for b in range(nblk - 1, -1, -1):
            rows = slice(b * C_TK, (b + 1) * C_TK)
            pv, carry = _sb_block(q, kc_ref[h, rows, :].astype(BF16), vc_ref[h, rows, :].astype(BF16), carry, tmat, None)
            acc = acc + pv
        o_ref[:, sl] = (acc * _silu(zg_ref[:, sl])).astype(o_ref.dtype)


def attn_sample(p, cache_k, cache_v, *, hb):
    bsz, nh, past, d = cache_k.shape
    t = p.shape[0] // bsz
    w = nh * d
    ng = nh // hb
    blk = lambda c: pl.BlockSpec((t, hb * d), lambda b, g, c=c: (b, c * ng + g))
    cspec = pl.BlockSpec((None, hb, past, d), lambda b, g: (b, g, 0, 0))
    nspec = pl.BlockSpec((None, hb, t, d), lambda b, g: (b, g, 0, 0))
    tmat = _suffix_sum_matrix()
    return pl.pallas_call(
        functools.partial(_attn_sample_kernel, hb=hb, past=past),
        grid=(bsz, ng),
        in_specs=[blk(0), blk(1), blk(2), blk(3), cspec, cspec, pl.BlockSpec(tmat.shape, lambda b, g: (0, 0))],
        out_specs=[pl.BlockSpec((t, hb * d), lambda b, g: (b, g)), nspec, nspec],
        out_shape=[
            jax.ShapeDtypeStruct((bsz * t, w), BF16),
            jax.ShapeDtypeStruct((bsz, nh, t, d), F32),
            jax.ShapeDtypeStruct((bsz, nh, t, d), F32),
        ],
        compiler_params=_cparams(("parallel", "parallel"), 40),
        name="attn_sample",
    )(p, p, p, p, cache_k, cache_v, tmat)


SSD_L = 128
SSD_HG = 16
SSD_PAIRS = SSD_HG // 2
SSD_W = SSD_HG * B_HEAD_DIM
SSD_GB = 2 * B_STATE
HALO = 8


def _split3(x):
    hi = x.astype(BF16)
    r = x - hi.astype(F32)
    mid = r.astype(BF16)
    lo = (r - mid.astype(F32)).astype(BF16)
    return hi, mid, lo


def _dot_x01(x, m):
    hi, mid, lo = _split3(x)
    d = lambda a: jnp.dot(a, m, preferred_element_type=F32)
    return (d(hi) + d(mid)) + d(lo)


def _dot_01x(m, x):
    hi, mid, lo = _split3(x)
    d = lambda a: jnp.dot(m, a, preferred_element_type=F32)
    return (d(hi) + d(mid)) + d(lo)


def _softplus(x):
    return jnp.maximum(x, 0.0) + jnp.log1p(jnp.exp(-jnp.abs(x)))


def _ssd_kernel(*refs, nseq, sample):
    (z_ref, x_ref, b_ref, c_ref, dt_ref, cwx_ref, cwb_ref, cwc_ref, cbx_ref, cbb_ref, cbc_ref,
     dtb_ref, alog_ref, dl_ref, nw_ref, tri_ref, same_ref, e64_ref, el_ref) = refs[:19]
    refs = refs[19:]
    if sample:
        hx_ref, hb_ref, hc_ref, s0_ref = refs[:4]
        refs = refs[4:]
    (y_ref, so_ref, extx, extb, extc, st_scr, xs_scr, bs_scr, cs_scr, ecx_scr, wex_scr, dcx_scr,
     cc_scr, cumt_scr, dtt_scr, yacc_scr) = refs
    L = SSD_L
    lseq = L // nseq
    c = pl.program_id(1)
    nchunks = pl.num_programs(1)

    if sample:
        for s in range(nseq):
            extx[s, HALO - 3:HALO, :] = hx_ref[s]
            extb[s, HALO - 3:HALO, :] = hb_ref[s]
            extc[s, HALO - 3:HALO, :] = hc_ref[s]
            for pp in range(SSD_PAIRS):
                st_scr[s, pp] = s0_ref[s, pp].T
    else:
        @pl.when(c == 0)
        def _():
            extx[0, 0:HALO, :] = jnp.zeros((HALO, SSD_W), F32)
            extb[0, 0:HALO, :] = jnp.zeros((HALO, SSD_GB), F32)
            extc[0, 0:HALO, :] = jnp.zeros((HALO, SSD_GB), F32)
            st_scr[...] = jnp.zeros_like(st_scr)

    def conv(ext, src_ref, cw_ref, cb_ref, dst):
        for s in range(nseq):
            ext[s, HALO:HALO + lseq, :] = src_ref[s * lseq:(s + 1) * lseq, :]
        for s in range(nseq):
            acc = cb_ref[...]
            for k in range(B_CONV):
                acc = acc + ext[s, HALO - 3 + k:HALO - 3 + k + lseq, :] * cw_ref[k:k + 1, :]
            dst[s * lseq:(s + 1) * lseq, :] = _silu(acc).astype(dst.dtype)
        if not sample:
            ext[0, 0:HALO, :] = ext[0, lseq:lseq + HALO, :]

    conv(extx, x_ref, cwx_ref, cbx_ref, xs_scr)
    conv(extb, b_ref, cwb_ref, cbb_ref, bs_scr)
    conv(extc, c_ref, cwc_ref, cbc_ref, cs_scr)

    dt = _softplus(dt_ref[...] + dtb_ref[...])
    dta = dt * (-jnp.exp(alog_ref[...]))
    cum = _dot_01x(tri_ref[...], dta)
    ctot = _dot_01x(same_ref[...], dta)
    cumt_scr[...] = cum.T
    dtt_scr[...] = dt.T
    e64 = e64_ref[...]
    ecx_scr[...] = _dot_x01(jnp.exp(cum), e64)
    wex_scr[...] = _dot_x01(jnp.exp(ctot - cum) * dt, e64)
    dcx_scr[...] = _dot_x01(jnp.exp(ctot), e64)
    cc_scr[...] = _dot_x01(cum, el_ref[...])
    mask = tri_ref[...] > 0
    lane = lax.broadcasted_iota(jnp.int32, (L, LANES), 1)
    first = lane < B_HEAD_DIM

    for g2 in range(2):
        bg = bs_scr[:, g2 * B_STATE:(g2 + 1) * B_STATE]
        cg = cs_scr[:, g2 * B_STATE:(g2 + 1) * B_STATE]
        cb = lax.dot_general(cg, bg, (((1,), (1,)), ((), ())), preferred_element_type=F32)
        for p in range(SSD_PAIRS // 2):
            pp = g2 * (SSD_PAIRS // 2) + p
            lanes = slice(pp * LANES, (pp + 1) * LANES)
            ms = []
            for r in (2 * pp, 2 * pp + 1):
                seg = cc_scr[:, r * L:(r + 1) * L] - cumt_scr[r:r + 1, :]
                ms.append((cb * jnp.exp(jnp.where(mask, seg, -jnp.inf)) * dtt_scr[r:r + 1, :]).astype(BF16))
            xp = xs_scr[:, lanes]
            xa = jnp.where(first, xp, 0.0).astype(BF16)
            xb = jnp.where(first, 0.0, xp).astype(BF16)
            y = jnp.dot(jnp.concatenate(ms, axis=1), jnp.concatenate([xa, xb], axis=0), preferred_element_type=F32)
            xw = (xp * wex_scr[:, lanes]).astype(BF16)
            ys = []
            for s in range(nseq):
                rows = slice(s * lseq, (s + 1) * lseq)
                st = st_scr[s, pp]
                ys.append(jnp.dot(cg[rows], st.astype(BF16), preferred_element_type=F32))
                upd = lax.dot_general(bg[rows], xw[rows], (((0,), (0,)), ((), ())), preferred_element_type=F32)
                st_scr[s, pp] = st * dcx_scr[s * lseq:s * lseq + 1, lanes] + upd
            ystate = ys[0] if nseq == 1 else jnp.concatenate(ys, axis=0)
            y = y + ystate * ecx_scr[:, lanes] + dl_ref[:, lanes] * xp
            yacc_scr[:, lanes] = y * _silu(z_ref[:, lanes])

    gw = SSD_W // 2
    for g2 in range(2):
        sl = slice(g2 * gw, (g2 + 1) * gw)
        yg = yacc_scr[:, sl]
        ms_ = jnp.mean(yg * yg, axis=-1, keepdims=True)
        y_ref[:, sl] = (yg * lax.rsqrt(ms_ + NORM_EPS) * nw_ref[:, sl]).astype(y_ref.dtype)

    if sample:
        for s in range(nseq):
            for pp in range(SSD_PAIRS):
                so_ref[s, pp] = st_scr[s, pp].T
    else:
        @pl.when(c == nchunks - 1)
        def _():
            for pp in range(SSD_PAIRS):
                so_ref[0, pp] = st_scr[0, pp].T


def _ssd_consts(lseq):
    L = SSD_L
    t = jnp.arange(L)
    same = (t[:, None] // lseq) == (t[None, :] // lseq)
    tri = same & (t[None, :] <= t[:, None])
    r = jnp.arange(LANES)
    e64 = (r[:, None] == (jnp.arange(SSD_W)[None, :] // B_HEAD_DIM)) & (r[:, None] < SSD_HG)
    el = (r[:, None] == (jnp.arange(SSD_HG * L)[None, :] // L)) & (r[:, None] < SSD_HG)
    return tri.astype(BF16), same.astype(BF16), e64.astype(BF16), el.astype(BF16)


def ssd_core(pz, dtp, conv_w, conv_b, dtb, alog, dlane, nw, *, lseq, conv_state=None, ssm_state=None):
    m = pz.shape[0]
    sample = conv_state is not None
    nseq = SSD_L // lseq
    nchunks = m // SSD_L
    nslices = B_HEADS // SSD_HG
    tri, same, e64, el = _ssd_consts(lseq)
    nx = 4096 // SSD_W
    rowblk = lambda w, off: pl.BlockSpec((SSD_L, w), lambda j, c, off=off: (c, off + j))
    parblk = lambda r, w, off: pl.BlockSpec((r, w), lambda j, c, off=off: (0, off + j))
    const = lambda a: pl.BlockSpec(a.shape, lambda j, c: (0, 0))
    ob = 8192 // SSD_GB
    oc = 9216 // SSD_GB
    in_specs = [
        rowblk(SSD_W, 0), rowblk(SSD_W, nx), rowblk(SSD_GB, ob), rowblk(SSD_GB, oc), rowblk(LANES, 0),
        parblk(B_CONV, SSD_W, 0), parblk(B_CONV, SSD_GB, 4096 // SSD_GB), parblk(B_CONV, SSD_GB, 5120 // SSD_GB),
        parblk(1, SSD_W, 0), parblk(1, SSD_GB, 4096 // SSD_GB), parblk(1, SSD_GB, 5120 // SSD_GB),
        parblk(1, LANES, 0), parblk(1, LANES, 0), parblk(1, SSD_W, 0), parblk(1, SSD_W, 0),
        const(tri), const(same), const(e64), const(el),
    ]
    args = [pz, pz, pz, pz, dtp, conv_w, conv_w, conv_w, conv_b, conv_b, conv_b, dtb, alog, dlane, nw, tri, same, e64, el]
    if sample:
        hblk = lambda w, off: pl.BlockSpec((nseq, B_CONV - 1, w), lambda j, c, off=off: (c, 0, off + j))
        in_specs += [hblk(SSD_W, 0), hblk(SSD_GB, 4096 // SSD_GB), hblk(SSD_GB, 5120 // SSD_GB),
                     pl.BlockSpec((nseq, SSD_PAIRS, LANES, B_STATE), lambda j, c: (c, j, 0, 0))]
        args += [conv_state, conv_state, conv_state, ssm_state]
        nstates = m // lseq
        so_spec = pl.BlockSpec((nseq, SSD_PAIRS, LANES, B_STATE), lambda j, c: (c, j, 0, 0))
    else:
        nstates = 1
        so_spec = pl.BlockSpec((1, SSD_PAIRS, LANES, B_STATE), lambda j, c: (0, j, 0, 0))
    L = SSD_L
    scratch = [
        pltpu.VMEM((nseq, HALO + lseq, SSD_W), F32), pltpu.VMEM((nseq, HALO + lseq, SSD_GB), F32),
        pltpu.VMEM((nseq, HALO + lseq, SSD_GB), F32),
        pltpu.VMEM((nseq, SSD_PAIRS, B_STATE, LANES), F32),
        pltpu.VMEM((L, SSD_W), F32), pltpu.VMEM((L, SSD_GB), BF16), pltpu.VMEM((L, SSD_GB), BF16),
        pltpu.VMEM((L, SSD_W), F32), pltpu.VMEM((L, SSD_W), F32), pltpu.VMEM((L, SSD_W), F32),
        pltpu.VMEM((L, SSD_HG * L), F32), pltpu.VMEM((LANES, L), F32), pltpu.VMEM((LANES, L), F32),
        pltpu.VMEM((L, SSD_W), F32),
    ]
    return pl.pallas_call(
        functools.partial(_ssd_kernel, nseq=nseq, sample=sample),
        grid=(nslices, nchunks),
        in_specs=in_specs,
        out_specs=[pl.BlockSpec((SSD_L, SSD_W), lambda j, c: (c, j)), so_spec],
        out_shape=[jax.ShapeDtypeStruct((m, B_HEADS * B_HEAD_DIM), BF16),
                   jax.ShapeDtypeStruct((nstates, B_HEADS // 2, LANES, B_STATE), F32)],
        scratch_shapes=scratch,
        compiler_params=_cparams(("arbitrary", "arbitrary"), 48),
        name="ssd_core",
    )(*args)


def _group_heads(v, fill=0.0):
    lead = v.shape[:-1]
    g = v.reshape(*lead, B_HEADS // SSD_HG, SSD_HG)
    g = jnp.pad(g, [(0, 0)] * (len(lead) + 1) + [(0, LANES - SSD_HG)], constant_values=fill)
    return g.reshape(*lead, (B_HEADS // SSD_HG) * LANES)


N_MIXERS = 3
PROJ_TM = 1024
PROJ_TN = 1024
OUT_TM = 256


def _run_stream(x, lseq, tm, par, state_conv=None, state_ssm=None, cache_k=None, cache_v=None):
    sample = state_conv is not None
    depth = par["norm_w"].shape[0]
    new = {"v": [], "ssm": [], "conv": [], "k": [], "v_attn": []}
    h = rms_norm_bf16(x, par["norm_w"][0], tm=tm)
    for i in range(depth):
        kind, j = i % N_MIXERS, i // N_MIXERS
        if kind == 0:
            p = matmul(h, par["a_w_in"][j], tm=tm, tn=PROJ_TN, name="a_in_proj")
            wpos, bias = _gmlp_pos_params(par["a_w_s"][j], par["a_b_s"][j], min(lseq, A_CHUNK))
            r = gmlp_core(p, wpos, bias, par["a_ln_g"][j], par["a_ln_b"][j], emit_v=sample)
            if sample:
                y, v = r
                new["v"].append(v)
            else:
                y = r
            w_out = par["a_w_out"][j]
        elif kind == 1:
            pz = matmul(h, par["b_w_zxbc"][j], tm=tm, tn=PROJ_TN, name="b_in_proj")
            dtp = matmul(h, par["b_w_dt"][j], tm=tm, tn=par["b_w_dt"][j].shape[1], name="b_dt_proj")
            y, s = ssd_core(
                pz, dtp, par["b_conv_w"][j], par["b_conv_b"][j], par["b_dt_bias"][j], par["b_a_log"][j],
                par["b_d_lane"][j], par["b_norm_w"][j], lseq=min(lseq, SSD_L),
                conv_state=state_conv[j] if sample else None,
                ssm_state=state_ssm[j].reshape(-1, B_HEADS // 2, LANES, B_STATE) if sample else None)
            new["ssm"].append(s.reshape(-1, B_HEADS, B_HEAD_DIM, B_STATE))
            new["conv"].append(pz.reshape(-1, lseq, pz.shape[1])[:, lseq - (B_CONV - 1):, 4096:])
            w_out = par["b_w_out"][j]
        else:
            p = matmul(h, par["c_w_in"][j], tm=tm, tn=PROJ_TN, name="c_in_proj")
            if sample:
                y, k, v = attn_sample(p, cache_k[j], cache_v[j], hb=8)
            else:
                y, k, v = attn_prompt(p, tq=256)
                k, v = k[None], v[None]
            new["k"].append(k)
            new["v_attn"].append(v)
            w_out = par["c_w_out"][j]
        last = i == depth - 1
        nw = par["final_norm_w"] if last else par["norm_w"][i + 1]
        r = matmul_residual_norm(y, w_out, x, nw, tm=OUT_TM, final=last)
        if last:
            return r, new
        x, h = r


def kernel(x_prompt, x_sample, state_ssm, state_conv, cache_k, cache_v, norm_w, final_norm_w, a_w_in, a_ln_g, a_ln_b, a_w_s, a_b_s, a_w_out, b_w_in, b_conv_w, b_conv_b, b_dt_bias, b_a_log, b_d, b_norm_w, b_w_out, c_w_in, c_w_out):
    bp, seq, d = x_prompt.shape
    bs, dseq, _ = x_sample.shape
    nb = b_w_in.shape[0]
    zx = b_w_in.shape[2] - B_HEADS
    par = {
        "norm_w": norm_w, "final_norm_w": final_norm_w,
        "a_w_in": a_w_in.astype(BF16), "a_ln_g": a_ln_g, "a_ln_b": a_ln_b, "a_w_s": a_w_s, "a_b_s": a_b_s,
        "a_w_out": a_w_out.astype(BF16),
        "b_w_zxbc": b_w_in[:, :, :zx].astype(BF16), "b_w_dt": _group_heads(b_w_in[:, :, zx:]).astype(BF16),
        "b_conv_w": b_conv_w, "b_conv_b": b_conv_b.reshape(nb, 1, -1),
        "b_dt_bias": _group_heads(b_dt_bias).reshape(nb, 1, -1), "b_a_log": _group_heads(b_a_log).reshape(nb, 1, -1),
        "b_d_lane": jnp.repeat(b_d, B_HEAD_DIM, axis=-1).reshape(nb, 1, -1), "b_norm_w": b_norm_w.reshape(nb, 1, -1),
        "b_w_out": b_w_out.astype(BF16), "c_w_in": c_w_in.astype(BF16), "c_w_out": c_w_out.astype(BF16),
    }
    assert bp == 1, "the prompt group is one stream"
    yp, newp = _run_stream(x_prompt.reshape(seq, d), seq, min(PROJ_TM, seq), par)
    ys, news = _run_stream(x_sample.reshape(bs * dseq, d), dseq, min(PROJ_TM, bs * dseq), par,
                           state_conv=state_conv, state_ssm=state_ssm, cache_k=cache_k, cache_v=cache_v)
    st = jnp.stack
    return (
        yp.reshape(bp, seq, d),
        ys.reshape(bs, dseq, d),
        st([v.reshape(bs, dseq, -1) for v in news["v"]]),
        st(newp["ssm"]), st(newp["conv"]), st(news["ssm"]), st(news["conv"]),
        st(newp["k"]), st(newp["v_attn"]), st(news["k"]), st(news["v_attn"]),
    )
```

```python
import functools

import jax
import jax.numpy as jnp
from jax import lax
from jax.experimental import pallas as pl
from jax.experimental.pallas import tpu as pltpu

F32 = jnp.float32
BF16 = jnp.bfloat16

NORM_EPS = 1e-6
D_MODEL = 2048
CHUNK = 64
A_GROUPS = 16
A_CHUNK = 128
B_HEADS = 64
B_HEAD_DIM = 64
B_GROUPS = 8
B_STATE = 128
B_CONV = 4
C_HEADS = 16
C_HEAD_DIM = 128

LANES = 128
MIB = 1024 * 1024


def _cparams(sem, vmem_mib):
    return pltpu.CompilerParams(dimension_semantics=sem, vmem_limit_bytes=vmem_mib * MIB)


def _rms_kernel(x_ref, w_ref, h_ref):
    x = x_ref[...]
    ms = jnp.mean(x * x, axis=-1, keepdims=True)
    h_ref[...] = (x * lax.rsqrt(ms + NORM_EPS) * w_ref[...]).astype(h_ref.dtype)


def rms_norm_bf16(x, w, *, tm):
    m, d = x.shape
    return pl.pallas_call(
        _rms_kernel,
        grid=(m // tm,),
        in_specs=[pl.BlockSpec((tm, d), lambda i: (i, 0)), pl.BlockSpec((1, d), lambda i: (0, 0))],
        out_specs=pl.BlockSpec((tm, d), lambda i: (i, 0)),
        out_shape=jax.ShapeDtypeStruct((m, d), BF16),
        compiler_params=_cparams(("parallel",), 32),
        name="rms_norm",
    )(x, w.reshape(1, d))


def _mm_kernel(x_ref, w_ref, o_ref):
    o_ref[...] = jnp.dot(x_ref[...], w_ref[...], preferred_element_type=F32).astype(o_ref.dtype)


def matmul(x, w, *, tm, tn, out_dtype=F32, name="proj"):
    m, k = x.shape
    n = w.shape[1]
    return pl.pallas_call(
        _mm_kernel,
        grid=(m // tm, n // tn),
        in_specs=[pl.BlockSpec((tm, k), lambda i, j: (i, 0)), pl.BlockSpec((k, tn), lambda i, j: (0, j))],
        out_specs=pl.BlockSpec((tm, tn), lambda i, j: (i, j)),
        out_shape=jax.ShapeDtypeStruct((m, n), out_dtype),
        compiler_params=_cparams(("parallel", "parallel"), 48),
        name=name,
    )(x, w)


def _mm_res_norm_kernel(y_ref, w_ref, x_ref, nw_ref, *out_refs, final):
    xn = x_ref[...] + jnp.dot(y_ref[...], w_ref[...], preferred_element_type=F32)
    ms = jnp.mean(xn * xn, axis=-1, keepdims=True)
    h = xn * lax.rsqrt(ms + NORM_EPS) * nw_ref[...]
    if final:
        out_refs[0][...] = h
    else:
        out_refs[0][...] = xn
        out_refs[1][...] = h.astype(BF16)


def matmul_residual_norm(y, w, x, nw, *, tm, final=False, name="out_proj"):
    m, k = y.shape
    d = w.shape[1]
    row = lambda i: (i, 0)
    if final:
        out_shape = jax.ShapeDtypeStruct((m, d), F32)
        out_specs = pl.BlockSpec((tm, d), row)
    else:
        out_shape = (jax.ShapeDtypeStruct((m, d), F32), jax.ShapeDtypeStruct((m, d), BF16))
        out_specs = (pl.BlockSpec((tm, d), row), pl.BlockSpec((tm, d), row))
    return pl.pallas_call(
        functools.partial(_mm_res_norm_kernel, final=final),
        grid=(m // tm,),
        in_specs=[
            pl.BlockSpec((tm, k), row),
            pl.BlockSpec((k, d), lambda i: (0, 0)),
            pl.BlockSpec((tm, d), row),
            pl.BlockSpec((1, d), lambda i: (0, 0)),
        ],
        out_specs=out_specs,
        out_shape=out_shape,
        compiler_params=_cparams(("parallel",), 56),
        name=name,
    )(y, w, x, nw.reshape(1, d))


def _gelu(x):
    return 0.5 * x * (1.0 + lax.erf(x * (2.0 ** -0.5)))


def _silu(x):
    return x * (1.0 / (1.0 + jnp.exp(-x)))


A_GW = 256


def _gmlp_kernel(u_ref, v_ref, z_ref, wpos_ref, bias_ref, g_ref, b_ref, *refs, emit_v):
    if emit_v:
        y_ref, vout_ref, gv_scr, vn_scr = refs
    else:
        y_ref, gv_scr, vn_scr = refs
    rows, width = gv_scr.shape
    ngroups = width // A_GW
    acc = jnp.zeros((rows, LANES), F32)
    for g in range(ngroups):
        sl = slice(g * A_GW, (g + 1) * A_GW)
        gv = _gelu(v_ref[:, sl])
        gv_scr[:, sl] = gv
        acc = acc + gv[:, :LANES] + gv[:, LANES:]
    mean = jnp.sum(acc, axis=-1, keepdims=True) * (1.0 / width)
    acc = jnp.zeros((rows, LANES), F32)
    for g in range(ngroups):
        sl = slice(g * A_GW, (g + 1) * A_GW)
        vc = gv_scr[:, sl] - mean
        sq = vc * vc
        acc = acc + sq[:, :LANES] + sq[:, LANES:]
    var = jnp.sum(acc, axis=-1, keepdims=True) * (1.0 / width)
    rstd = lax.rsqrt(var + NORM_EPS)
    for g in range(ngroups):
        sl = slice(g * A_GW, (g + 1) * A_GW)
        vn = (gv_scr[:, sl] - mean) * rstd * g_ref[:, sl] + b_ref[:, sl]
        if emit_v:
            vout_ref[:, sl] = vn
        vn_scr[:, sl] = vn.astype(BF16)
    for g in range(ngroups):
        sl = slice(g * A_GW, (g + 1) * A_GW)
        s = jnp.dot(wpos_ref[g], vn_scr[:, sl], preferred_element_type=F32) + bias_ref[:, sl]
        y_ref[:, sl] = (_gelu(u_ref[:, sl]) * s * _silu(z_ref[:, sl])).astype(BF16)


def gmlp_core(p, wpos, bias, ln_g, ln_b, *, emit_v):
    m, w3 = p.shape
    w = w3 // 3
    t = A_CHUNK
    col = lambda c: pl.BlockSpec((t, w), lambda i, c=c: (i, c))
    const2 = lambda shape: pl.BlockSpec(shape, lambda i: (0, 0))
    out_shape = [jax.ShapeDtypeStruct((m, w), BF16)]
    out_specs = [pl.BlockSpec((t, w), lambda i: (i, 0))]
    if emit_v:
        out_shape.append(jax.ShapeDtypeStruct((m, w), F32))
        out_specs.append(pl.BlockSpec((t, w), lambda i: (i, 0)))
    res = pl.pallas_call(
        functools.partial(_gmlp_kernel, emit_v=emit_v),
        grid=(m // t,),
        in_specs=[
            col(0), col(1), col(2),
            pl.BlockSpec(wpos.shape, lambda i: (0, 0, 0)),
            const2((t, w)), const2((1, w)), const2((1, w)),
        ],
        out_specs=out_specs,
        out_shape=out_shape,
        scratch_shapes=[pltpu.VMEM((t, w), F32), pltpu.VMEM((t, w), BF16)],
        compiler_params=_cparams(("parallel",), 40),
        name="gmlp_core",
    )(p, p, p, wpos, bias, ln_g.reshape(1, w), ln_b.reshape(1, w))
    return res if emit_v else res[0]


def _gmlp_pos_params(w_s, b_s, lc):
    pos = jnp.arange(lc)
    mask = (pos[None, :] // CHUNK) <= (pos[:, None] // CHUNK)
    wp = jnp.where(mask[None], w_s[:, :lc, :lc], 0.0)
    reps = A_CHUNK // lc
    if reps > 1:
        eye = jnp.eye(reps, dtype=wp.dtype)
        wp = jnp.einsum("ab,gts->gatbs", eye, wp).reshape(w_s.shape[0], A_CHUNK, A_CHUNK)
    bias_t = jnp.tile(b_s[:, :lc].T, (reps, 1))
    bias = jnp.repeat(bias_t, A_GW, axis=1)
    return wp.astype(BF16), bias.astype(F32)


C_TK = 128
C_TQ = 512


def _suffix_sum_matrix():
    j = jnp.arange(C_TK)
    tri = (j[:, None] >= j[None, :]).astype(BF16)
    half = jnp.concatenate([tri, jnp.ones((C_TK, C_TK), BF16)], axis=1)
    return jnp.concatenate([half, half], axis=0)


LOG2E = 1.4426950408889634
C_QSCALE = (C_HEAD_DIM ** -0.5) * LOG2E


def _sb_wide(q, kw, vw, carry, tmat, vis):
    nb = kw.shape[0] // C_TK
    z = lax.dot_general(q, kw, (((1,), (1,)), ((), ())), preferred_element_type=F32)
    nabs = lax.bitcast_convert_type(lax.bitcast_convert_type(z, jnp.uint32) | jnp.uint32(0x80000000), F32)
    sp = jnp.maximum(z, 0.0) + jnp.log2(1.0 + jnp.exp2(nabs))
    if vis is not None:
        sp = jnp.where(vis, sp, 0.0)
    hi = sp.astype(BF16)
    lo = (sp - hi.astype(F32)).astype(BF16)
    cs = []
    for u in range(nb):
        sl = slice(u * C_TK, (u + 1) * C_TK)
        cs.append(jnp.dot(jnp.concatenate([hi[:, sl], lo[:, sl]], axis=1), tmat, preferred_element_type=F32))
    ws = [None] * nb
    for u in range(nb - 1, -1, -1):
        sl = slice(u * C_TK, (u + 1) * C_TK)
        ws[u] = jnp.exp2(z[:, sl] - cs[u][:, :C_TK] - carry)
        carry = carry + cs[u][:, C_TK:]
    w = ws[0] if nb == 1 else jnp.concatenate(ws, axis=1)
    if vis is not None:
        w = jnp.where(vis, w, 0.0)
    pv = jnp.dot(w.astype(BF16), vw, preferred_element_type=F32)
    return pv, carry


def _attn_prompt_kernel(q_ref, k_ref, v_ref, zg_ref, tmat_ref, o_ref, ko_ref, vo_ref,
                        kb_scr, vb_scr, acc_scr, carry_scr, *, tq):
    i = pl.program_id(1)

    @pl.when(i == 0)
    def _():
        kb_scr[...] = k_ref[...].astype(BF16)
        vb_scr[...] = v_ref[...].astype(BF16)

    r0 = pl.multiple_of(i * tq, tq)
    ko_ref[...] = k_ref[pl.ds(r0, tq), :]
    vo_ref[...] = v_ref[pl.ds(r0, tq), :]
    q = (q_ref[...] * C_QSCALE).astype(BF16)
    tmat = tmat_ref[...]

    def run(j, n, vis):
        k0 = pl.multiple_of(j * tq, tq)
        return _sb_wide(q, kb_scr[pl.ds(k0, n * tq), :], vb_scr[pl.ds(k0, n * tq), :], carry_scr[...], tmat, vis)

    def accumulate(j, n):
        pv, carry = run(j, n, None)
        acc_scr[...] += pv
        carry_scr[...] = carry

    carry_scr[...] = jnp.zeros_like(carry_scr)
    qpos = lax.broadcasted_iota(jnp.int32, (tq, tq), 0)
    kpos = lax.broadcasted_iota(jnp.int32, (tq, tq), 1)
    acc_scr[...], carry_scr[...] = run(i, 1, kpos < qpos)

    def body(it, c):
        accumulate(i - 1 - it, 1)
        return c

    lax.fori_loop(0, i, body, 0)
    o_ref[...] = (acc_scr[...] * _silu(zg_ref[...])).astype(o_ref.dtype)


def attn_prompt(p, *, tq):
    seq, w4 = p.shape
    w = w4 // 4
    nh = w // C_HEAD_DIM
    d = C_HEAD_DIM
    blk = lambda c: pl.BlockSpec((tq, d), lambda h, i, c=c: (i, c * nh + h))
    whole = lambda c: pl.BlockSpec((seq, d), lambda h, i, c=c: (0, c * nh + h))
    tmat = _suffix_sum_matrix()
    return pl.pallas_call(
        functools.partial(_attn_prompt_kernel, tq=tq),
        grid=(nh, seq // tq),
        in_specs=[blk(0), whole(1), whole(2), blk(3), pl.BlockSpec(tmat.shape, lambda h, i: (0, 0))],
        out_specs=[
            pl.BlockSpec((tq, d), lambda h, i: (i, h)),
            pl.BlockSpec((None, tq, d), lambda h, i: (h, i, 0)),
            pl.BlockSpec((None, tq, d), lambda h, i: (h, i, 0)),
        ],
        out_shape=[
            jax.ShapeDtypeStruct((seq, w), BF16),
            jax.ShapeDtypeStruct((nh, seq, d), F32),
            jax.ShapeDtypeStruct((nh, seq, d), F32),
        ],
        scratch_shapes=[
            pltpu.VMEM((seq, d), BF16), pltpu.VMEM((seq, d), BF16),
            pltpu.VMEM((tq, d), F32), pltpu.VMEM((tq, C_TK), F32),
        ],
        compiler_params=_cparams(("arbitrary", "arbitrary"), 56),
        name="attn_prompt",
    )(p, p, p, p, tmat)


def _attn_sample_kernel(q_ref, kn_ref, vn_ref, zg_ref, kc_ref, vc_ref, tmat_ref, o_ref, ko_ref, vo_ref, *, hb, past):
    t = q_ref.shape[0]
    d = C_HEAD_DIM
    tmat = tmat_ref[...]
    qpos = lax.broadcasted_iota(jnp.int32, (t, C_TK), 0)
    kpos = lax.broadcasted_iota(jnp.int32, (t, C_TK), 1)
    vis_new = kpos < qpos
    pad = jnp.zeros((C_TK - t, d), BF16)
    for h in range(hb):
        sl = slice(h * d, (h + 1) * d)
        kn = kn_ref[:, sl]
        vn = vn_ref[:, sl]
        ko_ref[h] = kn
        vo_ref[h] = vn
        q = (q_ref[:, sl] * C_QSCALE).astype(BF16)
        carry = jnp.zeros((t, C_TK), F32)
        acc, carry = _sb_wide(q, jnp.concatenate([kn.astype(BF16), pad], axis=0),
                              jnp.concatenate([vn.astype(BF16), pad], axis=0), carry, tmat, vis_new)
        pv, carry = _sb_wide(q, kc_ref[h].astype(BF16), vc_ref[h].astype(BF16), carry, tmat, None)
        acc = acc + pv
        o_ref[:, sl] = (acc * _silu(zg_ref[:, sl])).astype(o_ref.dtype)


def attn_sample(p, cache_k, cache_v, *, hb):
    bsz, nh, past, d = cache_k.shape
    t = p.shape[0] // bsz
    w = nh * d
    ng = nh // hb
    blk = lambda c: pl.BlockSpec((t, hb * d), lambda b, g, c=c: (b, c * ng + g))
    cspec = pl.BlockSpec((None, hb, past, d), lambda b, g: (b, g, 0, 0))
    nspec = pl.BlockSpec((None, hb, t, d), lambda b, g: (b, g, 0, 0))
    tmat = _suffix_sum_matrix()
    return pl.pallas_call(
        functools.partial(_attn_sample_kernel, hb=hb, past=past),
        grid=(bsz, ng),
        in_specs=[blk(0), blk(1), blk(2), blk(3), cspec, cspec, pl.BlockSpec(tmat.shape, lambda b, g: (0, 0))],
        out_specs=[pl.BlockSpec((t, hb * d), lambda b, g: (b, g)), nspec, nspec],
        out_shape=[
            jax.ShapeDtypeStruct((bsz * t, w), BF16),
            jax.ShapeDtypeStruct((bsz, nh, t, d), F32),
            jax.ShapeDtypeStruct((bsz, nh, t, d), F32),
        ],
        compiler_params=_cparams(("parallel", "parallel"), 40),
        name="attn_sample",
    )(p, p, p, p, cache_k, cache_v, tmat)


SSD_L = 128
SSD_HG = 16
SSD_PAIRS = SSD_HG // 2
SSD_W = SSD_HG * B_HEAD_DIM
SSD_GB = 2 * B_STATE
HALO = 8


def _split3(x):
    hi = x.astype(BF16)
    r = x - hi.astype(F32)
    mid = r.astype(BF16)
    lo = (r - mid.astype(F32)).astype(BF16)
    return hi, mid, lo


def _dot_x01(x, m):
    hi, mid, lo = _split3(x)
    d = lambda a: jnp.dot(a, m, preferred_element_type=F32)
    return (d(hi) + d(mid)) + d(lo)


def _dot_01x(m, x):
    hi, mid, lo = _split3(x)
    d = lambda a: jnp.dot(m, a, preferred_element_type=F32)
    return (d(hi) + d(mid)) + d(lo)


def _softplus(x):
    return jnp.maximum(x, 0.0) + jnp.log1p(jnp.exp(-jnp.abs(x)))


def _ssd_kernel(*refs, nseq, sample):
    (z_ref, x_ref, b_ref, c_ref, dt_ref, cwx_ref, cwb_ref, cwc_ref, cbx_ref, cbb_ref, cbc_ref,
     dtb_ref, alog_ref, dl_ref, nw_ref, tri_ref, same_ref, e64_ref, el_ref) = refs[:19]
    refs = refs[19:]
    if sample:
        hx_ref, hb_ref, hc_ref, s0_ref = refs[:4]
        refs = refs[4:]
    (y_ref, so_ref, extx, extb, extc, st_scr, xs_scr, bs_scr, cs_scr, ecx_scr, wex_scr, dcx_scr,
     cc_scr, cumt_scr, dtt_scr, yacc_scr) = refs
    L = SSD_L
    lseq = L // nseq
    c = pl.program_id(1)
    nchunks = pl.num_programs(1)

    if sample:
        for s in range(nseq):
            extx[s, HALO - 3:HALO, :] = hx_ref[s]
            extb[s, HALO - 3:HALO, :] = hb_ref[s]
            extc[s, HALO - 3:HALO, :] = hc_ref[s]
            for pp in range(SSD_PAIRS):
                st_scr[s, pp] = s0_ref[s, pp].T
    else:
        @pl.when(c == 0)
        def _():
            extx[0, 0:HALO, :] = jnp.zeros((HALO, SSD_W), F32)
            extb[0, 0:HALO, :] = jnp.zeros((HALO, SSD_GB), F32)
            extc[0, 0:HALO, :] = jnp.zeros((HALO, SSD_GB), F32)
            st_scr[...] = jnp.zeros_like(st_scr)

    def conv(ext, src_ref, cw_ref, cb_ref, dst):
        for s in range(nseq):
            ext[s, HALO:HALO + lseq, :] = src_ref[s * lseq:(s + 1) * lseq, :]
        for s in range(nseq):
            acc = cb_ref[...]
            for k in range(B_CONV):
                acc = acc + ext[s, HALO - 3 + k:HALO - 3 + k + lseq, :] * cw_ref[k:k + 1, :]
            dst[s * lseq:(s + 1) * lseq, :] = _silu(acc).astype(dst.dtype)
        if not sample:
            ext[0, 0:HALO, :] = ext[0, lseq:lseq + HALO, :]

    conv(extx, x_ref, cwx_ref, cbx_ref, xs_scr)
    conv(extb, b_ref, cwb_ref, cbb_ref, bs_scr)
    conv(extc, c_ref, cwc_ref, cbc_ref, cs_scr)

    dt = _softplus(dt_ref[...] + dtb_ref[...])
    dta = dt * (-jnp.exp(alog_ref[...]))
    cum = _dot_01x(tri_ref[...], dta)
    ctot = _dot_01x(same_ref[...], dta)
    cumt_scr[...] = cum.T
    dtt_scr[...] = dt.T
    e64 = e64_ref[...]
    ecx_scr[...] = _dot_x01(jnp.exp(cum), e64)
    wex_scr[...] = _dot_x01(jnp.exp(ctot - cum) * dt, e64)
    dcx_scr[...] = _dot_x01(jnp.exp(ctot), e64)
    cc_scr[...] = _dot_x01(cum, el_ref[...])
    mask = tri_ref[...] > 0
    lane = lax.broadcasted_iota(jnp.int32, (L, LANES), 1)
    first = lane < B_HEAD_DIM

    for g2 in range(2):
        bg = bs_scr[:, g2 * B_STATE:(g2 + 1) * B_STATE]
        cg = cs_scr[:, g2 * B_STATE:(g2 + 1) * B_STATE]
        cb = lax.dot_general(cg, bg, (((1,), (1,)), ((), ())), preferred_element_type=F32)
        for p in range(SSD_PAIRS // 2):
            pp = g2 * (SSD_PAIRS // 2) + p
            lanes = slice(pp * LANES, (pp + 1) * LANES)
            ms = []
            for r in (2 * pp, 2 * pp + 1):
                seg = cc_scr[:, r * L:(r + 1) * L] - cumt_scr[r:r + 1, :]
                ms.append((cb * jnp.exp(jnp.where(mask, seg, -jnp.inf)) * dtt_scr[r:r + 1, :]).astype(BF16))
            xp = xs_scr[:, lanes]
            xa = jnp.where(first, xp, 0.0).astype(BF16)
            xb = jnp.where(first, 0.0, xp).astype(BF16)
            y = jnp.dot(jnp.concatenate(ms, axis=1), jnp.concatenate([xa, xb], axis=0), preferred_element_type=F32)
            xw = (xp * wex_scr[:, lanes]).astype(BF16)
            ys = []
            for s in range(nseq):
                rows = slice(s * lseq, (s + 1) * lseq)
                st = st_scr[s, pp]
                ys.append(jnp.dot(cg[rows], st.astype(BF16), preferred_element_type=F32))
                upd = lax.dot_general(bg[rows], xw[rows], (((0,), (0,)), ((), ())), preferred_element_type=F32)
                st_scr[s, pp] = st * dcx_scr[s * lseq:s * lseq + 1, lanes] + upd
            ystate = ys[0] if nseq == 1 else jnp.concatenate(ys, axis=0)
            y = y + ystate * ecx_scr[:, lanes] + dl_ref[:, lanes] * xp
            yacc_scr[:, lanes] = y * _silu(z_ref[:, lanes])

    gw = SSD_W // 2
    for g2 in range(2):
        sl = slice(g2 * gw, (g2 + 1) * gw)
        yg = yacc_scr[:, sl]
        ms_ = jnp.mean(yg * yg, axis=-1, keepdims=True)
        y_ref[:, sl] = (yg * lax.rsqrt(ms_ + NORM_EPS) * nw_ref[:, sl]).astype(y_ref.dtype)

    if sample:
        for s in range(nseq):
            for pp in range(SSD_PAIRS):
                so_ref[s, pp] = st_scr[s, pp].T
    else:
        @pl.when(c == nchunks - 1)
        def _():
            for pp in range(SSD_PAIRS):
                so_ref[0, pp] = st_scr[0, pp].T


def _ssd_consts(lseq):
    L = SSD_L
    t = jnp.arange(L)
    same = (t[:, None] // lseq) == (t[None, :] // lseq)
    tri = same & (t[None, :] <= t[:, None])
    r = jnp.arange(LANES)
    e64 = (r[:, None] == (jnp.arange(SSD_W)[None, :] // B_HEAD_DIM)) & (r[:, None] < SSD_HG)
    el = (r[:, None] == (jnp.arange(SSD_HG * L)[None, :] // L)) & (r[:, None] < SSD_HG)
    return tri.astype(BF16), same.astype(BF16), e64.astype(BF16), el.astype(BF16)


def ssd_core(pz, dtp, conv_w, conv_b, dtb, alog, dlane, nw, *, lseq, conv_state=None, ssm_state=None):
    m = pz.shape[0]
    sample = conv_state is not None
    nseq = SSD_L // lseq
    nchunks = m // SSD_L
    nslices = B_HEADS // SSD_HG
    tri, same, e64, el = _ssd_consts(lseq)
    nx = 4096 // SSD_W
    rowblk = lambda w, off: pl.BlockSpec((SSD_L, w), lambda j, c, off=off: (c, off + j))
    parblk = lambda r, w, off: pl.BlockSpec((r, w), lambda j, c, off=off: (0, off + j))
    const = lambda a: pl.BlockSpec(a.shape, lambda j, c: (0, 0))
    ob = 8192 // SSD_GB
    oc = 9216 // SSD_GB
    in_specs = [
        rowblk(SSD_W, 0), rowblk(SSD_W, nx), rowblk(SSD_GB, ob), rowblk(SSD_GB, oc), rowblk(LANES, 0),
        parblk(B_CONV, SSD_W, 0), parblk(B_CONV, SSD_GB, 4096 // SSD_GB), parblk(B_CONV, SSD_GB, 5120 // SSD_GB),
        parblk(1, SSD_W, 0), parblk(1, SSD_GB, 4096 // SSD_GB), parblk(1, SSD_GB, 5120 // SSD_GB),
        parblk(1, LANES, 0), parblk(1, LANES, 0), parblk(1, SSD_W, 0), parblk(1, SSD_W, 0),
        const(tri), const(same), const(e64), const(el),
    ]
    args = [pz, pz, pz, pz, dtp, conv_w, conv_w, conv_w, conv_b, conv_b, conv_b, dtb, alog, dlane, nw, tri, same, e64, el]
    if sample:
        hblk = lambda w, off: pl.BlockSpec((nseq, B_CONV - 1, w), lambda j, c, off=off: (c, 0, off + j))
        in_specs += [hblk(SSD_W, 0), hblk(SSD_GB, 4096 // SSD_GB), hblk(SSD_GB, 5120 // SSD_GB),
                     pl.BlockSpec((nseq, SSD_PAIRS, LANES, B_STATE), lambda j, c: (c, j, 0, 0))]
        args += [conv_state, conv_state, conv_state, ssm_state]
        nstates = m // lseq
        so_spec = pl.BlockSpec((nseq, SSD_PAIRS, LANES, B_STATE), lambda j, c: (c, j, 0, 0))
    else:
        nstates = 1
        so_spec = pl.BlockSpec((1, SSD_PAIRS, LANES, B_STATE), lambda j, c: (0, j, 0, 0))
    L = SSD_L
    scratch = [
        pltpu.VMEM((nseq, HALO + lseq, SSD_W), F32), pltpu.VMEM((nseq, HALO + lseq, SSD_GB), F32),
        pltpu.VMEM((nseq, HALO + lseq, SSD_GB), F32),
        pltpu.VMEM((nseq, SSD_PAIRS, B_STATE, LANES), F32),
        pltpu.VMEM((L, SSD_W), F32), pltpu.VMEM((L, SSD_GB), BF16), pltpu.VMEM((L, SSD_GB), BF16),
        pltpu.VMEM((L, SSD_W), F32), pltpu.VMEM((L, SSD_W), F32), pltpu.VMEM((L, SSD_W), F32),
        pltpu.VMEM((L, SSD_HG * L), F32), pltpu.VMEM((LANES, L), F32), pltpu.VMEM((LANES, L), F32),
        pltpu.VMEM((L, SSD_W), F32),
    ]
    return pl.pallas_call(
        functools.partial(_ssd_kernel, nseq=nseq, sample=sample),
        grid=(nslices, nchunks),
        in_specs=in_specs,
        out_specs=[pl.BlockSpec((SSD_L, SSD_W), lambda j, c: (c, j)), so_spec],
        out_shape=[jax.ShapeDtypeStruct((m, B_HEADS * B_HEAD_DIM), BF16),
                   jax.ShapeDtypeStruct((nstates, B_HEADS // 2, LANES, B_STATE), F32)],
        scratch_shapes=scratch,
        compiler_params=_cparams(("arbitrary", "arbitrary"), 48),
        name="ssd_core",
    )(*args)


def _group_heads(v, fill=0.0):
    lead = v.shape[:-1]
    g = v.reshape(*lead, B_HEADS // SSD_HG, SSD_HG)
    g = jnp.pad(g, [(0, 0)] * (len(lead) + 1) + [(0, LANES - SSD_HG)], constant_values=fill)
    return g.reshape(*lead, (B_HEADS // SSD_HG) * LANES)


N_MIXERS = 3
PROJ_TM = 1024
PROJ_TN = 1024
OUT_TM = 256


def _run_stream(x, lseq, tm, par, state_conv=None, state_ssm=None, cache_k=None, cache_v=None):
    sample = state_conv is not None
    depth = par["norm_w"].shape[0]
    new = {"v": [], "ssm": [], "conv": [], "k": [], "v_attn": []}
    h = rms_norm_bf16(x, par["norm_w"][0], tm=tm)
    for i in range(depth):
        kind, j = i % N_MIXERS, i // N_MIXERS
        if kind == 0:
            p = matmul(h, par["a_w_in"][j], tm=tm, tn=PROJ_TN, name="a_in_proj")
            wpos, bias = _gmlp_pos_params(par["a_w_s"][j], par["a_b_s"][j], min(lseq, A_CHUNK))
            r = gmlp_core(p, wpos, bias, par["a_ln_g"][j], par["a_ln_b"][j], emit_v=sample)
            if sample:
                y, v = r
                new["v"].append(v)
            else:
                y = r
            w_out = par["a_w_out"][j]
        elif kind == 1:
            pz = matmul(h, par["b_w_zxbc"][j], tm=tm, tn=PROJ_TN, name="b_in_proj")
            dtp = matmul(h, par["b_w_dt"][j], tm=tm, tn=par["b_w_dt"][j].shape[1], name="b_dt_proj")
            y, s = ssd_core(
                pz, dtp, par["b_conv_w"][j], par["b_conv_b"][j], par["b_dt_bias"][j], par["b_a_log"][j],
                par["b_d_lane"][j], par["b_norm_w"][j], lseq=min(lseq, SSD_L),
                conv_state=state_conv[j] if sample else None,
                ssm_state=state_ssm[j].reshape(-1, B_HEADS // 2, LANES, B_STATE) if sample else None)
            new["ssm"].append(s.reshape(-1, B_HEADS, B_HEAD_DIM, B_STATE))
            new["conv"].append(pz.reshape(-1, lseq, pz.shape[1])[:, lseq - (B_CONV - 1):, 4096:])
            w_out = par["b_w_out"][j]
        else:
            p = matmul(h, par["c_w_in"][j], tm=tm, tn=PROJ_TN, name="c_in_proj")
            if sample:
                y, k, v = attn_sample(p, cache_k[j], cache_v[j], hb=8)
            else:
                y, k, v = attn_prompt(p, tq=min(C_TQ, p.shape[0]))
                k, v = k[None], v[None]
            new["k"].append(k)
            new["v_attn"].append(v)
            w_out = par["c_w_out"][j]
        last = i == depth - 1
        nw = par["final_norm_w"] if last else par["norm_w"][i + 1]
        r = matmul_residual_norm(y, w_out, x, nw, tm=OUT_TM, final=last)
        if last:
            return r, new
        x, h = r


def kernel(x_prompt, x_sample, state_ssm, state_conv, cache_k, cache_v, norm_w, final_norm_w, a_w_in, a_ln_g, a_ln_b, a_w_s, a_b_s, a_w_out, b_w_in, b_conv_w, b_conv_b, b_dt_bias, b_a_log, b_d, b_norm_w, b_w_out, c_w_in, c_w_out):
    bp, seq, d = x_prompt.shape
    bs, dseq, _ = x_sample.shape
    nb = b_w_in.shape[0]
    zx = b_w_in.shape[2] - B_HEADS
    par = {
        "norm_w": norm_w, "final_norm_w": final_norm_w,
        "a_w_in": a_w_in.astype(BF16), "a_ln_g": a_ln_g, "a_ln_b": a_ln_b, "a_w_s": a_w_s, "a_b_s": a_b_s,
        "a_w_out": a_w_out.astype(BF16),
        "b_w_zxbc": b_w_in[:, :, :zx].astype(BF16), "b_w_dt": _group_heads(b_w_in[:, :, zx:]).astype(BF16),
        "b_conv_w": b_conv_w, "b_conv_b": b_conv_b.reshape(nb, 1, -1),
        "b_dt_bias": _group_heads(b_dt_bias).reshape(nb, 1, -1), "b_a_log": _group_heads(b_a_log).reshape(nb, 1, -1),
        "b_d_lane": jnp.repeat(b_d, B_HEAD_DIM, axis=-1).reshape(nb, 1, -1), "b_norm_w": b_norm_w.reshape(nb, 1, -1),
        "b_w_out": b_w_out.astype(BF16), "c_w_in": c_w_in.astype(BF16), "c_w_out": c_w_out.astype(BF16),
    }
    assert bp == 1, "the prompt group is one stream"
    yp, newp = _run_stream(x_prompt.reshape(seq, d), seq, min(PROJ_TM, seq), par)
    ys, news = _run_stream(x_sample.reshape(bs * dseq, d), dseq, min(PROJ_TM, bs * dseq), par,
                           state_conv=state_conv, state_ssm=state_ssm, cache_k=cache_k, cache_v=cache_v)
    st = jnp.stack
    return (
        yp.reshape(bp, seq, d),
        ys.reshape(bs, dseq, d),
        st([v.reshape(bs, dseq, -1) for v in news["v"]]),
        st(newp["ssm"]), st(newp["conv"]), st(news["ssm"]), st(news["conv"]),
        st(newp["k"]), st(newp["v_attn"]), st(news["k"]), st(news["v_attn"]),
    )
```

```python
import functools

import jax
import jax.numpy as jnp
from jax import lax
from jax.experimental import pallas as pl
from jax.experimental.pallas import tpu as pltpu

F32 = jnp.float32
BF16 = jnp.bfloat16

NORM_EPS = 1e-6
D_MODEL = 2048
CHUNK = 64
A_GROUPS = 16
A_CHUNK = 128
B_HEADS = 64
B_HEAD_DIM = 64
B_GROUPS = 8
B_STATE = 128
B_CONV = 4
C_HEADS = 16
C_HEAD_DIM = 128

LANES = 128
MIB = 1024 * 1024


def _cparams(sem, vmem_mib):
    return pltpu.CompilerParams(dimension_semantics=sem, vmem_limit_bytes=vmem_mib * MIB)


def _rms_kernel(x_ref, w_ref, h_ref):
    x = x_ref[...]
    ms = jnp.mean(x * x, axis=-1, keepdims=True)
    h_ref[...] = (x * lax.rsqrt(ms + NORM_EPS) * w_ref[...]).astype(h_ref.dtype)


def rms_norm_bf16(x, w, *, tm):
    m, d = x.shape
    return pl.pallas_call(
        _rms_kernel,
        grid=(m // tm,),
        in_specs=[pl.BlockSpec((tm, d), lambda i: (i, 0)), pl.BlockSpec((1, d), lambda i: (0, 0))],
        out_specs=pl.BlockSpec((tm, d), lambda i: (i, 0)),
        out_shape=jax.ShapeDtypeStruct((m, d), BF16),
        compiler_params=_cparams(("parallel",), 32),
        name="rms_norm",
    )(x, w.reshape(1, d))


def _mm_kernel(x_ref, w_ref, o_ref):
    o_ref[...] = jnp.dot(x_ref[...], w_ref[...], preferred_element_type=F32).astype(o_ref.dtype)


def matmul(x, w, *, tm, tn, out_dtype=F32, name="proj"):
    m, k = x.shape
    n = w.shape[1]
    return pl.pallas_call(
        _mm_kernel,
        grid=(m // tm, n // tn),
        in_specs=[pl.BlockSpec((tm, k), lambda i, j: (i, 0)), pl.BlockSpec((k, tn), lambda i, j: (0, j))],
        out_specs=pl.BlockSpec((tm, tn), lambda i, j: (i, j)),
        out_shape=jax.ShapeDtypeStruct((m, n), out_dtype),
        compiler_params=_cparams(("parallel", "parallel"), 48),
        name=name,
    )(x, w)


def _mm_res_norm_kernel(y_ref, w_ref, x_ref, nw_ref, *out_refs, final):
    xn = x_ref[...] + jnp.dot(y_ref[...], w_ref[...], preferred_element_type=F32)
    ms = jnp.mean(xn * xn, axis=-1, keepdims=True)
    h = xn * lax.rsqrt(ms + NORM_EPS) * nw_ref[...]
    if final:
        out_refs[0][...] = h
    else:
        out_refs[0][...] = xn
        out_refs[1][...] = h.astype(BF16)


def matmul_residual_norm(y, w, x, nw, *, tm, final=False, name="out_proj"):
    m, k = y.shape
    d = w.shape[1]
    row = lambda i: (i, 0)
    if final:
        out_shape = jax.ShapeDtypeStruct((m, d), F32)
        out_specs = pl.BlockSpec((tm, d), row)
    else:
        out_shape = (jax.ShapeDtypeStruct((m, d), F32), jax.ShapeDtypeStruct((m, d), BF16))
        out_specs = (pl.BlockSpec((tm, d), row), pl.BlockSpec((tm, d), row))
    return pl.pallas_call(
        functools.partial(_mm_res_norm_kernel, final=final),
        grid=(m // tm,),
        in_specs=[
            pl.BlockSpec((tm, k), row),
            pl.BlockSpec((k, d), lambda i: (0, 0)),
            pl.BlockSpec((tm, d), row),
            pl.BlockSpec((1, d), lambda i: (0, 0)),
        ],
        out_specs=out_specs,
        out_shape=out_shape,
        compiler_params=_cparams(("parallel",), 56),
        name=name,
    )(y, w, x, nw.reshape(1, d))


def _gelu(x):
    return 0.5 * x * (1.0 + lax.erf(x * (2.0 ** -0.5)))


def _silu(x):
    return x * (1.0 / (1.0 + jnp.exp(-x)))


A_GW = 256


def _gmlp_kernel(u_ref, v_ref, z_ref, wpos_ref, bias_ref, g_ref, b_ref, *refs, emit_v):
    if emit_v:
        y_ref, vout_ref, gv_scr, vn_scr = refs
    else:
        y_ref, gv_scr, vn_scr = refs
    rows, width = gv_scr.shape
    ngroups = width // A_GW
    acc = jnp.zeros((rows, LANES), F32)
    for g in range(ngroups):
        sl = slice(g * A_GW, (g + 1) * A_GW)
        gv = _gelu(v_ref[:, sl])
        gv_scr[:, sl] = gv
        acc = acc + gv[:, :LANES] + gv[:, LANES:]
    mean = jnp.sum(acc, axis=-1, keepdims=True) * (1.0 / width)
    acc = jnp.zeros((rows, LANES), F32)
    for g in range(ngroups):
        sl = slice(g * A_GW, (g + 1) * A_GW)
        vc = gv_scr[:, sl] - mean
        sq = vc * vc
        acc = acc + sq[:, :LANES] + sq[:, LANES:]
    var = jnp.sum(acc, axis=-1, keepdims=True) * (1.0 / width)
    rstd = lax.rsqrt(var + NORM_EPS)
    for g in range(ngroups):
        sl = slice(g * A_GW, (g + 1) * A_GW)
        vn = (gv_scr[:, sl] - mean) * rstd * g_ref[:, sl] + b_ref[:, sl]
        if emit_v:
            vout_ref[:, sl] = vn
        vn_scr[:, sl] = vn.astype(BF16)
    for g in range(ngroups):
        sl = slice(g * A_GW, (g + 1) * A_GW)
        s = jnp.dot(wpos_ref[g], vn_scr[:, sl], preferred_element_type=F32) + bias_ref[:, sl]
        y_ref[:, sl] = (_gelu(u_ref[:, sl]) * s * _silu(z_ref[:, sl])).astype(BF16)


def gmlp_core(p, wpos, bias, ln_g, ln_b, *, emit_v):
    m, w3 = p.shape
    w = w3 // 3
    t = A_CHUNK
    col = lambda c: pl.BlockSpec((t, w), lambda i, c=c: (i, c))
    const2 = lambda shape: pl.BlockSpec(shape, lambda i: (0, 0))
    out_shape = [jax.ShapeDtypeStruct((m, w), BF16)]
    out_specs = [pl.BlockSpec((t, w), lambda i: (i, 0))]
    if emit_v:
        out_shape.append(jax.ShapeDtypeStruct((m, w), F32))
        out_specs.append(pl.BlockSpec((t, w), lambda i: (i, 0)))
    res = pl.pallas_call(
        functools.partial(_gmlp_kernel, emit_v=emit_v),
        grid=(m // t,),
        in_specs=[
            col(0), col(1), col(2),
            pl.BlockSpec(wpos.shape, lambda i: (0, 0, 0)),
            const2((t, w)), const2((1, w)), const2((1, w)),
        ],
        out_specs=out_specs,
        out_shape=out_shape,
        scratch_shapes=[pltpu.VMEM((t, w), F32), pltpu.VMEM((t, w), BF16)],
        compiler_params=_cparams(("parallel",), 40),
        name="gmlp_core",
    )(p, p, p, wpos, bias, ln_g.reshape(1, w), ln_b.reshape(1, w))
    return res if emit_v else res[0]


def _gmlp_pos_params(w_s, b_s, lc):
    pos = jnp.arange(lc)
    mask = (pos[None, :] // CHUNK) <= (pos[:, None] // CHUNK)
    wp = jnp.where(mask[None], w_s[:, :lc, :lc], 0.0)
    reps = A_CHUNK // lc
    if reps > 1:
        eye = jnp.eye(reps, dtype=wp.dtype)
        wp = jnp.einsum("ab,gts->gatbs", eye, wp).reshape(w_s.shape[0], A_CHUNK, A_CHUNK)
    bias_t = jnp.tile(b_s[:, :lc].T, (reps, 1))
    bias = jnp.repeat(bias_t, A_GW, axis=1)
    return wp.astype(BF16), bias.astype(F32)


C_TK = 128
C_TQ = 512


def _suffix_sum_matrix():
    j = jnp.arange(C_TK)
    tri = (j[:, None] >= j[None, :]).astype(BF16)
    half = jnp.concatenate([tri, jnp.ones((C_TK, C_TK), BF16)], axis=1)
    return jnp.concatenate([half, half], axis=0)


LOG2E = 1.4426950408889634
C_QSCALE = (C_HEAD_DIM ** -0.5) * LOG2E


def _sb_scores(q, kw):
    return lax.dot_general(q, kw, (((1,), (1,)), ((), ())), preferred_element_type=F32)


def _sb_weights(z, carry, tmat, vis):
    nb = z.shape[1] // C_TK
    nabs = lax.bitcast_convert_type(lax.bitcast_convert_type(z, jnp.uint32) | jnp.uint32(0x80000000), F32)
    sp = jnp.maximum(z, 0.0) + jnp.log2(1.0 + jnp.exp2(nabs))
    if vis is not None:
        sp = jnp.where(vis, sp, 0.0)
    hi = sp.astype(BF16)
    lo = (sp - hi.astype(F32)).astype(BF16)
    cs = [None] * nb
    for u in range(nb - 1, -1, -1):
        sl = slice(u * C_TK, (u + 1) * C_TK)
        cs[u] = jnp.dot(jnp.concatenate([hi[:, sl], lo[:, sl]], axis=1), tmat, preferred_element_type=F32)
    ws = [None] * nb
    for u in range(nb - 1, -1, -1):
        sl = slice(u * C_TK, (u + 1) * C_TK)
        ws[u] = jnp.exp2(z[:, sl] - cs[u][:, :C_TK] - carry)
        carry = carry + cs[u][:, C_TK:]
    w = ws[0] if nb == 1 else jnp.concatenate(ws, axis=1)
    if vis is not None:
        w = jnp.where(vis, w, 0.0)
    return w.astype(BF16), carry


def _sb_wide(q, kw, vw, carry, tmat, vis):
    w, carry = _sb_weights(_sb_scores(q, kw), carry, tmat, vis)
    return jnp.dot(w, vw, preferred_element_type=F32), carry


def _attn_prompt_kernel(q_ref, k_ref, v_ref, zg_ref, tmat_ref, o_ref, ko_ref, vo_ref,
                        kb_scr, vb_scr, acc_scr, carry_scr, z_scr, w_scr, *, tq):
    i = pl.program_id(1)

    @pl.when(i == 0)
    def _():
        kb_scr[...] = k_ref[...].astype(BF16)
        vb_scr[...] = v_ref[...].astype(BF16)

    r0 = pl.multiple_of(i * tq, tq)
    ko_ref[...] = k_ref[pl.ds(r0, tq), :]
    vo_ref[...] = v_ref[pl.ds(r0, tq), :]
    q = (q_ref[...] * C_QSCALE).astype(BF16)
    tmat = tmat_ref[...]

    def rows(j):
        return pl.ds(pl.multiple_of(jnp.maximum(j, 0) * tq, tq), tq)

    def stage(j, cur):
        nxt = 1 - cur
        z_scr[nxt] = _sb_scores(q, kb_scr[rows(j - 1), :])
        w_scr[nxt], carry_scr[...] = _sb_weights(z_scr[cur], carry_scr[...], tmat, None)
        acc_scr[...] += jnp.dot(w_scr[cur], vb_scr[rows(j + 1), :], preferred_element_type=F32)

    def finish(cur):
        acc = acc_scr[...] + jnp.dot(w_scr[cur], vb_scr[rows(0), :], preferred_element_type=F32)
        o_ref[...] = (acc * _silu(zg_ref[...])).astype(o_ref.dtype)

    qpos = lax.broadcasted_iota(jnp.int32, (tq, tq), 0)
    kpos = lax.broadcasted_iota(jnp.int32, (tq, tq), 1)
    w_scr[0], carry_scr[...] = _sb_weights(
        _sb_scores(q, kb_scr[rows(i), :]), jnp.zeros((tq, C_TK), F32), tmat, kpos < qpos)
    z_scr[0] = _sb_scores(q, kb_scr[rows(i - 1), :])
    acc_scr[...] = jnp.zeros_like(acc_scr)

    def body(it, c):
        j = i - 1 - 2 * it
        stage(j, 0)
        stage(j - 1, 1)
        return c

    lax.fori_loop(0, i // 2, body, 0)

    @pl.when(i % 2 == 1)
    def _():
        stage(0, 0)
        finish(1)

    @pl.when(i % 2 == 0)
    def _():
        finish(0)


def attn_prompt(p, *, tq):
    seq, w4 = p.shape
    w = w4 // 4
    nh = w // C_HEAD_DIM
    d = C_HEAD_DIM
    blk = lambda c: pl.BlockSpec((tq, d), lambda h, i, c=c: (i, c * nh + h))
    whole = lambda c: pl.BlockSpec((seq, d), lambda h, i, c=c: (0, c * nh + h))
    tmat = _suffix_sum_matrix()
    return pl.pallas_call(
        functools.partial(_attn_prompt_kernel, tq=tq),
        grid=(nh, seq // tq),
        in_specs=[blk(0), whole(1), whole(2), blk(3), pl.BlockSpec(tmat.shape, lambda h, i: (0, 0))],
        out_specs=[
            pl.BlockSpec((tq, d), lambda h, i: (i, h)),
            pl.BlockSpec((None, tq, d), lambda h, i: (h, i, 0)),
            pl.BlockSpec((None, tq, d), lambda h, i: (h, i, 0)),
        ],
        out_shape=[
            jax.ShapeDtypeStruct((seq, w), BF16),
            jax.ShapeDtypeStruct((nh, seq, d), F32),
            jax.ShapeDtypeStruct((nh, seq, d), F32),
        ],
        scratch_shapes=[
            pltpu.VMEM((seq, d), BF16), pltpu.VMEM((seq, d), BF16),
            pltpu.VMEM((tq, d), F32), pltpu.VMEM((tq, C_TK), F32),
            pltpu.VMEM((2, tq, tq), F32), pltpu.VMEM((2, tq, tq), BF16),
        ],
        compiler_params=_cparams(("arbitrary", "arbitrary"), 56),
        name="attn_prompt",
    )(p, p, p, p, tmat)


def _attn_sample_kernel(q_ref, kn_ref, vn_ref, zg_ref, kc_ref, vc_ref, tmat_ref, o_ref, ko_ref, vo_ref, *, hb, past):
    t = q_ref.shape[0]
    d = C_HEAD_DIM
    tmat = tmat_ref[...]
    qpos = lax.broadcasted_iota(jnp.int32, (t, C_TK), 0)
    kpos = lax.broadcasted_iota(jnp.int32, (t, C_TK), 1)
    vis_new = kpos < qpos
    pad = jnp.zeros((C_TK - t, d), BF16)
    for h in range(hb):
        sl = slice(h * d, (h + 1) * d)
        kn = kn_ref[:, sl]
        vn = vn_ref[:, sl]
        ko_ref[h] = kn
        vo_ref[h] = vn
        q = (q_ref[:, sl] * C_QSCALE).astype(BF16)
        carry = jnp.zeros((t, C_TK), F32)
        acc, carry = _sb_wide(q, jnp.concatenate([kn.astype(BF16), pad], axis=0),
                              jnp.concatenate([vn.astype(BF16), pad], axis=0), carry, tmat, vis_new)
        pv, carry = _sb_wide(q, kc_ref[h].astype(BF16), vc_ref[h].astype(BF16), carry, tmat, None)
        acc = acc + pv
        o_ref[:, sl] = (acc * _silu(zg_ref[:, sl])).astype(o_ref.dtype)


def attn_sample(p, cache_k, cache_v, *, hb):
    bsz, nh, past, d = cache_k.shape
    t = p.shape[0] // bsz
    w = nh * d
    ng = nh // hb
    blk = lambda c: pl.BlockSpec((t, hb * d), lambda b, g, c=c: (b, c * ng + g))
    cspec = pl.BlockSpec((None, hb, past, d), lambda b, g: (b, g, 0, 0))
    nspec = pl.BlockSpec((None, hb, t, d), lambda b, g: (b, g, 0, 0))
    tmat = _suffix_sum_matrix()
    return pl.pallas_call(
        functools.partial(_attn_sample_kernel, hb=hb, past=past),
        grid=(bsz, ng),
        in_specs=[blk(0), blk(1), blk(2), blk(3), cspec, cspec, pl.BlockSpec(tmat.shape, lambda b, g: (0, 0))],
        out_specs=[pl.BlockSpec((t, hb * d), lambda b, g: (b, g)), nspec, nspec],
        out_shape=[
            jax.ShapeDtypeStruct((bsz * t, w), BF16),
            jax.ShapeDtypeStruct((bsz, nh, t, d), F32),
            jax.ShapeDtypeStruct((bsz, nh, t, d), F32),
        ],
        compiler_params=_cparams(("parallel", "parallel"), 40),
        name="attn_sample",
    )(p, p, p, p, cache_k, cache_v, tmat)


SSD_L = 128
SSD_HG = 16
SSD_PAIRS = SSD_HG // 2
SSD_W = SSD_HG * B_HEAD_DIM
SSD_GB = 2 * B_STATE
HALO = 8


def _split3(x):
    hi = x.astype(BF16)
    r = x - hi.astype(F32)
    mid = r.astype(BF16)
    lo = (r - mid.astype(F32)).astype(BF16)
    return hi, mid, lo


def _dot_x01(x, m):
    hi, mid, lo = _split3(x)
    d = lambda a: jnp.dot(a, m, preferred_element_type=F32)
    return (d(hi) + d(mid)) + d(lo)


def _dot_01x(m, x):
    hi, mid, lo = _split3(x)
    d = lambda a: jnp.dot(m, a, preferred_element_type=F32)
    return (d(hi) + d(mid)) + d(lo)


def _softplus(x):
    return jnp.maximum(x, 0.0) + jnp.log1p(jnp.exp(-jnp.abs(x)))


def _ssd_kernel(*refs, nseq, sample):
    (z_ref, x_ref, b_ref, c_ref, dt_ref, cwx_ref, cwb_ref, cwc_ref, cbx_ref, cbb_ref, cbc_ref,
     dtb_ref, alog_ref, dl_ref, nw_ref, tri_ref, same_ref, e64_ref, el_ref) = refs[:19]
    refs = refs[19:]
    if sample:
        hx_ref, hb_ref, hc_ref, s0_ref = refs[:4]
        refs = refs[4:]
    (y_ref, so_ref, extx, extb, extc, st_scr, xs_scr, bs_scr, cs_scr, ecx_scr, wex_scr, dcx_scr,
     cc_scr, cumt_scr, dtt_scr, yacc_scr) = refs
    L = SSD_L
    lseq = L // nseq
    c = pl.program_id(1)
    nchunks = pl.num_programs(1)

    if sample:
        for s in range(nseq):
            extx[s, HALO - 3:HALO, :] = hx_ref[s]
            extb[s, HALO - 3:HALO, :] = hb_ref[s]
            extc[s, HALO - 3:HALO, :] = hc_ref[s]
            for pp in range(SSD_PAIRS):
                st_scr[s, pp] = s0_ref[s, pp].T
    else:
        @pl.when(c == 0)
        def _():
            extx[0, 0:HALO, :] = jnp.zeros((HALO, SSD_W), F32)
            extb[0, 0:HALO, :] = jnp.zeros((HALO, SSD_GB), F32)
            extc[0, 0:HALO, :] = jnp.zeros((HALO, SSD_GB), F32)
            st_scr[...] = jnp.zeros_like(st_scr)

    def conv(ext, src_ref, cw_ref, cb_ref, dst):
        for s in range(nseq):
            ext[s, HALO:HALO + lseq, :] = src_ref[s * lseq:(s + 1) * lseq, :]
        for s in range(nseq):
            acc = cb_ref[...]
            for k in range(B_CONV):
                acc = acc + ext[s, HALO - 3 + k:HALO - 3 + k + lseq, :] * cw_ref[k:k + 1, :]
            dst[s * lseq:(s + 1) * lseq, :] = _silu(acc).astype(dst.dtype)
        if not sample:
            ext[0, 0:HALO, :] = ext[0, lseq:lseq + HALO, :]

    conv(extx, x_ref, cwx_ref, cbx_ref, xs_scr)
    conv(extb, b_ref, cwb_ref, cbb_ref, bs_scr)
    conv(extc, c_ref, cwc_ref, cbc_ref, cs_scr)

    dt = _softplus(dt_ref[...] + dtb_ref[...])
    dta = dt * (-jnp.exp(alog_ref[...]))
    cum = _dot_01x(tri_ref[...], dta)
    ctot = _dot_01x(same_ref[...], dta)
    cumt_scr[...] = cum.T
    dtt_scr[...] = dt.T
    e64 = e64_ref[...]
    ecx_scr[...] = _dot_x01(jnp.exp(cum), e64)
    wex_scr[...] = _dot_x01(jnp.exp(ctot - cum) * dt, e64)
    dcx_scr[...] = _dot_x01(jnp.exp(ctot), e64)
    cc_scr[...] = _dot_x01(cum, el_ref[...])
    mask = tri_ref[...] > 0
    lane = lax.broadcasted_iota(jnp.int32, (L, LANES), 1)
    first = lane < B_HEAD_DIM

    for g2 in range(2):
        bg = bs_scr[:, g2 * B_STATE:(g2 + 1) * B_STATE]
        cg = cs_scr[:, g2 * B_STATE:(g2 + 1) * B_STATE]
        cb = lax.dot_general(cg, bg, (((1,), (1,)), ((), ())), preferred_element_type=F32)
        for p in range(SSD_PAIRS // 2):
            pp = g2 * (SSD_PAIRS // 2) + p
            lanes = slice(pp * LANES, (pp + 1) * LANES)
            ms = []
            for r in (2 * pp, 2 * pp + 1):
                seg = cc_scr[:, r * L:(r + 1) * L] - cumt_scr[r:r + 1, :]
                ms.append((cb * jnp.exp(jnp.where(mask, seg, -jnp.inf)) * dtt_scr[r:r + 1, :]).astype(BF16))
            xp = xs_scr[:, lanes]
            xa = jnp.where(first, xp, 0.0).astype(BF16)
            xb = jnp.where(first, 0.0, xp).astype(BF16)
            y = jnp.dot(jnp.concatenate(ms, axis=1), jnp.concatenate([xa, xb], axis=0), preferred_element_type=F32)
            xw = (xp * wex_scr[:, lanes]).astype(BF16)
            ys = []
            for s in range(nseq):
                rows = slice(s * lseq, (s + 1) * lseq)
                st = st_scr[s, pp]
                ys.append(jnp.dot(cg[rows], st.astype(BF16), preferred_element_type=F32))
                upd = lax.dot_general(bg[rows], xw[rows], (((0,), (0,)), ((), ())), preferred_element_type=F32)
                st_scr[s, pp] = st * dcx_scr[s * lseq:s * lseq + 1, lanes] + upd
            ystate = ys[0] if nseq == 1 else jnp.concatenate(ys, axis=0)
            y = y + ystate * ecx_scr[:, lanes] + dl_ref[:, lanes] * xp
            yacc_scr[:, lanes] = y * _silu(z_ref[:, lanes])

    gw = SSD_W // 2
    for g2 in range(2):
        sl = slice(g2 * gw, (g2 + 1) * gw)
        yg = yacc_scr[:, sl]
        ms_ = jnp.mean(yg * yg, axis=-1, keepdims=True)
        y_ref[:, sl] = (yg * lax.rsqrt(ms_ + NORM_EPS) * nw_ref[:, sl]).astype(y_ref.dtype)

    if sample:
        for s in range(nseq):
            for pp in range(SSD_PAIRS):
                so_ref[s, pp] = st_scr[s, pp].T
    else:
        @pl.when(c == nchunks - 1)
        def _():
            for pp in range(SSD_PAIRS):
                so_ref[0, pp] = st_scr[0, pp].T


def _ssd_consts(lseq):
    L = SSD_L
    t = jnp.arange(L)
    same = (t[:, None] // lseq) == (t[None, :] // lseq)
    tri = same & (t[None, :] <= t[:, None])
    r = jnp.arange(LANES)
    e64 = (r[:, None] == (jnp.arange(SSD_W)[None, :] // B_HEAD_DIM)) & (r[:, None] < SSD_HG)
    el = (r[:, None] == (jnp.arange(SSD_HG * L)[None, :] // L)) & (r[:, None] < SSD_HG)
    return tri.astype(BF16), same.astype(BF16), e64.astype(BF16), el.astype(BF16)


def ssd_core(pz, dtp, conv_w, conv_b, dtb, alog, dlane, nw, *, lseq, conv_state=None, ssm_state=None):
    m = pz.shape[0]
    sample = conv_state is not None
    nseq = SSD_L // lseq
    nchunks = m // SSD_L
    nslices = B_HEADS // SSD_HG
    tri, same, e64, el = _ssd_consts(lseq)
    nx = 4096 // SSD_W
    rowblk = lambda w, off: pl.BlockSpec((SSD_L, w), lambda j, c, off=off: (c, off + j))
    parblk = lambda r, w, off: pl.BlockSpec((r, w), lambda j, c, off=off: (0, off + j))
    const = lambda a: pl.BlockSpec(a.shape, lambda j, c: (0, 0))
    ob = 8192 // SSD_GB
    oc = 9216 // SSD_GB
    in_specs = [
        rowblk(SSD_W, 0), rowblk(SSD_W, nx), rowblk(SSD_GB, ob), rowblk(SSD_GB, oc), rowblk(LANES, 0),
        parblk(B_CONV, SSD_W, 0), parblk(B_CONV, SSD_GB, 4096 // SSD_GB), parblk(B_CONV, SSD_GB, 5120 // SSD_GB),
        parblk(1, SSD_W, 0), parblk(1, SSD_GB, 4096 // SSD_GB), parblk(1, SSD_GB, 5120 // SSD_GB),
        parblk(1, LANES, 0), parblk(1, LANES, 0), parblk(1, SSD_W, 0), parblk(1, SSD_W, 0),
        const(tri), const(same), const(e64), const(el),
    ]
    args = [pz, pz, pz, pz, dtp, conv_w, conv_w, conv_w, conv_b, conv_b, conv_b, dtb, alog, dlane, nw, tri, same, e64, el]
    if sample:
        hblk = lambda w, off: pl.BlockSpec((nseq, B_CONV - 1, w), lambda j, c, off=off: (c, 0, off + j))
        in_specs += [hblk(SSD_W, 0), hblk(SSD_GB, 4096 // SSD_GB), hblk(SSD_GB, 5120 // SSD_GB),
                     pl.BlockSpec((nseq, SSD_PAIRS, LANES, B_STATE), lambda j, c: (c, j, 0, 0))]
        args += [conv_state, conv_state, conv_state, ssm_state]
        nstates = m // lseq
        so_spec = pl.BlockSpec((nseq, SSD_PAIRS, LANES, B_STATE), lambda j, c: (c, j, 0, 0))
    else:
        nstates = 1
        so_spec = pl.BlockSpec((1, SSD_PAIRS, LANES, B_STATE), lambda j, c: (0, j, 0, 0))
    L = SSD_L
    scratch = [
        pltpu.VMEM((nseq, HALO + lseq, SSD_W), F32), pltpu.VMEM((nseq, HALO + lseq, SSD_GB), F32),
        pltpu.VMEM((nseq, HALO + lseq, SSD_GB), F32),
        pltpu.VMEM((nseq, SSD_PAIRS, B_STATE, LANES), F32),
        pltpu.VMEM((L, SSD_W), F32), pltpu.VMEM((L, SSD_GB), BF16), pltpu.VMEM((L, SSD_GB), BF16),
        pltpu.VMEM((L, SSD_W), F32), pltpu.VMEM((L, SSD_W), F32), pltpu.VMEM((L, SSD_W), F32),
        pltpu.VMEM((L, SSD_HG * L), F32), pltpu.VMEM((LANES, L), F32), pltpu.VMEM((LANES, L), F32),
        pltpu.VMEM((L, SSD_W), F32),
    ]
    return pl.pallas_call(
        functools.partial(_ssd_kernel, nseq=nseq, sample=sample),
        grid=(nslices, nchunks),
        in_specs=in_specs,
        out_specs=[pl.BlockSpec((SSD_L, SSD_W), lambda j, c: (c, j)), so_spec],
        out_shape=[jax.ShapeDtypeStruct((m, B_HEADS * B_HEAD_DIM), BF16),
                   jax.ShapeDtypeStruct((nstates, B_HEADS // 2, LANES, B_STATE), F32)],
        scratch_shapes=scratch,
        compiler_params=_cparams(("arbitrary", "arbitrary"), 48),
        name="ssd_core",
    )(*args)


def _group_heads(v, fill=0.0):
    lead = v.shape[:-1]
    g = v.reshape(*lead, B_HEADS // SSD_HG, SSD_HG)
    g = jnp.pad(g, [(0, 0)] * (len(lead) + 1) + [(0, LANES - SSD_HG)], constant_values=fill)
    return g.reshape(*lead, (B_HEADS // SSD_HG) * LANES)


N_MIXERS = 3
PROJ_TM = 1024
PROJ_TN = 1024
OUT_TM = 256


def _run_stream(x, lseq, tm, par, state_conv=None, state_ssm=None, cache_k=None, cache_v=None):
    sample = state_conv is not None
    depth = par["norm_w"].shape[0]
    new = {"v": [], "ssm": [], "conv": [], "k": [], "v_attn": []}
    h = rms_norm_bf16(x, par["norm_w"][0], tm=tm)
    for i in range(depth):
        kind, j = i % N_MIXERS, i // N_MIXERS
        if kind == 0:
            p = matmul(h, par["a_w_in"][j], tm=tm, tn=PROJ_TN, name="a_in_proj")
            wpos, bias = _gmlp_pos_params(par["a_w_s"][j], par["a_b_s"][j], min(lseq, A_CHUNK))
            r = gmlp_core(p, wpos, bias, par["a_ln_g"][j], par["a_ln_b"][j], emit_v=sample)
            if sample:
                y, v = r
                new["v"].append(v)
            else:
                y = r
            w_out = par["a_w_out"][j]
        elif kind == 1:
            pz = matmul(h, par["b_w_zxbc"][j], tm=tm, tn=PROJ_TN, name="b_in_proj")
            dtp = matmul(h, par["b_w_dt"][j], tm=tm, tn=par["b_w_dt"][j].shape[1], name="b_dt_proj")
            y, s = ssd_core(
                pz, dtp, par["b_conv_w"][j], par["b_conv_b"][j], par["b_dt_bias"][j], par["b_a_log"][j],
                par["b_d_lane"][j], par["b_norm_w"][j], lseq=min(lseq, SSD_L),
                conv_state=state_conv[j] if sample else None,
                ssm_state=state_ssm[j].reshape(-1, B_HEADS // 2, LANES, B_STATE) if sample else None)
            new["ssm"].append(s.reshape(-1, B_HEADS, B_HEAD_DIM, B_STATE))
            new["conv"].append(pz.reshape(-1, lseq, pz.shape[1])[:, lseq - (B_CONV - 1):, 4096:])
            w_out = par["b_w_out"][j]
        else:
            p = matmul(h, par["c_w_in"][j], tm=tm, tn=PROJ_TN, name="c_in_proj")
            if sample:
                y, k, v = attn_sample(p, cache_k[j], cache_v[j], hb=8)
            else:
                y, k, v = attn_prompt(p, tq=min(C_TQ, p.shape[0]))
                k, v = k[None], v[None]
            new["k"].append(k)
            new["v_attn"].append(v)
            w_out = par["c_w_out"][j]
        last = i == depth - 1
        nw = par["final_norm_w"] if last else par["norm_w"][i + 1]
        r = matmul_residual_norm(y, w_out, x, nw, tm=OUT_TM, final=last)
        if last:
            return r, new
        x, h = r


def kernel(x_prompt, x_sample, state_ssm, state_conv, cache_k, cache_v, norm_w, final_norm_w, a_w_in, a_ln_g, a_ln_b, a_w_s, a_b_s, a_w_out, b_w_in, b_conv_w, b_conv_b, b_dt_bias, b_a_log, b_d, b_norm_w, b_w_out, c_w_in, c_w_out):
    bp, seq, d = x_prompt.shape
    bs, dseq, _ = x_sample.shape
    nb = b_w_in.shape[0]
    zx = b_w_in.shape[2] - B_HEADS
    par = {
        "norm_w": norm_w, "final_norm_w": final_norm_w,
        "a_w_in": a_w_in.astype(BF16), "a_ln_g": a_ln_g, "a_ln_b": a_ln_b, "a_w_s": a_w_s, "a_b_s": a_b_s,
        "a_w_out": a_w_out.astype(BF16),
        "b_w_zxbc": b_w_in[:, :, :zx].astype(BF16), "b_w_dt": _group_heads(b_w_in[:, :, zx:]).astype(BF16),
        "b_conv_w": b_conv_w, "b_conv_b": b_conv_b.reshape(nb, 1, -1),
        "b_dt_bias": _group_heads(b_dt_bias).reshape(nb, 1, -1), "b_a_log": _group_heads(b_a_log).reshape(nb, 1, -1),
        "b_d_lane": jnp.repeat(b_d, B_HEAD_DIM, axis=-1).reshape(nb, 1, -1), "b_norm_w": b_norm_w.reshape(nb, 1, -1),
        "b_w_out": b_w_out.astype(BF16), "c_w_in": c_w_in.astype(BF16), "c_w_out": c_w_out.astype(BF16),
    }
    assert bp == 1, "the prompt group is one stream"
    yp, newp = _run_stream(x_prompt.reshape(seq, d), seq, min(PROJ_TM, seq), par)
    ys, news = _run_stream(x_sample.reshape(bs * dseq, d), dseq, min(PROJ_TM, bs * dseq), par,
                           state_conv=state_conv, state_ssm=state_ssm, cache_k=cache_k, cache_v=cache_v)
    st = jnp.stack
    return (
        yp.reshape(bp, seq, d),
        ys.reshape(bs, dseq, d),
        st([v.reshape(bs, dseq, -1) for v in news["v"]]),
        st(newp["ssm"]), st(newp["conv"]), st(news["ssm"]), st(news["conv"]),
        st(newp["k"]), st(newp["v_attn"]), st(news["k"]), st(news["v_attn"]),
    )
```

```python
import functools

import jax
import jax.numpy as jnp
from jax import lax
from jax.experimental import pallas as pl
from jax.experimental.pallas import tpu as pltpu

F32 = jnp.float32
BF16 = jnp.bfloat16

NORM_EPS = 1e-6
D_MODEL = 2048
CHUNK = 64
A_GROUPS = 16
A_CHUNK = 128
B_HEADS = 64
B_HEAD_DIM = 64
B_GROUPS = 8
B_STATE = 128
B_CONV = 4
C_HEADS = 16
C_HEAD_DIM = 128

LANES = 128
MIB = 1024 * 1024


def _cparams(sem, vmem_mib):
    return pltpu.CompilerParams(dimension_semantics=sem, vmem_limit_bytes=vmem_mib * MIB)


def _rms_kernel(x_ref, w_ref, h_ref):
    x = x_ref[...]
    ms = jnp.mean(x * x, axis=-1, keepdims=True)
    h_ref[...] = (x * lax.rsqrt(ms + NORM_EPS) * w_ref[...]).astype(h_ref.dtype)


def rms_norm_bf16(x, w, *, tm):
    m, d = x.shape
    return pl.pallas_call(
        _rms_kernel,
        grid=(m // tm,),
        in_specs=[pl.BlockSpec((tm, d), lambda i: (i, 0)), pl.BlockSpec((1, d), lambda i: (0, 0))],
        out_specs=pl.BlockSpec((tm, d), lambda i: (i, 0)),
        out_shape=jax.ShapeDtypeStruct((m, d), BF16),
        compiler_params=_cparams(("parallel",), 32),
        name="rms_norm",
    )(x, w.reshape(1, d))


def _mm_kernel(x_ref, w_ref, o_ref):
    o_ref[...] = jnp.dot(x_ref[...], w_ref[...], preferred_element_type=F32).astype(o_ref.dtype)


def matmul(x, w, *, tm, tn, out_dtype=F32, name="proj"):
    m, k = x.shape
    n = w.shape[1]
    return pl.pallas_call(
        _mm_kernel,
        grid=(m // tm, n // tn),
        in_specs=[pl.BlockSpec((tm, k), lambda i, j: (i, 0)), pl.BlockSpec((k, tn), lambda i, j: (0, j))],
        out_specs=pl.BlockSpec((tm, tn), lambda i, j: (i, j)),
        out_shape=jax.ShapeDtypeStruct((m, n), out_dtype),
        compiler_params=_cparams(("parallel", "parallel"), 48),
        name=name,
    )(x, w)


def _mm_res_norm_kernel(y_ref, w_ref, x_ref, nw_ref, *out_refs, final):
    xn = x_ref[...] + jnp.dot(y_ref[...], w_ref[...], preferred_element_type=F32)
    ms = jnp.mean(xn * xn, axis=-1, keepdims=True)
    h = xn * lax.rsqrt(ms + NORM_EPS) * nw_ref[...]
    if final:
        out_refs[0][...] = h
    else:
        out_refs[0][...] = xn
        out_refs[1][...] = h.astype(BF16)


def matmul_residual_norm(y, w, x, nw, *, tm, final=False, name="out_proj"):
    m, k = y.shape
    d = w.shape[1]
    row = lambda i: (i, 0)
    if final:
        out_shape = jax.ShapeDtypeStruct((m, d), F32)
        out_specs = pl.BlockSpec((tm, d), row)
    else:
        out_shape = (jax.ShapeDtypeStruct((m, d), F32), jax.ShapeDtypeStruct((m, d), BF16))
        out_specs = (pl.BlockSpec((tm, d), row), pl.BlockSpec((tm, d), row))
    return pl.pallas_call(
        functools.partial(_mm_res_norm_kernel, final=final),
        grid=(m // tm,),
        in_specs=[
            pl.BlockSpec((tm, k), row),
            pl.BlockSpec((k, d), lambda i: (0, 0)),
            pl.BlockSpec((tm, d), row),
            pl.BlockSpec((1, d), lambda i: (0, 0)),
        ],
        out_specs=out_specs,
        out_shape=out_shape,
        compiler_params=_cparams(("parallel",), 56),
        name=name,
    )(y, w, x, nw.reshape(1, d))


def _gelu(x):
    return 0.5 * x * (1.0 + lax.erf(x * (2.0 ** -0.5)))


def _silu(x):
    return x * (1.0 / (1.0 + jnp.exp(-x)))


A_GW = 256


def _gmlp_kernel(u_ref, v_ref, z_ref, wpos_ref, bias_ref, g_ref, b_ref, *refs, emit_v):
    if emit_v:
        y_ref, vout_ref, gv_scr, vn_scr = refs
    else:
        y_ref, gv_scr, vn_scr = refs
    rows, width = gv_scr.shape
    ngroups = width // A_GW
    acc = jnp.zeros((rows, LANES), F32)
    for g in range(ngroups):
        sl = slice(g * A_GW, (g + 1) * A_GW)
        gv = _gelu(v_ref[:, sl])
        gv_scr[:, sl] = gv
        acc = acc + gv[:, :LANES] + gv[:, LANES:]
    mean = jnp.sum(acc, axis=-1, keepdims=True) * (1.0 / width)
    acc = jnp.zeros((rows, LANES), F32)
    for g in range(ngroups):
        sl = slice(g * A_GW, (g + 1) * A_GW)
        vc = gv_scr[:, sl] - mean
        sq = vc * vc
        acc = acc + sq[:, :LANES] + sq[:, LANES:]
    var = jnp.sum(acc, axis=-1, keepdims=True) * (1.0 / width)
    rstd = lax.rsqrt(var + NORM_EPS)
    for g in range(ngroups):
        sl = slice(g * A_GW, (g + 1) * A_GW)
        vn = (gv_scr[:, sl] - mean) * rstd * g_ref[:, sl] + b_ref[:, sl]
        if emit_v:
            vout_ref[:, sl] = vn
        vn_scr[:, sl] = vn.astype(BF16)
    for g in range(ngroups):
        sl = slice(g * A_GW, (g + 1) * A_GW)
        s = jnp.dot(wpos_ref[g], vn_scr[:, sl], preferred_element_type=F32) + bias_ref[:, sl]
        y_ref[:, sl] = (_gelu(u_ref[:, sl]) * s * _silu(z_ref[:, sl])).astype(BF16)


def gmlp_core(p, wpos, bias, ln_g, ln_b, *, emit_v):
    m, w3 = p.shape
    w = w3 // 3
    t = A_CHUNK
    col = lambda c: pl.BlockSpec((t, w), lambda i, c=c: (i, c))
    const2 = lambda shape: pl.BlockSpec(shape, lambda i: (0, 0))
    out_shape = [jax.ShapeDtypeStruct((m, w), BF16)]
    out_specs = [pl.BlockSpec((t, w), lambda i: (i, 0))]
    if emit_v:
        out_shape.append(jax.ShapeDtypeStruct((m, w), F32))
        out_specs.append(pl.BlockSpec((t, w), lambda i: (i, 0)))
    res = pl.pallas_call(
        functools.partial(_gmlp_kernel, emit_v=emit_v),
        grid=(m // t,),
        in_specs=[
            col(0), col(1), col(2),
            pl.BlockSpec(wpos.shape, lambda i: (0, 0, 0)),
            const2((t, w)), const2((1, w)), const2((1, w)),
        ],
        out_specs=out_specs,
        out_shape=out_shape,
        scratch_shapes=[pltpu.VMEM((t, w), F32), pltpu.VMEM((t, w), BF16)],
        compiler_params=_cparams(("parallel",), 40),
        name="gmlp_core",
    )(p, p, p, wpos, bias, ln_g.reshape(1, w), ln_b.reshape(1, w))
    return res if emit_v else res[0]


def _gmlp_pos_params(w_s, b_s, lc):
    pos = jnp.arange(lc)
    mask = (pos[None, :] // CHUNK) <= (pos[:, None] // CHUNK)
    wp = jnp.where(mask[None], w_s[:, :lc, :lc], 0.0)
    reps = A_CHUNK // lc
    if reps > 1:
        eye = jnp.eye(reps, dtype=wp.dtype)
        wp = jnp.einsum("ab,gts->gatbs", eye, wp).reshape(w_s.shape[0], A_CHUNK, A_CHUNK)
    bias_t = jnp.tile(b_s[:, :lc].T, (reps, 1))
    bias = jnp.repeat(bias_t, A_GW, axis=1)
    return wp.astype(BF16), bias.astype(F32)


C_TK = 128
C_SW = 256
C_TQ = 512


def _suffix_sum_matrix():
    j = jnp.arange(C_SW)
    return (j[:, None] >= j[None, :]).astype(BF16)


LOG2E = 1.4426950408889634
C_QSCALE = (C_HEAD_DIM ** -0.5) * LOG2E


def _sb_scores(q, kw):
    return lax.dot_general(q, kw, (((1,), (1,)), ((), ())), preferred_element_type=F32)


def _sb_weights(z, carry, tmat, vis):
    t, n = z.shape
    nabs = lax.bitcast_convert_type(lax.bitcast_convert_type(z, jnp.uint32) | jnp.uint32(0x80000000), F32)
    sp = jnp.maximum(z, 0.0) + jnp.log2(1.0 + jnp.exp2(nabs))
    if vis is not None:
        sp = jnp.where(vis, sp, 0.0)
    sp = sp.astype(BF16)
    bounds = list(range(0, n, C_SW)) + [n]
    groups = list(zip(bounds[:-1], bounds[1:]))
    sums = {}
    for lo, hi in reversed(groups):
        sums[lo] = jnp.dot(sp[:, lo:hi], tmat[:hi - lo, :hi - lo], preferred_element_type=F32)
    ws = []
    for lo, hi in reversed(groups):
        s = sums[lo]
        for c0 in range(hi - lo - C_TK, -1, -C_TK):
            ws.append(jnp.exp2(z[:, lo + c0:lo + c0 + C_TK] - s[:, c0:c0 + C_TK] - carry))
        carry = carry + jnp.broadcast_to(s[:, 0:1], (t, C_TK))
    w = ws[0] if len(ws) == 1 else jnp.concatenate(ws[::-1], axis=1)
    if vis is not None:
        w = jnp.where(vis, w, 0.0)
    return w.astype(BF16), carry


def _sb_wide(q, kw, vw, carry, tmat, vis):
    w, carry = _sb_weights(_sb_scores(q, kw), carry, tmat, vis)
    return jnp.dot(w, vw, preferred_element_type=F32), carry


def _attn_prompt_kernel(q_ref, k_ref, v_ref, zg_ref, tmat_ref, o_ref, ko_ref, vo_ref,
                        kb_scr, vb_scr, acc_scr, carry_scr, z_scr, w_scr, *, tq):
    i = pl.program_id(1)

    @pl.when(i == 0)
    def _():
        kb_scr[...] = k_ref[...].astype(BF16)
        vb_scr[...] = v_ref[...].astype(BF16)

    r0 = pl.multiple_of(i * tq, tq)
    ko_ref[...] = k_ref[pl.ds(r0, tq), :]
    vo_ref[...] = v_ref[pl.ds(r0, tq), :]
    q = (q_ref[...] * C_QSCALE).astype(BF16)
    tmat = tmat_ref[...]

    def rows(j):
        return pl.ds(pl.multiple_of(jnp.maximum(j, 0) * tq, tq), tq)

    def stage(j, cur):
        nxt = 1 - cur
        z_scr[nxt] = _sb_scores(q, kb_scr[rows(j - 1), :])
        w_scr[nxt], carry_scr[...] = _sb_weights(z_scr[cur], carry_scr[...], tmat, None)
        acc_scr[...] += jnp.dot(w_scr[cur], vb_scr[rows(j + 1), :], preferred_element_type=F32)

    def finish(cur):
        acc = acc_scr[...] + jnp.dot(w_scr[cur], vb_scr[rows(0), :], preferred_element_type=F32)
        o_ref[...] = (acc * _silu(zg_ref[...])).astype(o_ref.dtype)

    qpos = lax.broadcasted_iota(jnp.int32, (tq, tq), 0)
    kpos = lax.broadcasted_iota(jnp.int32, (tq, tq), 1)
    w_scr[0], carry_scr[...] = _sb_weights(
        _sb_scores(q, kb_scr[rows(i), :]), jnp.zeros((tq, C_TK), F32), tmat, kpos < qpos)
    z_scr[0] = _sb_scores(q, kb_scr[rows(i - 1), :])
    acc_scr[...] = jnp.zeros_like(acc_scr)

    def body(it, c):
        j = i - 1 - 2 * it
        stage(j, 0)
        stage(j - 1, 1)
        return c

    lax.fori_loop(0, i // 2, body, 0)

    @pl.when(i % 2 == 1)
    def _():
        stage(0, 0)
        finish(1)

    @pl.when(i % 2 == 0)
    def _():
        finish(0)


def attn_prompt(p, *, tq):
    seq, w4 = p.shape
    w = w4 // 4
    nh = w // C_HEAD_DIM
    d = C_HEAD_DIM
    blk = lambda c: pl.BlockSpec((tq, d), lambda h, i, c=c: (i, c * nh + h))
    whole = lambda c: pl.BlockSpec((seq, d), lambda h, i, c=c: (0, c * nh + h))
    tmat = _suffix_sum_matrix()
    return pl.pallas_call(
        functools.partial(_attn_prompt_kernel, tq=tq),
        grid=(nh, seq // tq),
        in_specs=[blk(0), whole(1), whole(2), blk(3), pl.BlockSpec(tmat.shape, lambda h, i: (0, 0))],
        out_specs=[
            pl.BlockSpec((tq, d), lambda h, i: (i, h)),
            pl.BlockSpec((None, tq, d), lambda h, i: (h, i, 0)),
            pl.BlockSpec((None, tq, d), lambda h, i: (h, i, 0)),
        ],
        out_shape=[
            jax.ShapeDtypeStruct((seq, w), BF16),
            jax.ShapeDtypeStruct((nh, seq, d), F32),
            jax.ShapeDtypeStruct((nh, seq, d), F32),
        ],
        scratch_shapes=[
            pltpu.VMEM((seq, d), BF16), pltpu.VMEM((seq, d), BF16),
            pltpu.VMEM((tq, d), F32), pltpu.VMEM((tq, C_TK), F32),
            pltpu.VMEM((2, tq, tq), F32), pltpu.VMEM((2, tq, tq), BF16),
        ],
        compiler_params=_cparams(("arbitrary", "arbitrary"), 56),
        name="attn_prompt",
    )(p, p, p, p, tmat)


def _attn_sample_kernel(q_ref, kn_ref, vn_ref, zg_ref, kc_ref, vc_ref, tmat_ref, o_ref, ko_ref, vo_ref, *, hb, past):
    t = q_ref.shape[0]
    d = C_HEAD_DIM
    tmat = tmat_ref[...]
    qpos = lax.broadcasted_iota(jnp.int32, (t, C_TK), 0)
    kpos = lax.broadcasted_iota(jnp.int32, (t, C_TK), 1)
    vis_new = kpos < qpos
    pad = jnp.zeros((C_TK - t, d), BF16)
    for h in range(hb):
        sl = slice(h * d, (h + 1) * d)
        kn = kn_ref[:, sl]
        vn = vn_ref[:, sl]
        ko_ref[h] = kn
        vo_ref[h] = vn
        q = (q_ref[:, sl] * C_QSCALE).astype(BF16)
        carry = jnp.zeros((t, C_TK), F32)
        acc, carry = _sb_wide(q, jnp.concatenate([kn.astype(BF16), pad], axis=0),
                              jnp.concatenate([vn.astype(BF16), pad], axis=0), carry, tmat, vis_new)
        pv, carry = _sb_wide(q, kc_ref[h].astype(BF16), vc_ref[h].astype(BF16), carry, tmat, None)
        acc = acc + pv
        o_ref[:, sl] = (acc * _silu(zg_ref[:, sl])).astype(o_ref.dtype)


def attn_sample(p, cache_k, cache_v, *, hb):
    bsz, nh, past, d = cache_k.shape
    t = p.shape[0] // bsz
    w = nh * d
    ng = nh // hb
    blk = lambda c: pl.BlockSpec((t, hb * d), lambda b, g, c=c: (b, c * ng + g))
    cspec = pl.BlockSpec((None, hb, past, d), lambda b, g: (b, g, 0, 0))
    nspec = pl.BlockSpec((None, hb, t, d), lambda b, g: (b, g, 0, 0))
    tmat = _suffix_sum_matrix()
    return pl.pallas_call(
        functools.partial(_attn_sample_kernel, hb=hb, past=past),
        grid=(bsz, ng),
        in_specs=[blk(0), blk(1), blk(2), blk(3), cspec, cspec, pl.BlockSpec(tmat.shape, lambda b, g: (0, 0))],
        out_specs=[pl.BlockSpec((t, hb * d), lambda b, g: (b, g)), nspec, nspec],
        out_shape=[
            jax.ShapeDtypeStruct((bsz * t, w), BF16),
            jax.ShapeDtypeStruct((bsz, nh, t, d), F32),
            jax.ShapeDtypeStruct((bsz, nh, t, d), F32),
        ],
        compiler_params=_cparams(("parallel", "parallel"), 40),
        name="attn_sample",
    )(p, p, p, p, cache_k, cache_v, tmat)


SSD_L = 128
SSD_HG = 16
SSD_PAIRS = SSD_HG // 2
SSD_W = SSD_HG * B_HEAD_DIM
SSD_GB = 2 * B_STATE
HALO = 8


def _split3(x):
    hi = x.astype(BF16)
    r = x - hi.astype(F32)
    mid = r.astype(BF16)
    lo = (r - mid.astype(F32)).astype(BF16)
    return hi, mid, lo


def _dot_x01(x, m):
    hi, mid, lo = _split3(x)
    d = lambda a: jnp.dot(a, m, preferred_element_type=F32)
    return (d(hi) + d(mid)) + d(lo)


def _dot_01x(m, x):
    hi, mid, lo = _split3(x)
    d = lambda a: jnp.dot(m, a, preferred_element_type=F32)
    return (d(hi) + d(mid)) + d(lo)


def _softplus(x):
    return jnp.maximum(x, 0.0) + jnp.log1p(jnp.exp(-jnp.abs(x)))


def _ssd_kernel(*refs, nseq, sample):
    (z_ref, x_ref, b_ref, c_ref, dt_ref, cwx_ref, cwb_ref, cwc_ref, cbx_ref, cbb_ref, cbc_ref,
     dtb_ref, alog_ref, dl_ref, nw_ref, tri_ref, same_ref, e64_ref, el_ref) = refs[:19]
    refs = refs[19:]
    if sample:
        hx_ref, hb_ref, hc_ref, s0_ref = refs[:4]
        refs = refs[4:]
    (y_ref, so_ref, extx, extb, extc, st_scr, xs_scr, bs_scr, cs_scr, ecx_scr, wex_scr, dcx_scr,
     cc_scr, cumt_scr, dtt_scr, yacc_scr) = refs
    L = SSD_L
    lseq = L // nseq
    c = pl.program_id(1)
    nchunks = pl.num_programs(1)

    if sample:
        for s in range(nseq):
            extx[s, HALO - 3:HALO, :] = hx_ref[s]
            extb[s, HALO - 3:HALO, :] = hb_ref[s]
            extc[s, HALO - 3:HALO, :] = hc_ref[s]
            for pp in range(SSD_PAIRS):
                st_scr[s, pp] = s0_ref[s, pp].T
    else:
        @pl.when(c == 0)
        def _():
            extx[0, 0:HALO, :] = jnp.zeros((HALO, SSD_W), F32)
            extb[0, 0:HALO, :] = jnp.zeros((HALO, SSD_GB), F32)
            extc[0, 0:HALO, :] = jnp.zeros((HALO, SSD_GB), F32)
            st_scr[...] = jnp.zeros_like(st_scr)

    def conv(ext, src_ref, cw_ref, cb_ref, dst):
        for s in range(nseq):
            ext[s, HALO:HALO + lseq, :] = src_ref[s * lseq:(s + 1) * lseq, :]
        for s in range(nseq):
            acc = cb_ref[...]
            for k in range(B_CONV):
                acc = acc + ext[s, HALO - 3 + k:HALO - 3 + k + lseq, :] * cw_ref[k:k + 1, :]
            dst[s * lseq:(s + 1) * lseq, :] = _silu(acc).astype(dst.dtype)
        if not sample:
            ext[0, 0:HALO, :] = ext[0, lseq:lseq + HALO, :]

    conv(extx, x_ref, cwx_ref, cbx_ref, xs_scr)
    conv(extb, b_ref, cwb_ref, cbb_ref, bs_scr)
    conv(extc, c_ref, cwc_ref, cbc_ref, cs_scr)

    dt = _softplus(dt_ref[...] + dtb_ref[...])
    dta = dt * (-jnp.exp(alog_ref[...]))
    cum = _dot_01x(tri_ref[...], dta)
    ctot = _dot_01x(same_ref[...], dta)
    cumt_scr[...] = cum.T
    dtt_scr[...] = dt.T
    e64 = e64_ref[...]
    ecx_scr[...] = _dot_x01(jnp.exp(cum), e64)
    wex_scr[...] = _dot_x01(jnp.exp(ctot - cum) * dt, e64)
    dcx_scr[...] = _dot_x01(jnp.exp(ctot), e64)
    cc_scr[...] = _dot_x01(cum, el_ref[...])
    mask = tri_ref[...] > 0
    lane = lax.broadcasted_iota(jnp.int32, (L, LANES), 1)
    first = lane < B_HEAD_DIM

    for g2 in range(2):
        bg = bs_scr[:, g2 * B_STATE:(g2 + 1) * B_STATE]
        cg = cs_scr[:, g2 * B_STATE:(g2 + 1) * B_STATE]
        cb = lax.dot_general(cg, bg, (((1,), (1,)), ((), ())), preferred_element_type=F32)
        for p in range(SSD_PAIRS // 2):
            pp = g2 * (SSD_PAIRS // 2) + p
            lanes = slice(pp * LANES, (pp + 1) * LANES)
            ms = []
            for r in (2 * pp, 2 * pp + 1):
                seg = cc_scr[:, r * L:(r + 1) * L] - cumt_scr[r:r + 1, :]
                ms.append((cb * jnp.exp(jnp.where(mask, seg, -jnp.inf)) * dtt_scr[r:r + 1, :]).astype(BF16))
            xp = xs_scr[:, lanes]
            xa = jnp.where(first, xp, 0.0).astype(BF16)
            xb = jnp.where(first, 0.0, xp).astype(BF16)
            y = jnp.dot(jnp.concatenate(ms, axis=1), jnp.concatenate([xa, xb], axis=0), preferred_element_type=F32)
            xw = (xp * wex_scr[:, lanes]).astype(BF16)
            ys = []
            for s in range(nseq):
                rows = slice(s * lseq, (s + 1) * lseq)
                st = st_scr[s, pp]
                ys.append(jnp.dot(cg[rows], st.astype(BF16), preferred_element_type=F32))
                upd = lax.dot_general(bg[rows], xw[rows], (((0,), (0,)), ((), ())), preferred_element_type=F32)
                st_scr[s, pp] = st * dcx_scr[s * lseq:s * lseq + 1, lanes] + upd
            ystate = ys[0] if nseq == 1 else jnp.concatenate(ys, axis=0)
            y = y + ystate * ecx_scr[:, lanes] + dl_ref[:, lanes] * xp
            yacc_scr[:, lanes] = y * _silu(z_ref[:, lanes])

    gw = SSD_W // 2
    for g2 in range(2):
        sl = slice(g2 * gw, (g2 + 1) * gw)
        yg = yacc_scr[:, sl]
        ms_ = jnp.mean(yg * yg, axis=-1, keepdims=True)
        y_ref[:, sl] = (yg * lax.rsqrt(ms_ + NORM_EPS) * nw_ref[:, sl]).astype(y_ref.dtype)

    if sample:
        for s in range(nseq):
            for pp in range(SSD_PAIRS):
                so_ref[s, pp] = st_scr[s, pp].T
    else:
        @pl.when(c == nchunks - 1)
        def _():
            for pp in range(SSD_PAIRS):
                so_ref[0, pp] = st_scr[0, pp].T


def _ssd_consts(lseq):
    L = SSD_L
    t = jnp.arange(L)
    same = (t[:, None] // lseq) == (t[None, :] // lseq)
    tri = same & (t[None, :] <= t[:, None])
    r = jnp.arange(LANES)
    e64 = (r[:, None] == (jnp.arange(SSD_W)[None, :] // B_HEAD_DIM)) & (r[:, None] < SSD_HG)
    el = (r[:, None] == (jnp.arange(SSD_HG * L)[None, :] // L)) & (r[:, None] < SSD_HG)
    return tri.astype(BF16), same.astype(BF16), e64.astype(BF16), el.astype(BF16)


def ssd_core(pz, dtp, conv_w, conv_b, dtb, alog, dlane, nw, *, lseq, conv_state=None, ssm_state=None):
    m = pz.shape[0]
    sample = conv_state is not None
    nseq = SSD_L // lseq
    nchunks = m // SSD_L
    nslices = B_HEADS // SSD_HG
    tri, same, e64, el = _ssd_consts(lseq)
    nx = 4096 // SSD_W
    rowblk = lambda w, off: pl.BlockSpec((SSD_L, w), lambda j, c, off=off: (c, off + j))
    parblk = lambda r, w, off: pl.BlockSpec((r, w), lambda j, c, off=off: (0, off + j))
    const = lambda a: pl.BlockSpec(a.shape, lambda j, c: (0, 0))
    ob = 8192 // SSD_GB
    oc = 9216 // SSD_GB
    in_specs = [
        rowblk(SSD_W, 0), rowblk(SSD_W, nx), rowblk(SSD_GB, ob), rowblk(SSD_GB, oc), rowblk(LANES, 0),
        parblk(B_CONV, SSD_W, 0), parblk(B_CONV, SSD_GB, 4096 // SSD_GB), parblk(B_CONV, SSD_GB, 5120 // SSD_GB),
        parblk(1, SSD_W, 0), parblk(1, SSD_GB, 4096 // SSD_GB), parblk(1, SSD_GB, 5120 // SSD_GB),
        parblk(1, LANES, 0), parblk(1, LANES, 0), parblk(1, SSD_W, 0), parblk(1, SSD_W, 0),
        const(tri), const(same), const(e64), const(el),
    ]
    args = [pz, pz, pz, pz, dtp, conv_w, conv_w, conv_w, conv_b, conv_b, conv_b, dtb, alog, dlane, nw, tri, same, e64, el]
    if sample:
        hblk = lambda w, off: pl.BlockSpec((nseq, B_CONV - 1, w), lambda j, c, off=off: (c, 0, off + j))
        in_specs += [hblk(SSD_W, 0), hblk(SSD_GB, 4096 // SSD_GB), hblk(SSD_GB, 5120 // SSD_GB),
                     pl.BlockSpec((nseq, SSD_PAIRS, LANES, B_STATE), lambda j, c: (c, j, 0, 0))]
        args += [conv_state, conv_state, conv_state, ssm_state]
        nstates = m // lseq
        so_spec = pl.BlockSpec((nseq, SSD_PAIRS, LANES, B_STATE), lambda j, c: (c, j, 0, 0))
    else:
        nstates = 1
        so_spec = pl.BlockSpec((1, SSD_PAIRS, LANES, B_STATE), lambda j, c: (0, j, 0, 0))
    L = SSD_L
    scratch = [
        pltpu.VMEM((nseq, HALO + lseq, SSD_W), F32), pltpu.VMEM((nseq, HALO + lseq, SSD_GB), F32),
        pltpu.VMEM((nseq, HALO + lseq, SSD_GB), F32),
        pltpu.VMEM((nseq, SSD_PAIRS, B_STATE, LANES), F32),
        pltpu.VMEM((L, SSD_W), F32), pltpu.VMEM((L, SSD_GB), BF16), pltpu.VMEM((L, SSD_GB), BF16),
        pltpu.VMEM((L, SSD_W), F32), pltpu.VMEM((L, SSD_W), F32), pltpu.VMEM((L, SSD_W), F32),
        pltpu.VMEM((L, SSD_HG * L), F32), pltpu.VMEM((LANES, L), F32), pltpu.VMEM((LANES, L), F32),
        pltpu.VMEM((L, SSD_W), F32),
    ]
    return pl.pallas_call(
        functools.partial(_ssd_kernel, nseq=nseq, sample=sample),
        grid=(nslices, nchunks),
        in_specs=in_specs,
        out_specs=[pl.BlockSpec((SSD_L, SSD_W), lambda j, c: (c, j)), so_spec],
        out_shape=[jax.ShapeDtypeStruct((m, B_HEADS * B_HEAD_DIM), BF16),
                   jax.ShapeDtypeStruct((nstates, B_HEADS // 2, LANES, B_STATE), F32)],
        scratch_shapes=scratch,
        compiler_params=_cparams(("arbitrary", "arbitrary"), 48),
        name="ssd_core",
    )(*args)


def _group_heads(v, fill=0.0):
    lead = v.shape[:-1]
    g = v.reshape(*lead, B_HEADS // SSD_HG, SSD_HG)
    g = jnp.pad(g, [(0, 0)] * (len(lead) + 1) + [(0, LANES - SSD_HG)], constant_values=fill)
    return g.reshape(*lead, (B_HEADS // SSD_HG) * LANES)


N_MIXERS = 3
PROJ_TM = 1024
PROJ_TN = 1024
OUT_TM = 256


def _run_stream(x, lseq, tm, par, state_conv=None, state_ssm=None, cache_k=None, cache_v=None):
    sample = state_conv is not None
    depth = par["norm_w"].shape[0]
    new = {"v": [], "ssm": [], "conv": [], "k": [], "v_attn": []}
    h = rms_norm_bf16(x, par["norm_w"][0], tm=tm)
    for i in range(depth):
        kind, j = i % N_MIXERS, i // N_MIXERS
        if kind == 0:
            p = matmul(h, par["a_w_in"][j], tm=tm, tn=PROJ_TN, name="a_in_proj")
            wpos, bias = _gmlp_pos_params(par["a_w_s"][j], par["a_b_s"][j], min(lseq, A_CHUNK))
            r = gmlp_core(p, wpos, bias, par["a_ln_g"][j], par["a_ln_b"][j], emit_v=sample)
            if sample:
                y, v = r
                new["v"].append(v)
            else:
                y = r
            w_out = par["a_w_out"][j]
        elif kind == 1:
            pz = matmul(h, par["b_w_zxbc"][j], tm=tm, tn=PROJ_TN, name="b_in_proj")
            dtp = matmul(h, par["b_w_dt"][j], tm=tm, tn=par["b_w_dt"][j].shape[1], name="b_dt_proj")
            y, s = ssd_core(
                pz, dtp, par["b_conv_w"][j], par["b_conv_b"][j], par["b_dt_bias"][j], par["b_a_log"][j],
                par["b_d_lane"][j], par["b_norm_w"][j], lseq=min(lseq, SSD_L),
                conv_state=state_conv[j] if sample else None,
                ssm_state=state_ssm[j].reshape(-1, B_HEADS // 2, LANES, B_STATE) if sample else None)
            new["ssm"].append(s.reshape(-1, B_HEADS, B_HEAD_DIM, B_STATE))
            new["conv"].append(pz.reshape(-1, lseq, pz.shape[1])[:, lseq - (B_CONV - 1):, 4096:])
            w_out = par["b_w_out"][j]
        else:
            p = matmul(h, par["c_w_in"][j], tm=tm, tn=PROJ_TN, name="c_in_proj")
            if sample:
                y, k, v = attn_sample(p, cache_k[j], cache_v[j], hb=8)
            else:
                y, k, v = attn_prompt(p, tq=min(C_TQ, p.shape[0]))
                k, v = k[None], v[None]
            new["k"].append(k)
            new["v_attn"].append(v)
            w_out = par["c_w_out"][j]
        last = i == depth - 1
        nw = par["final_norm_w"] if last else par["norm_w"][i + 1]
        r = matmul_residual_norm(y, w_out, x, nw, tm=OUT_TM, final=last)
        if last:
            return r, new
        x, h = r


def kernel(x_prompt, x_sample, state_ssm, state_conv, cache_k, cache_v, norm_w, final_norm_w, a_w_in, a_ln_g, a_ln_b, a_w_s, a_b_s, a_w_out, b_w_in, b_conv_w, b_conv_b, b_dt_bias, b_a_log, b_d, b_norm_w, b_w_out, c_w_in, c_w_out):
    bp, seq, d = x_prompt.shape
    bs, dseq, _ = x_sample.shape
    nb = b_w_in.shape[0]
    zx = b_w_in.shape[2] - B_HEADS
    par = {
        "norm_w": norm_w, "final_norm_w": final_norm_w,
        "a_w_in": a_w_in.astype(BF16), "a_ln_g": a_ln_g, "a_ln_b": a_ln_b, "a_w_s": a_w_s, "a_b_s": a_b_s,
        "a_w_out": a_w_out.astype(BF16),
        "b_w_zxbc": b_w_in[:, :, :zx].astype(BF16), "b_w_dt": _group_heads(b_w_in[:, :, zx:]).astype(BF16),
        "b_conv_w": b_conv_w, "b_conv_b": b_conv_b.reshape(nb, 1, -1),
        "b_dt_bias": _group_heads(b_dt_bias).reshape(nb, 1, -1), "b_a_log": _group_heads(b_a_log).reshape(nb, 1, -1),
        "b_d_lane": jnp.repeat(b_d, B_HEAD_DIM, axis=-1).reshape(nb, 1, -1), "b_norm_w": b_norm_w.reshape(nb, 1, -1),
        "b_w_out": b_w_out.astype(BF16), "c_w_in": c_w_in.astype(BF16), "c_w_out": c_w_out.astype(BF16),
    }
    assert bp == 1, "the prompt group is one stream"
    yp, newp = _run_stream(x_prompt.reshape(seq, d), seq, min(PROJ_TM, seq), par)
    ys, news = _run_stream(x_sample.reshape(bs * dseq, d), dseq, min(PROJ_TM, bs * dseq), par,
                           state_conv=state_conv, state_ssm=state_ssm, cache_k=cache_k, cache_v=cache_v)
    st = jnp.stack
    return (
        yp.reshape(bp, seq, d),
        ys.reshape(bs, dseq, d),
        st([v.reshape(bs, dseq, -1) for v in news["v"]]),
        st(newp["ssm"]), st(newp["conv"]), st(news["ssm"]), st(news["conv"]),
        st(newp["k"]), st(newp["v_attn"]), st(news["k"]), st(news["v_attn"]),
    )
```

```python
import functools

import jax
import jax.numpy as jnp
from jax import lax
from jax.experimental import pallas as pl
from jax.experimental.pallas import tpu as pltpu

F32 = jnp.float32
BF16 = jnp.bfloat16

NORM_EPS = 1e-6
D_MODEL = 2048
CHUNK = 64
A_GROUPS = 16
A_CHUNK = 128
B_HEADS = 64
B_HEAD_DIM = 64
B_GROUPS = 8
B_STATE = 128
B_CONV = 4
C_HEADS = 16
C_HEAD_DIM = 128

LANES = 128
MIB = 1024 * 1024


def _cparams(sem, vmem_mib):
    return pltpu.CompilerParams(dimension_semantics=sem, vmem_limit_bytes=vmem_mib * MIB)


def _rms_kernel(x_ref, w_ref, h_ref):
    x = x_ref[...]
    ms = jnp.mean(x * x, axis=-1, keepdims=True)
    h_ref[...] = (x * lax.rsqrt(ms + NORM_EPS) * w_ref[...]).astype(h_ref.dtype)


def rms_norm_bf16(x, w, *, tm):
    m, d = x.shape
    return pl.pallas_call(
        _rms_kernel,
        grid=(m // tm,),
        in_specs=[pl.BlockSpec((tm, d), lambda i: (i, 0)), pl.BlockSpec((1, d), lambda i: (0, 0))],
        out_specs=pl.BlockSpec((tm, d), lambda i: (i, 0)),
        out_shape=jax.ShapeDtypeStruct((m, d), BF16),
        compiler_params=_cparams(("parallel",), 32),
        name="rms_norm",
    )(x, w.reshape(1, d))


def _mm_kernel(x_ref, w_ref, o_ref, wb_scr):
    @pl.when(pl.program_id(1) == 0)
    def _():
        wb_scr[...] = w_ref[...].astype(BF16)

    o_ref[...] = jnp.dot(x_ref[...], wb_scr[...], preferred_element_type=F32).astype(o_ref.dtype)


def matmul(x, w, layer, *, tm, tn, n=None, out_dtype=F32, name="proj"):
    m, k = x.shape
    n = w.shape[2] if n is None else n
    return pl.pallas_call(
        _mm_kernel,
        grid=(n // tn, m // tm),
        in_specs=[pl.BlockSpec((tm, k), lambda j, i: (i, 0)), pl.BlockSpec((None, k, tn), lambda j, i: (layer, 0, j))],
        out_specs=pl.BlockSpec((tm, tn), lambda j, i: (i, j)),
        out_shape=jax.ShapeDtypeStruct((m, n), out_dtype),
        scratch_shapes=[pltpu.VMEM((k, tn), BF16)],
        compiler_params=_cparams(("arbitrary", "arbitrary"), 48),
        name=name,
    )(x, w)


def _mm_res_norm_kernel(y_ref, w_ref, x_ref, nw_ref, *out_refs, final):
    xn = x_ref[...] + jnp.dot(y_ref[...], w_ref[...], preferred_element_type=F32)
    ms = jnp.mean(xn * xn, axis=-1, keepdims=True)
    h = xn * lax.rsqrt(ms + NORM_EPS) * nw_ref[...]
    if final:
        out_refs[0][...] = h
    else:
        out_refs[0][...] = xn
        out_refs[1][...] = h.astype(BF16)


def matmul_residual_norm(y, w, layer, x, nw, *, tm, final=False, name="out_proj"):
    m, k = y.shape
    d = w.shape[2]
    row = lambda i: (i, 0)
    if final:
        out_shape = jax.ShapeDtypeStruct((m, d), F32)
        out_specs = pl.BlockSpec((tm, d), row)
    else:
        out_shape = (jax.ShapeDtypeStruct((m, d), F32), jax.ShapeDtypeStruct((m, d), BF16))
        out_specs = (pl.BlockSpec((tm, d), row), pl.BlockSpec((tm, d), row))
    return pl.pallas_call(
        functools.partial(_mm_res_norm_kernel, final=final),
        grid=(m // tm,),
        in_specs=[
            pl.BlockSpec((tm, k), row),
            pl.BlockSpec((None, k, d), lambda i: (layer, 0, 0)),
            pl.BlockSpec((tm, d), row),
            pl.BlockSpec((1, d), lambda i: (0, 0)),
        ],
        out_specs=out_specs,
        out_shape=out_shape,
        compiler_params=_cparams(("parallel",), 56),
        name=name,
    )(y, w, x, nw.reshape(1, d))


def _gelu(x):
    return 0.5 * x * (1.0 + lax.erf(x * (2.0 ** -0.5)))


def _silu(x):
    return x * (1.0 / (1.0 + jnp.exp(-x)))


A_GW = 256


def _gmlp_kernel(u_ref, v_ref, z_ref, wpos_ref, bias_ref, g_ref, b_ref, *refs, emit_v):
    if emit_v:
        y_ref, vout_ref, gv_scr, vn_scr = refs
    else:
        y_ref, gv_scr, vn_scr = refs
    rows, width = gv_scr.shape
    ngroups = width // A_GW
    acc = jnp.zeros((rows, LANES), F32)
    for g in range(ngroups):
        sl = slice(g * A_GW, (g + 1) * A_GW)
        gv = _gelu(v_ref[:, sl])
        gv_scr[:, sl] = gv
        acc = acc + gv[:, :LANES] + gv[:, LANES:]
    mean = jnp.sum(acc, axis=-1, keepdims=True) * (1.0 / width)
    acc = jnp.zeros((rows, LANES), F32)
    for g in range(ngroups):
        sl = slice(g * A_GW, (g + 1) * A_GW)
        vc = gv_scr[:, sl] - mean
        sq = vc * vc
        acc = acc + sq[:, :LANES] + sq[:, LANES:]
    var = jnp.sum(acc, axis=-1, keepdims=True) * (1.0 / width)
    rstd = lax.rsqrt(var + NORM_EPS)
    for g in range(ngroups):
        sl = slice(g * A_GW, (g + 1) * A_GW)
        vn = (gv_scr[:, sl] - mean) * rstd * g_ref[:, sl] + b_ref[:, sl]
        if emit_v:
            vout_ref[:, sl] = vn
        vn_scr[:, sl] = vn.astype(BF16)
    for g in range(ngroups):
        sl = slice(g * A_GW, (g + 1) * A_GW)
        s = jnp.dot(wpos_ref[g], vn_scr[:, sl], preferred_element_type=F32) + bias_ref[:, sl]
        y_ref[:, sl] = (_gelu(u_ref[:, sl]) * s * _silu(z_ref[:, sl])).astype(BF16)


def gmlp_core(p, wpos, bias, ln_g, ln_b, *, emit_v):
    m, w3 = p.shape
    w = w3 // 3
    t = A_CHUNK
    col = lambda c: pl.BlockSpec((t, w), lambda i, c=c: (i, c))
    const2 = lambda shape: pl.BlockSpec(shape, lambda i: (0, 0))
    out_shape = [jax.ShapeDtypeStruct((m, w), BF16)]
    out_specs = [pl.BlockSpec((t, w), lambda i: (i, 0))]
    if emit_v:
        out_shape.append(jax.ShapeDtypeStruct((m, w), F32))
        out_specs.append(pl.BlockSpec((t, w), lambda i: (i, 0)))
    res = pl.pallas_call(
        functools.partial(_gmlp_kernel, emit_v=emit_v),
        grid=(m // t,),
        in_specs=[
            col(0), col(1), col(2),
            pl.BlockSpec(wpos.shape, lambda i: (0, 0, 0)),
            const2((t, w)), const2((1, w)), const2((1, w)),
        ],
        out_specs=out_specs,
        out_shape=out_shape,
        scratch_shapes=[pltpu.VMEM((t, w), F32), pltpu.VMEM((t, w), BF16)],
        compiler_params=_cparams(("parallel",), 40),
        name="gmlp_core",
    )(p, p, p, wpos, bias, ln_g.reshape(1, w), ln_b.reshape(1, w))
    return res if emit_v else res[0]


def _gmlp_pos_params(w_s, b_s, lc):
    pos = jnp.arange(lc)
    mask = (pos[None, :] // CHUNK) <= (pos[:, None] // CHUNK)
    wp = jnp.where(mask[None], w_s[:, :lc, :lc], 0.0)
    reps = A_CHUNK // lc
    if reps > 1:
        eye = jnp.eye(reps, dtype=wp.dtype)
        wp = jnp.einsum("ab,gts->gatbs", eye, wp).reshape(w_s.shape[0], A_CHUNK, A_CHUNK)
    bias_t = jnp.tile(b_s[:, :lc].T, (reps, 1))
    bias = jnp.repeat(bias_t, A_GW, axis=1)
    return wp.astype(BF16), bias.astype(F32)


C_TK = 128
C_SW = 256
C_TQ = 512


def _suffix_sum_matrix():
    j = jnp.arange(C_SW)
    return (j[:, None] >= j[None, :]).astype(BF16)


LOG2E = 1.4426950408889634
C_QSCALE = (C_HEAD_DIM ** -0.5) * LOG2E


def _sb_scores(q, kw):
    return lax.dot_general(q, kw, (((1,), (1,)), ((), ())), preferred_element_type=F32)


def _sb_weights(z, carry, tmat, vis):
    t, n = z.shape
    sp = jnp.maximum(z, 0.0) + jnp.log2(1.0 + jnp.exp2(-jnp.abs(z)))
    if vis is not None:
        sp = jnp.where(vis, sp, 0.0)
    sp = sp.astype(BF16)
    bounds = list(range(0, n, C_SW)) + [n]
    groups = list(zip(bounds[:-1], bounds[1:]))
    sums = {}
    for lo, hi in reversed(groups):
        sums[lo] = jnp.dot(sp[:, lo:hi], tmat[:hi - lo, :hi - lo], preferred_element_type=F32)
    ws = []
    for lo, hi in reversed(groups):
        s = sums[lo]
        for c0 in range(hi - lo - C_TK, -1, -C_TK):
            ws.append(jnp.exp2(z[:, lo + c0:lo + c0 + C_TK] - s[:, c0:c0 + C_TK] - carry))
        carry = carry + jnp.broadcast_to(s[:, 0:1], (t, C_TK))
    w = ws[0] if len(ws) == 1 else jnp.concatenate(ws[::-1], axis=1)
    if vis is not None:
        w = jnp.where(vis, w, 0.0)
    return w.astype(BF16), carry


def _sb_wide(q, kw, vw, carry, tmat, vis):
    w, carry = _sb_weights(_sb_scores(q, kw), carry, tmat, vis)
    return jnp.dot(w, vw, preferred_element_type=F32), carry


def _attn_prompt_kernel(q_ref, k_ref, v_ref, zg_ref, tmat_ref, o_ref, ko_ref, vo_ref,
                        kb_scr, vb_scr, acc_scr, carry_scr, z_scr, w_scr, *, tq):
    i = pl.program_id(1)

    def rows(j):
        return pl.ds(pl.multiple_of(jnp.maximum(j, 0) * tq, tq), tq)

    k = k_ref[...]
    v = v_ref[...]
    ko_ref[...] = k
    vo_ref[...] = v
    kb_scr[rows(i), :] = k.astype(BF16)
    vb_scr[rows(i), :] = v.astype(BF16)
    q = (q_ref[...] * C_QSCALE).astype(BF16)
    tmat = tmat_ref[...]

    def stage(j, cur):
        nxt = 1 - cur
        z_scr[nxt] = _sb_scores(q, kb_scr[rows(j - 1), :])
        w_scr[nxt], carry_scr[...] = _sb_weights(z_scr[cur], carry_scr[...], tmat, None)
        acc_scr[...] += jnp.dot(w_scr[cur], vb_scr[rows(j + 1), :], preferred_element_type=F32)

    def finish(cur):
        acc = acc_scr[...] + jnp.dot(w_scr[cur], vb_scr[rows(0), :], preferred_element_type=F32)
        o_ref[...] = (acc * _silu(zg_ref[...])).astype(o_ref.dtype)

    qpos = lax.broadcasted_iota(jnp.int32, (tq, tq), 0)
    kpos = lax.broadcasted_iota(jnp.int32, (tq, tq), 1)
    w_scr[0], carry_scr[...] = _sb_weights(
        _sb_scores(q, kb_scr[rows(i), :]), jnp.zeros((tq, C_TK), F32), tmat, kpos < qpos)
    z_scr[0] = _sb_scores(q, kb_scr[rows(i - 1), :])
    acc_scr[...] = jnp.zeros_like(acc_scr)

    def body(it, c):
        j = i - 1 - 2 * it
        stage(j, 0)
        stage(j - 1, 1)
        return c

    lax.fori_loop(0, i // 2, body, 0)

    @pl.when(i % 2 == 1)
    def _():
        stage(0, 0)
        finish(1)

    @pl.when(i % 2 == 0)
    def _():
        finish(0)


def attn_prompt(p, *, tq):
    seq, w4 = p.shape
    w = w4 // 4
    nh = w // C_HEAD_DIM
    d = C_HEAD_DIM
    blk = lambda c: pl.BlockSpec((tq, d), lambda h, i, c=c: (i, c * nh + h))
    tmat = _suffix_sum_matrix()
    return pl.pallas_call(
        functools.partial(_attn_prompt_kernel, tq=tq),
        grid=(nh, seq // tq),
        in_specs=[blk(0), blk(1), blk(2), blk(3), pl.BlockSpec(tmat.shape, lambda h, i: (0, 0))],
        out_specs=[
            pl.BlockSpec((tq, d), lambda h, i: (i, h)),
            pl.BlockSpec((None, tq, d), lambda h, i: (h, i, 0)),
            pl.BlockSpec((None, tq, d), lambda h, i: (h, i, 0)),
        ],
        out_shape=[
            jax.ShapeDtypeStruct((seq, w), BF16),
            jax.ShapeDtypeStruct((nh, seq, d), F32),
            jax.ShapeDtypeStruct((nh, seq, d), F32),
        ],
        scratch_shapes=[
            pltpu.VMEM((seq, d), BF16), pltpu.VMEM((seq, d), BF16),
            pltpu.VMEM((tq, d), F32), pltpu.VMEM((tq, C_TK), F32),
            pltpu.VMEM((2, tq, tq), F32), pltpu.VMEM((2, tq, tq), BF16),
        ],
        compiler_params=_cparams(("arbitrary", "arbitrary"), 40),
        name="attn_prompt",
    )(p, p, p, p, tmat)


def _attn_sample_kernel(q_ref, kn_ref, vn_ref, zg_ref, kc_ref, vc_ref, tmat_ref, o_ref, ko_ref, vo_ref, *, hb, past):
    t = q_ref.shape[0]
    d = C_HEAD_DIM
    tmat = tmat_ref[...]
    qpos = lax.broadcasted_iota(jnp.int32, (t, C_TK), 0)
    kpos = lax.broadcasted_iota(jnp.int32, (t, C_TK), 1)
    vis_new = kpos < qpos
    pad = jnp.zeros((C_TK - t, d), BF16)
    for h in range(hb):
        sl = slice(h * d, (h + 1) * d)
        kn = kn_ref[:, sl]
        vn = vn_ref[:, sl]
        ko_ref[h] = kn
        vo_ref[h] = vn
        q = (q_ref[:, sl] * C_QSCALE).astype(BF16)
        carry = jnp.zeros((t, C_TK), F32)
        acc, carry = _sb_wide(q, jnp.concatenate([kn.astype(BF16), pad], axis=0),
                              jnp.concatenate([vn.astype(BF16), pad], axis=0), carry, tmat, vis_new)
        pv, carry = _sb_wide(q, kc_ref[h].astype(BF16), vc_ref[h].astype(BF16), carry, tmat, None)
        acc = acc + pv
        o_ref[:, sl] = (acc * _silu(zg_ref[:, sl])).astype(o_ref.dtype)


def attn_sample(p, cache_k, cache_v, *, hb):
    bsz, nh, past, d = cache_k.shape
    t = p.shape[0] // bsz
    w = nh * d
    ng = nh // hb
    blk = lambda c: pl.BlockSpec((t, hb * d), lambda b, g, c=c: (b, c * ng + g))
    cspec = pl.BlockSpec((None, hb, past, d), lambda b, g: (b, g, 0, 0))
    nspec = pl.BlockSpec((None, hb, t, d), lambda b, g: (b, g, 0, 0))
    tmat = _suffix_sum_matrix()
    return pl.pallas_call(
        functools.partial(_attn_sample_kernel, hb=hb, past=past),
        grid=(bsz, ng),
        in_specs=[blk(0), blk(1), blk(2), blk(3), cspec, cspec, pl.BlockSpec(tmat.shape, lambda b, g: (0, 0))],
        out_specs=[pl.BlockSpec((t, hb * d), lambda b, g: (b, g)), nspec, nspec],
        out_shape=[
            jax.ShapeDtypeStruct((bsz * t, w), BF16),
            jax.ShapeDtypeStruct((bsz, nh, t, d), F32),
            jax.ShapeDtypeStruct((bsz, nh, t, d), F32),
        ],
        compiler_params=_cparams(("parallel", "parallel"), 40),
        name="attn_sample",
    )(p, p, p, p, cache_k, cache_v, tmat)


SSD_L = 128
SSD_HG = 16
SSD_PAIRS = SSD_HG // 2
SSD_W = SSD_HG * B_HEAD_DIM
SSD_GB = 2 * B_STATE
HALO = 8


def _split3(x):
    hi = x.astype(BF16)
    r = x - hi.astype(F32)
    mid = r.astype(BF16)
    lo = (r - mid.astype(F32)).astype(BF16)
    return hi, mid, lo


def _dot_x01(x, m):
    hi, mid, lo = _split3(x)
    d = lambda a: jnp.dot(a, m, preferred_element_type=F32)
    return (d(hi) + d(mid)) + d(lo)


def _dot_01x(m, x):
    hi, mid, lo = _split3(x)
    d = lambda a: jnp.dot(m, a, preferred_element_type=F32)
    return (d(hi) + d(mid)) + d(lo)


def _softplus(x):
    return jnp.maximum(x, 0.0) + jnp.log1p(jnp.exp(-jnp.abs(x)))


def _ssd_kernel(*refs, nseq, sample):
    (z_ref, x_ref, b_ref, c_ref, dt_ref, cwx_ref, cwb_ref, cwc_ref, cbx_ref, cbb_ref, cbc_ref,
     dtb_ref, alog_ref, dl_ref, nw_ref, tri_ref, same_ref, e64_ref, el_ref) = refs[:19]
    refs = refs[19:]
    if sample:
        hx_ref, hb_ref, hc_ref, s0_ref = refs[:4]
        refs = refs[4:]
    (y_ref, so_ref, extx, extb, extc, st_scr, xs_scr, bs_scr, cs_scr, ecx_scr, wex_scr, dcx_scr,
     cc_scr, cumt_scr, dtt_scr, yacc_scr) = refs
    L = SSD_L
    lseq = L // nseq
    c = pl.program_id(1)
    nchunks = pl.num_programs(1)

    if sample:
        for s in range(nseq):
            extx[s, HALO - 3:HALO, :] = hx_ref[s]
            extb[s, HALO - 3:HALO, :] = hb_ref[s]
            extc[s, HALO - 3:HALO, :] = hc_ref[s]
            for pp in range(SSD_PAIRS):
                st_scr[s, pp] = s0_ref[s, pp].T
    else:
        @pl.when(c == 0)
        def _():
            extx[0, 0:HALO, :] = jnp.zeros((HALO, SSD_W), F32)
            extb[0, 0:HALO, :] = jnp.zeros((HALO, SSD_GB), F32)
            extc[0, 0:HALO, :] = jnp.zeros((HALO, SSD_GB), F32)
            st_scr[...] = jnp.zeros_like(st_scr)

    def conv(ext, src_ref, cw_ref, cb_ref, dst):
        for s in range(nseq):
            ext[s, HALO:HALO + lseq, :] = src_ref[s * lseq:(s + 1) * lseq, :]
        for s in range(nseq):
            acc = cb_ref[...]
            for k in range(B_CONV):
                acc = acc + ext[s, HALO - 3 + k:HALO - 3 + k + lseq, :] * cw_ref[k:k + 1, :]
            dst[s * lseq:(s + 1) * lseq, :] = _silu(acc).astype(dst.dtype)
        if not sample:
            ext[0, 0:HALO, :] = ext[0, lseq:lseq + HALO, :]

    conv(extx, x_ref, cwx_ref, cbx_ref, xs_scr)
    conv(extb, b_ref, cwb_ref, cbb_ref, bs_scr)
    conv(extc, c_ref, cwc_ref, cbc_ref, cs_scr)
    yacc_scr[...] = _silu(z_ref[...])

    dt = _softplus(dt_ref[...] + dtb_ref[...])
    dta = dt * (-jnp.exp(alog_ref[...]))
    cum = _dot_01x(tri_ref[...], dta)
    if nseq == 1:
        ctot = jnp.broadcast_to(cum[L - 1:L, :], (L, LANES))
    else:
        ctot = _dot_01x(same_ref[...], dta)
    cumt_scr[...] = cum.T
    dtt_scr[...] = dt.T
    e64 = e64_ref[...]
    expand = lambda a: jnp.dot(a.astype(BF16), e64, preferred_element_type=F32)
    ecx_scr[...] = expand(jnp.exp(cum))
    wex_scr[...] = expand(jnp.exp(ctot - cum) * dt)
    drows = HALO if nseq == 1 else L
    dcx_scr[0:drows, :] = _dot_x01(jnp.exp(ctot[0:drows]), e64)
    cc_scr[...] = _dot_x01(cum, el_ref[...])
    mask = tri_ref[...] > 0
    lane = lax.broadcasted_iota(jnp.int32, (L, LANES), 1)
    first = lane < B_HEAD_DIM

    for g2 in range(2):
        bg = bs_scr[:, g2 * B_STATE:(g2 + 1) * B_STATE]
        cg = cs_scr[:, g2 * B_STATE:(g2 + 1) * B_STATE]
        cb = lax.dot_general(cg, bg, (((1,), (1,)), ((), ())), preferred_element_type=F32)
        for p in range(SSD_PAIRS // 2):
            pp = g2 * (SSD_PAIRS // 2) + p
            lanes = slice(pp * LANES, (pp + 1) * LANES)
            ms = []
            for r in (2 * pp, 2 * pp + 1):
                seg = cc_scr[:, r * L:(r + 1) * L] - cumt_scr[r:r + 1, :]
                ms.append((cb * jnp.exp(jnp.where(mask, seg, -jnp.inf)) * dtt_scr[r:r + 1, :]).astype(BF16))
            xp = xs_scr[:, lanes]
            xa = jnp.where(first, xp, 0.0).astype(BF16)
            xb = jnp.where(first, 0.0, xp).astype(BF16)
            y = jnp.dot(jnp.concatenate(ms, axis=1), jnp.concatenate([xa, xb], axis=0), preferred_element_type=F32)
            xw = (xp * wex_scr[:, lanes]).astype(BF16)
            ys = []
            for s in range(nseq):
                rows = slice(s * lseq, (s + 1) * lseq)
                st = st_scr[s, pp]
                ys.append(jnp.dot(cg[rows], st.astype(BF16), preferred_element_type=F32))
                upd = lax.dot_general(bg[rows], xw[rows], (((0,), (0,)), ((), ())), preferred_element_type=F32)
                st_scr[s, pp] = st * dcx_scr[s * lseq:s * lseq + 1, lanes] + upd
            ystate = ys[0] if nseq == 1 else jnp.concatenate(ys, axis=0)
            y = y + ystate * ecx_scr[:, lanes] + dl_ref[:, lanes] * xp
            yacc_scr[:, lanes] = y * yacc_scr[:, lanes]

    gw = SSD_W // 2
    for g2 in range(2):
        sl = slice(g2 * gw, (g2 + 1) * gw)
        yg = yacc_scr[:, sl]
        ms_ = jnp.mean(yg * yg, axis=-1, keepdims=True)
        y_ref[:, sl] = (yg * lax.rsqrt(ms_ + NORM_EPS) * nw_ref[:, sl]).astype(y_ref.dtype)

    if sample:
        for s in range(nseq):
            for pp in range(SSD_PAIRS):
                so_ref[s, pp] = st_scr[s, pp].T
    else:
        @pl.when(c == nchunks - 1)
        def _():
            for pp in range(SSD_PAIRS):
                so_ref[0, pp] = st_scr[0, pp].T


def _ssd_consts(lseq):
    L = SSD_L
    t = jnp.arange(L)
    same = (t[:, None] // lseq) == (t[None, :] // lseq)
    tri = same & (t[None, :] <= t[:, None])
    r = jnp.arange(LANES)
    e64 = (r[:, None] == (jnp.arange(SSD_W)[None, :] // B_HEAD_DIM)) & (r[:, None] < SSD_HG)
    el = (r[:, None] == (jnp.arange(SSD_HG * L)[None, :] // L)) & (r[:, None] < SSD_HG)
    return tri.astype(BF16), same.astype(BF16), e64.astype(BF16), el.astype(BF16)


def ssd_core(pz, dtp, conv_w, conv_b, dtb, alog, dlane, nw, *, lseq, conv_state=None, ssm_state=None):
    m = pz.shape[0]
    sample = conv_state is not None
    nseq = SSD_L // lseq
    nchunks = m // SSD_L
    nslices = B_HEADS // SSD_HG
    tri, same, e64, el = _ssd_consts(lseq)
    nx = 4096 // SSD_W
    rowblk = lambda w, off: pl.BlockSpec((SSD_L, w), lambda j, c, off=off: (c, off + j))
    parblk = lambda r, w, off: pl.BlockSpec((r, w), lambda j, c, off=off: (0, off + j))
    const = lambda a: pl.BlockSpec(a.shape, lambda j, c: (0, 0))
    ob = 8192 // SSD_GB
    oc = 9216 // SSD_GB
    in_specs = [
        rowblk(SSD_W, 0), rowblk(SSD_W, nx), rowblk(SSD_GB, ob), rowblk(SSD_GB, oc), rowblk(LANES, 0),
        parblk(B_CONV, SSD_W, 0), parblk(B_CONV, SSD_GB, 4096 // SSD_GB), parblk(B_CONV, SSD_GB, 5120 // SSD_GB),
        parblk(1, SSD_W, 0), parblk(1, SSD_GB, 4096 // SSD_GB), parblk(1, SSD_GB, 5120 // SSD_GB),
        parblk(1, LANES, 0), parblk(1, LANES, 0), parblk(1, SSD_W, 0), parblk(1, SSD_W, 0),
        const(tri), const(same), const(e64), const(el),
    ]
    args = [pz, pz, pz, pz, dtp, conv_w, conv_w, conv_w, conv_b, conv_b, conv_b, dtb, alog, dlane, nw, tri, same, e64, el]
    if sample:
        hblk = lambda w, off: pl.BlockSpec((nseq, B_CONV - 1, w), lambda j, c, off=off: (c, 0, off + j))
        in_specs += [hblk(SSD_W, 0), hblk(SSD_GB, 4096 // SSD_GB), hblk(SSD_GB, 5120 // SSD_GB),
                     pl.BlockSpec((nseq, SSD_PAIRS, LANES, B_STATE), lambda j, c: (c, j, 0, 0))]
        args += [conv_state, conv_state, conv_state, ssm_state]
        nstates = m // lseq
        so_spec = pl.BlockSpec((nseq, SSD_PAIRS, LANES, B_STATE), lambda j, c: (c, j, 0, 0))
    else:
        nstates = 1
        so_spec = pl.BlockSpec((1, SSD_PAIRS, LANES, B_STATE), lambda j, c: (0, j, 0, 0))
    L = SSD_L
    scratch = [
        pltpu.VMEM((nseq, HALO + lseq, SSD_W), F32), pltpu.VMEM((nseq, HALO + lseq, SSD_GB), F32),
        pltpu.VMEM((nseq, HALO + lseq, SSD_GB), F32),
        pltpu.VMEM((nseq, SSD_PAIRS, B_STATE, LANES), F32),
        pltpu.VMEM((L, SSD_W), F32), pltpu.VMEM((L, SSD_GB), BF16), pltpu.VMEM((L, SSD_GB), BF16),
        pltpu.VMEM((L, SSD_W), F32), pltpu.VMEM((L, SSD_W), F32), pltpu.VMEM((L, SSD_W), F32),
        pltpu.VMEM((L, SSD_HG * L), F32), pltpu.VMEM((LANES, L), F32), pltpu.VMEM((LANES, L), F32),
        pltpu.VMEM((L, SSD_W), F32),
    ]
    return pl.pallas_call(
        functools.partial(_ssd_kernel, nseq=nseq, sample=sample),
        grid=(nslices, nchunks),
        in_specs=in_specs,
        out_specs=[pl.BlockSpec((SSD_L, SSD_W), lambda j, c: (c, j)), so_spec],
        out_shape=[jax.ShapeDtypeStruct((m, B_HEADS * B_HEAD_DIM), BF16),
                   jax.ShapeDtypeStruct((nstates, B_HEADS // 2, LANES, B_STATE), F32)],
        scratch_shapes=scratch,
        compiler_params=_cparams(("arbitrary", "arbitrary"), 48),
        name="ssd_core",
    )(*args)


def _group_heads(v, fill=0.0):
    lead = v.shape[:-1]
    g = v.reshape(*lead, B_HEADS // SSD_HG, SSD_HG)
    g = jnp.pad(g, [(0, 0)] * (len(lead) + 1) + [(0, LANES - SSD_HG)], constant_values=fill)
    return g.reshape(*lead, (B_HEADS // SSD_HG) * LANES)


N_MIXERS = 3
PROJ_TM = 1024
PROJ_TN = 1024
OUT_TM = 256


def _run_stream(x, lseq, tm, par, state_conv=None, state_ssm=None, cache_k=None, cache_v=None):
    sample = state_conv is not None
    depth = par["norm_w"].shape[0]
    new = {"v": [], "ssm": [], "conv": [], "k": [], "v_attn": []}
    h = rms_norm_bf16(x, par["norm_w"][0], tm=tm)
    for i in range(depth):
        kind, j = i % N_MIXERS, i // N_MIXERS
        if kind == 0:
            p = matmul(h, par["a_w_in"], j, tm=tm, tn=PROJ_TN, name="a_in_proj")
            wpos, bias = _gmlp_pos_params(par["a_w_s"][j], par["a_b_s"][j], min(lseq, A_CHUNK))
            r = gmlp_core(p, wpos, bias, par["a_ln_g"][j], par["a_ln_b"][j], emit_v=sample)
            if sample:
                y, v = r
                new["v"].append(v)
            else:
                y = r
            w_out = par["a_w_out"]
        elif kind == 1:
            pz = matmul(h, par["b_w_in"], j, tm=tm, tn=PROJ_TN, n=par["b_zx"], name="b_in_proj")
            dtp = matmul(h, par["b_w_dt"], j, tm=tm, tn=par["b_w_dt"].shape[2], name="b_dt_proj")
            y, s = ssd_core(
                pz, dtp, par["b_conv_w"][j], par["b_conv_b"][j], par["b_dt_bias"][j], par["b_a_log"][j],
                par["b_d_lane"][j], par["b_norm_w"][j], lseq=min(lseq, SSD_L),
                conv_state=state_conv[j] if sample else None,
                ssm_state=state_ssm[j].reshape(-1, B_HEADS // 2, LANES, B_STATE) if sample else None)
            new["ssm"].append(s.reshape(-1, B_HEADS, B_HEAD_DIM, B_STATE))
            new["conv"].append(pz.reshape(-1, lseq, pz.shape[1])[:, lseq - (B_CONV - 1):, 4096:])
            w_out = par["b_w_out"]
        else:
            p = matmul(h, par["c_w_in"], j, tm=tm, tn=PROJ_TN, name="c_in_proj")
            if sample:
                y, k, v = attn_sample(p, cache_k[j], cache_v[j], hb=8)
            else:
                y, k, v = attn_prompt(p, tq=min(C_TQ, p.shape[0]))
                k, v = k[None], v[None]
            new["k"].append(k)
            new["v_attn"].append(v)
            w_out = par["c_w_out"]
        last = i == depth - 1
        nw = par["final_norm_w"] if last else par["norm_w"][i + 1]
        r = matmul_residual_norm(y, w_out, j, x, nw, tm=OUT_TM, final=last)
        if last:
            return r, new
        x, h = r


def kernel(x_prompt, x_sample, state_ssm, state_conv, cache_k, cache_v, norm_w, final_norm_w, a_w_in, a_ln_g, a_ln_b, a_w_s, a_b_s, a_w_out, b_w_in, b_conv_w, b_conv_b, b_dt_bias, b_a_log, b_d, b_norm_w, b_w_out, c_w_in, c_w_out):
    bp, seq, d = x_prompt.shape
    bs, dseq, _ = x_sample.shape
    nb = b_w_in.shape[0]
    zx = b_w_in.shape[2] - B_HEADS
    par = {
        "norm_w": norm_w, "final_norm_w": final_norm_w,
        "a_w_in": a_w_in, "a_ln_g": a_ln_g, "a_ln_b": a_ln_b, "a_w_s": a_w_s, "a_b_s": a_b_s,
        "a_w_out": a_w_out.astype(BF16),
        "b_w_in": b_w_in, "b_zx": zx, "b_w_dt": _group_heads(b_w_in[:, :, zx:]),
        "b_conv_w": b_conv_w, "b_conv_b": b_conv_b.reshape(nb, 1, -1),
        "b_dt_bias": _group_heads(b_dt_bias).reshape(nb, 1, -1), "b_a_log": _group_heads(b_a_log).reshape(nb, 1, -1),
        "b_d_lane": jnp.repeat(b_d, B_HEAD_DIM, axis=-1).reshape(nb, 1, -1), "b_norm_w": b_norm_w.reshape(nb, 1, -1),
        "b_w_out": b_w_out.astype(BF16), "c_w_in": c_w_in, "c_w_out": c_w_out.astype(BF16),
    }
    assert bp == 1, "the prompt group is one stream"
    yp, newp = _run_stream(x_prompt.reshape(seq, d), seq, min(PROJ_TM, seq), par)
    ys, news = _run_stream(x_sample.reshape(bs * dseq, d), dseq, min(PROJ_TM, bs * dseq), par,
                           state_conv=state_conv, state_ssm=state_ssm, cache_k=cache_k, cache_v=cache_v)
    st = jnp.stack
    return (
        yp.reshape(bp, seq, d),
        ys.reshape(bs, dseq, d),
        st([v.reshape(bs, dseq, -1) for v in news["v"]]),
        st(newp["ssm"]), st(newp["conv"]), st(news["ssm"]), st(news["conv"]),
        st(newp["k"]), st(newp["v_attn"]), st(news["k"]), st(news["v_attn"]),
    )
```

```python
import functools

import jax
import jax.numpy as jnp
from jax import lax
from jax.experimental import pallas as pl
from jax.experimental.pallas import tpu as pltpu

F32 = jnp.float32
BF16 = jnp.bfloat16

NORM_EPS = 1e-6
D_MODEL = 2048
CHUNK = 64
A_GROUPS = 16
A_CHUNK = 128
B_HEADS = 64
B_HEAD_DIM = 64
B_GROUPS = 8
B_STATE = 128
B_CONV = 4
C_HEADS = 16
C_HEAD_DIM = 128

LANES = 128
MIB = 1024 * 1024


def _cparams(sem, vmem_mib):
    return pltpu.CompilerParams(dimension_semantics=sem, vmem_limit_bytes=vmem_mib * MIB)


def _rms_kernel(x_ref, w_ref, h_ref):
    x = x_ref[...]
    ms = jnp.mean(x * x, axis=-1, keepdims=True)
    h_ref[...] = (x * lax.rsqrt(ms + NORM_EPS) * w_ref[...]).astype(h_ref.dtype)


def rms_norm_bf16(x, w, *, tm):
    m, d = x.shape
    return pl.pallas_call(
        _rms_kernel,
        grid=(m // tm,),
        in_specs=[pl.BlockSpec((tm, d), lambda i: (i, 0)), pl.BlockSpec((1, d), lambda i: (0, 0))],
        out_specs=pl.BlockSpec((tm, d), lambda i: (i, 0)),
        out_shape=jax.ShapeDtypeStruct((m, d), BF16),
        compiler_params=_cparams(("parallel",), 32),
        name="rms_norm",
    )(x, w.reshape(1, d))


def _mm_kernel(x_ref, w_ref, o_ref, wb_scr, *, head_major):
    @pl.when(pl.program_id(1) == 0)
    def _():
        wb_scr[...] = w_ref[...].astype(BF16)

    acc = jnp.dot(x_ref[...], wb_scr[...], preferred_element_type=F32)
    if head_major:
        for h in range(o_ref.shape[0]):
            o_ref[h] = acc[:, h * LANES:(h + 1) * LANES].astype(o_ref.dtype)
    else:
        o_ref[...] = acc.astype(o_ref.dtype)


def matmul(x, w, layer, *, tm, tn, n=None, col0=0, head_major=False, out_dtype=F32, name="proj"):
    m, k = x.shape
    n = w.shape[2] if n is None else n
    c0 = col0 // tn
    if head_major:
        hb = tn // LANES
        out_specs = pl.BlockSpec((hb, tm, LANES), lambda j, i: (j, i, 0))
        out_shape = jax.ShapeDtypeStruct((n // LANES, m, LANES), out_dtype)
    else:
        out_specs = pl.BlockSpec((tm, tn), lambda j, i: (i, j))
        out_shape = jax.ShapeDtypeStruct((m, n), out_dtype)
    return pl.pallas_call(
        functools.partial(_mm_kernel, head_major=head_major),
        grid=(n // tn, m // tm),
        in_specs=[pl.BlockSpec((tm, k), lambda j, i: (i, 0)),
                  pl.BlockSpec((None, k, tn), lambda j, i: (layer, 0, c0 + j))],
        out_specs=out_specs,
        out_shape=out_shape,
        scratch_shapes=[pltpu.VMEM((k, tn), BF16)],
        compiler_params=_cparams(("arbitrary", "arbitrary"), 48),
        name=name,
    )(x, w)


def _mm_res_norm_kernel(y_ref, w_ref, x_ref, nw_ref, *out_refs, final):
    xn = x_ref[...] + jnp.dot(y_ref[...], w_ref[...], preferred_element_type=F32)
    ms = jnp.mean(xn * xn, axis=-1, keepdims=True)
    h = xn * lax.rsqrt(ms + NORM_EPS) * nw_ref[...]
    if final:
        out_refs[0][...] = h
    else:
        out_refs[0][...] = xn
        out_refs[1][...] = h.astype(BF16)


def matmul_residual_norm(y, w, layer, x, nw, *, tm, final=False, name="out_proj"):
    m, k = y.shape
    d = w.shape[2]
    row = lambda i: (i, 0)
    if final:
        out_shape = jax.ShapeDtypeStruct((m, d), F32)
        out_specs = pl.BlockSpec((tm, d), row)
    else:
        out_shape = (jax.ShapeDtypeStruct((m, d), F32), jax.ShapeDtypeStruct((m, d), BF16))
        out_specs = (pl.BlockSpec((tm, d), row), pl.BlockSpec((tm, d), row))
    return pl.pallas_call(
        functools.partial(_mm_res_norm_kernel, final=final),
        grid=(m // tm,),
        in_specs=[
            pl.BlockSpec((tm, k), row),
            pl.BlockSpec((None, k, d), lambda i: (layer, 0, 0)),
            pl.BlockSpec((tm, d), row),
            pl.BlockSpec((1, d), lambda i: (0, 0)),
        ],
        out_specs=out_specs,
        out_shape=out_shape,
        compiler_params=_cparams(("parallel",), 56),
        name=name,
    )(y, w, x, nw.reshape(1, d))


def _gelu(x):
    return 0.5 * x * (1.0 + lax.erf(x * (2.0 ** -0.5)))


def _silu(x):
    return x * (1.0 / (1.0 + jnp.exp(-x)))


A_GW = 256


def _gmlp_kernel(u_ref, v_ref, z_ref, wpos_ref, bias_ref, g_ref, b_ref, *refs, emit_v):
    if emit_v:
        y_ref, vout_ref, gv_scr, vn_scr = refs
    else:
        y_ref, gv_scr, vn_scr = refs
    rows, width = gv_scr.shape
    ngroups = width // A_GW
    acc = jnp.zeros((rows, LANES), F32)
    for g in range(ngroups):
        sl = slice(g * A_GW, (g + 1) * A_GW)
        gv = _gelu(v_ref[:, sl])
        gv_scr[:, sl] = gv
        acc = acc + gv[:, :LANES] + gv[:, LANES:]
    mean = jnp.sum(acc, axis=-1, keepdims=True) * (1.0 / width)
    acc = jnp.zeros((rows, LANES), F32)
    for g in range(ngroups):
        sl = slice(g * A_GW, (g + 1) * A_GW)
        vc = gv_scr[:, sl] - mean
        sq = vc * vc
        acc = acc + sq[:, :LANES] + sq[:, LANES:]
    var = jnp.sum(acc, axis=-1, keepdims=True) * (1.0 / width)
    rstd = lax.rsqrt(var + NORM_EPS)
    for g in range(ngroups):
        sl = slice(g * A_GW, (g + 1) * A_GW)
        vn = (gv_scr[:, sl] - mean) * rstd * g_ref[:, sl] + b_ref[:, sl]
        if emit_v:
            vout_ref[:, sl] = vn
        vn_scr[:, sl] = vn.astype(BF16)
    for g in range(ngroups):
        sl = slice(g * A_GW, (g + 1) * A_GW)
        s = jnp.dot(wpos_ref[g], vn_scr[:, sl], preferred_element_type=F32) + bias_ref[:, sl]
        y_ref[:, sl] = (_gelu(u_ref[:, sl]) * s * _silu(z_ref[:, sl])).astype(BF16)


def gmlp_core(p, wpos, bias, ln_g, ln_b, *, emit_v):
    m, w3 = p.shape
    w = w3 // 3
    t = A_CHUNK
    col = lambda c: pl.BlockSpec((t, w), lambda i, c=c: (i, c))
    const2 = lambda shape: pl.BlockSpec(shape, lambda i: (0, 0))
    out_shape = [jax.ShapeDtypeStruct((m, w), BF16)]
    out_specs = [pl.BlockSpec((t, w), lambda i: (i, 0))]
    if emit_v:
        out_shape.append(jax.ShapeDtypeStruct((m, w), F32))
        out_specs.append(pl.BlockSpec((t, w), lambda i: (i, 0)))
    res = pl.pallas_call(
        functools.partial(_gmlp_kernel, emit_v=emit_v),
        grid=(m // t,),
        in_specs=[
            col(0), col(1), col(2),
            pl.BlockSpec(wpos.shape, lambda i: (0, 0, 0)),
            const2((t, w)), const2((1, w)), const2((1, w)),
        ],
        out_specs=out_specs,
        out_shape=out_shape,
        scratch_shapes=[pltpu.VMEM((t, w), F32), pltpu.VMEM((t, w), BF16)],
        compiler_params=_cparams(("parallel",), 40),
        name="gmlp_core",
    )(p, p, p, wpos, bias, ln_g.reshape(1, w), ln_b.reshape(1, w))
    return res if emit_v else res[0]


def _gmlp_pos_params(w_s, b_s, lc):
    pos = jnp.arange(lc)
    mask = (pos[None, :] // CHUNK) <= (pos[:, None] // CHUNK)
    wp = jnp.where(mask[None], w_s[:, :lc, :lc], 0.0)
    reps = A_CHUNK // lc
    if reps > 1:
        eye = jnp.eye(reps, dtype=wp.dtype)
        wp = jnp.einsum("ab,gts->gatbs", eye, wp).reshape(w_s.shape[0], A_CHUNK, A_CHUNK)
    bias_t = jnp.tile(b_s[:, :lc].T, (reps, 1))
    bias = jnp.repeat(bias_t, A_GW, axis=1)
    return wp.astype(BF16), bias.astype(F32)


C_TK = 128
C_SW = 256
C_TQ = 512


def _suffix_sum_matrix():
    j = jnp.arange(C_SW)
    return (j[:, None] >= j[None, :]).astype(BF16)


LOG2E = 1.4426950408889634
C_QSCALE = (C_HEAD_DIM ** -0.5) * LOG2E


def _sb_scores(q, kw):
    return lax.dot_general(q, kw, (((1,), (1,)), ((), ())), preferred_element_type=F32)


def _sb_weights(z, carry, tmat, vis):
    t, n = z.shape
    sp = jnp.maximum(z, 0.0) + jnp.log2(1.0 + jnp.exp2(-jnp.abs(z)))
    if vis is not None:
        sp = jnp.where(vis, sp, 0.0)
    sp = sp.astype(BF16)
    bounds = list(range(0, n, C_SW)) + [n]
    groups = list(zip(bounds[:-1], bounds[1:]))
    sums = {}
    for lo, hi in reversed(groups):
        sums[lo] = jnp.dot(sp[:, lo:hi], tmat[:hi - lo, :hi - lo], preferred_element_type=F32)
    ws = []
    for lo, hi in reversed(groups):
        s = sums[lo]
        for c0 in range(hi - lo - C_TK, -1, -C_TK):
            ws.append(jnp.exp2(z[:, lo + c0:lo + c0 + C_TK] - s[:, c0:c0 + C_TK] - carry))
        carry = carry + jnp.broadcast_to(s[:, 0:1], (t, C_TK))
    w = ws[0] if len(ws) == 1 else jnp.concatenate(ws[::-1], axis=1)
    if vis is not None:
        w = jnp.where(vis, w, 0.0)
    return w.astype(BF16), carry


def _sb_wide(q, kw, vw, carry, tmat, vis):
    w, carry = _sb_weights(_sb_scores(q, kw), carry, tmat, vis)
    return jnp.dot(w, vw, preferred_element_type=F32), carry


def _attn_prompt_kernel(q_ref, k_ref, v_ref, zg_ref, tmat_ref, o_ref,
                        kb_scr, vb_scr, acc_scr, carry_scr, z_scr, w_scr, *, tq):
    i = pl.program_id(1)

    def rows(j):
        return pl.ds(pl.multiple_of(jnp.maximum(j, 0) * tq, tq), tq)

    kb_scr[rows(i), :] = k_ref[...].astype(BF16)
    vb_scr[rows(i), :] = v_ref[...].astype(BF16)
    q = (q_ref[...] * C_QSCALE).astype(BF16)
    tmat = tmat_ref[...]

    def stage(j, cur):
        nxt = 1 - cur
        z_scr[nxt] = _sb_scores(q, kb_scr[rows(j - 1), :])
        w_scr[nxt], carry_scr[...] = _sb_weights(z_scr[cur], carry_scr[...], tmat, None)
        acc_scr[...] += jnp.dot(w_scr[cur], vb_scr[rows(j + 1), :], preferred_element_type=F32)

    def finish(cur):
        acc = acc_scr[...] + jnp.dot(w_scr[cur], vb_scr[rows(0), :], preferred_element_type=F32)
        o_ref[...] = (acc * _silu(zg_ref[...])).astype(o_ref.dtype)

    qpos = lax.broadcasted_iota(jnp.int32, (tq, tq), 0)
    kpos = lax.broadcasted_iota(jnp.int32, (tq, tq), 1)
    w_scr[0], carry_scr[...] = _sb_weights(
        _sb_scores(q, kb_scr[rows(i), :]), jnp.zeros((tq, C_TK), F32), tmat, kpos < qpos)
    z_scr[0] = _sb_scores(q, kb_scr[rows(i - 1), :])
    acc_scr[...] = jnp.zeros_like(acc_scr)

    def body(it, c):
        j = i - 1 - 2 * it
        stage(j, 0)
        stage(j - 1, 1)
        return c

    lax.fori_loop(0, i // 2, body, 0)

    @pl.when(i % 2 == 1)
    def _():
        stage(0, 0)
        finish(1)

    @pl.when(i % 2 == 0)
    def _():
        finish(0)


def attn_prompt(q, k, v, zg, *, tq):
    nh, seq, d = q.shape
    hblk = pl.BlockSpec((None, tq, d), lambda h, i: (h, i, 0))
    tmat = _suffix_sum_matrix()
    return pl.pallas_call(
        functools.partial(_attn_prompt_kernel, tq=tq),
        grid=(nh, seq // tq),
        in_specs=[hblk, hblk, hblk, pl.BlockSpec((tq, d), lambda h, i: (i, h)),
                  pl.BlockSpec(tmat.shape, lambda h, i: (0, 0))],
        out_specs=pl.BlockSpec((tq, d), lambda h, i: (i, h)),
        out_shape=jax.ShapeDtypeStruct((seq, nh * d), BF16),
        scratch_shapes=[
            pltpu.VMEM((seq, d), BF16), pltpu.VMEM((seq, d), BF16),
            pltpu.VMEM((tq, d), F32), pltpu.VMEM((tq, C_TK), F32),
            pltpu.VMEM((2, tq, tq), F32), pltpu.VMEM((2, tq, tq), BF16),
        ],
        compiler_params=_cparams(("arbitrary", "arbitrary"), 56),
        name="attn_prompt",
    )(q, k, v, zg, tmat)


def _attn_sample_kernel(q_ref, kn_ref, vn_ref, zg_ref, kc_ref, vc_ref, tmat_ref, o_ref, ko_ref, vo_ref, *, hb, past):
    t = q_ref.shape[0]
    d = C_HEAD_DIM
    tmat = tmat_ref[...]
    qpos = lax.broadcasted_iota(jnp.int32, (t, C_TK), 0)
    kpos = lax.broadcasted_iota(jnp.int32, (t, C_TK), 1)
    vis_new = kpos < qpos
    pad = jnp.zeros((C_TK - t, d), BF16)
    for h in range(hb):
        sl = slice(h * d, (h + 1) * d)
        kn = kn_ref[:, sl]
        vn = vn_ref[:, sl]
        ko_ref[h] = kn
        vo_ref[h] = vn
        q = (q_ref[:, sl] * C_QSCALE).astype(BF16)
        carry = jnp.zeros((t, C_TK), F32)
        acc, carry = _sb_wide(q, jnp.concatenate([kn.astype(BF16), pad], axis=0),
                              jnp.concatenate([vn.astype(BF16), pad], axis=0), carry, tmat, vis_new)
        pv, carry = _sb_wide(q, kc_ref[h].astype(BF16), vc_ref[h].astype(BF16), carry, tmat, None)
        acc = acc + pv
        o_ref[:, sl] = (acc * _silu(zg_ref[:, sl])).astype(o_ref.dtype)


def attn_sample(p, cache_k, cache_v, *, hb):
    bsz, nh, past, d = cache_k.shape
    t = p.shape[0] // bsz
    w = nh * d
    ng = nh // hb
    blk = lambda c: pl.BlockSpec((t, hb * d), lambda b, g, c=c: (b, c * ng + g))
    cspec = pl.BlockSpec((None, hb, past, d), lambda b, g: (b, g, 0, 0))
    nspec = pl.BlockSpec((None, hb, t, d), lambda b, g: (b, g, 0, 0))
    tmat = _suffix_sum_matrix()
    return pl.pallas_call(
        functools.partial(_attn_sample_kernel, hb=hb, past=past),
        grid=(bsz, ng),
        in_specs=[blk(0), blk(1), blk(2), blk(3), cspec, cspec, pl.BlockSpec(tmat.shape, lambda b, g: (0, 0))],
        out_specs=[pl.BlockSpec((t, hb * d), lambda b, g: (b, g)), nspec, nspec],
        out_shape=[
            jax.ShapeDtypeStruct((bsz * t, w), BF16),
            jax.ShapeDtypeStruct((bsz, nh, t, d), F32),
            jax.ShapeDtypeStruct((bsz, nh, t, d), F32),
        ],
        compiler_params=_cparams(("parallel", "parallel"), 40),
        name="attn_sample",
    )(p, p, p, p, cache_k, cache_v, tmat)


SSD_L = 128
SSD_HG = 16
SSD_PAIRS = SSD_HG // 2
SSD_W = SSD_HG * B_HEAD_DIM
SSD_GB = 2 * B_STATE
HALO = 8


def _split3(x):
    hi = x.astype(BF16)
    r = x - hi.astype(F32)
    mid = r.astype(BF16)
    lo = (r - mid.astype(F32)).astype(BF16)
    return hi, mid, lo


def _dot_x01(x, m):
    hi, mid, lo = _split3(x)
    d = lambda a: jnp.dot(a, m, preferred_element_type=F32)
    return (d(hi) + d(mid)) + d(lo)


def _dot_01x(m, x):
    hi, mid, lo = _split3(x)
    d = lambda a: jnp.dot(m, a, preferred_element_type=F32)
    return (d(hi) + d(mid)) + d(lo)


def _softplus(x):
    return jnp.maximum(x, 0.0) + jnp.log1p(jnp.exp(-jnp.abs(x)))


def _ssd_kernel(*refs, nseq, sample):
    (z_ref, x_ref, b_ref, c_ref, dt_ref, cwx_ref, cwb_ref, cwc_ref, cbx_ref, cbb_ref, cbc_ref,
     dtb_ref, alog_ref, dl_ref, nw_ref, tri_ref, same_ref, e64_ref, el_ref) = refs[:19]
    refs = refs[19:]
    if sample:
        hx_ref, hb_ref, hc_ref, s0_ref = refs[:4]
        refs = refs[4:]
    (y_ref, so_ref, extx, extb, extc, st_scr, xs_scr, bs_scr, cs_scr, ecx_scr, wex_scr, dcx_scr,
     cc_scr, cumt_scr, dtt_scr, yacc_scr) = refs
    L = SSD_L
    lseq = L // nseq
    c = pl.program_id(1)
    nchunks = pl.num_programs(1)

    if sample:
        for s in range(nseq):
            extx[s, HALO - 3:HALO, :] = hx_ref[s]
            extb[s, HALO - 3:HALO, :] = hb_ref[s]
            extc[s, HALO - 3:HALO, :] = hc_ref[s]
            for pp in range(SSD_PAIRS):
                st_scr[s, pp] = s0_ref[s, pp].T
    else:
        @pl.when(c == 0)
        def _():
            extx[0, 0:HALO, :] = jnp.zeros((HALO, SSD_W), F32)
            extb[0, 0:HALO, :] = jnp.zeros((HALO, SSD_GB), F32)
            extc[0, 0:HALO, :] = jnp.zeros((HALO, SSD_GB), F32)
            st_scr[...] = jnp.zeros_like(st_scr)

    def conv(ext, src_ref, cw_ref, cb_ref, dst):
        for s in range(nseq):
            ext[s, HALO:HALO + lseq, :] = src_ref[s * lseq:(s + 1) * lseq, :]
        for s in range(nseq):
            acc = cb_ref[...]
            for k in range(B_CONV):
                acc = acc + ext[s, HALO - 3 + k:HALO - 3 + k + lseq, :] * cw_ref[k:k + 1, :]
            dst[s * lseq:(s + 1) * lseq, :] = _silu(acc).astype(dst.dtype)
        if not sample:
            ext[0, 0:HALO, :] = ext[0, lseq:lseq + HALO, :]

    conv(extx, x_ref, cwx_ref, cbx_ref, xs_scr)
    conv(extb, b_ref, cwb_ref, cbb_ref, bs_scr)
    conv(extc, c_ref, cwc_ref, cbc_ref, cs_scr)
    yacc_scr[...] = _silu(z_ref[...])

    dt = _softplus(dt_ref[...] + dtb_ref[...])
    dta = dt * (-jnp.exp(alog_ref[...]))
    cum = _dot_01x(tri_ref[...], dta)
    if nseq == 1:
        ctot = jnp.broadcast_to(cum[L - 1:L, :], (L, LANES))
    else:
        ctot = _dot_01x(same_ref[...], dta)
    cumt_scr[...] = cum.T
    dtt_scr[...] = dt.T
    e64 = e64_ref[...]
    expand = lambda a: jnp.dot(a.astype(BF16), e64, preferred_element_type=F32)
    ecx_scr[...] = expand(jnp.exp(cum))
    wex_scr[...] = expand(jnp.exp(ctot - cum) * dt)
    drows = HALO if nseq == 1 else L
    dcx_scr[0:drows, :] = _dot_x01(jnp.exp(ctot[0:drows]), e64)
    cc_scr[...] = _dot_x01(cum, el_ref[...])
    mask = tri_ref[...] > 0
    lane = lax.broadcasted_iota(jnp.int32, (L, LANES), 1)
    first = lane < B_HEAD_DIM

    for g2 in range(2):
        bg = bs_scr[:, g2 * B_STATE:(g2 + 1) * B_STATE]
        cg = cs_scr[:, g2 * B_STATE:(g2 + 1) * B_STATE]
        cb = lax.dot_general(cg, bg, (((1,), (1,)), ((), ())), preferred_element_type=F32)
        for p in range(SSD_PAIRS // 2):
            pp = g2 * (SSD_PAIRS // 2) + p
            lanes = slice(pp * LANES, (pp + 1) * LANES)
            ms = []
            for r in (2 * pp, 2 * pp + 1):
                seg = cc_scr[:, r * L:(r + 1) * L] - cumt_scr[r:r + 1, :]
                ms.append((cb * jnp.exp(jnp.where(mask, seg, -jnp.inf)) * dtt_scr[r:r + 1, :]).astype(BF16))
            xp = xs_scr[:, lanes]
            xa = jnp.where(first, xp, 0.0).astype(BF16)
            xb = jnp.where(first, 0.0, xp).astype(BF16)
            y = jnp.dot(jnp.concatenate(ms, axis=1), jnp.concatenate([xa, xb], axis=0), preferred_element_type=F32)
            xw = (xp * wex_scr[:, lanes]).astype(BF16)
            ys = []
            for s in range(nseq):
                rows = slice(s * lseq, (s + 1) * lseq)
                st = st_scr[s, pp]
                ys.append(jnp.dot(cg[rows], st.astype(BF16), preferred_element_type=F32))
                upd = lax.dot_general(bg[rows], xw[rows], (((0,), (0,)), ((), ())), preferred_element_type=F32)
                st_scr[s, pp] = st * dcx_scr[s * lseq:s * lseq + 1, lanes] + upd
            ystate = ys[0] if nseq == 1 else jnp.concatenate(ys, axis=0)
            y = y + ystate * ecx_scr[:, lanes] + dl_ref[:, lanes] * xp
            yacc_scr[:, lanes] = y * yacc_scr[:, lanes]

    gw = SSD_W // 2
    for g2 in range(2):
        sl = slice(g2 * gw, (g2 + 1) * gw)
        yg = yacc_scr[:, sl]
        ms_ = jnp.mean(yg * yg, axis=-1, keepdims=True)
        y_ref[:, sl] = (yg * lax.rsqrt(ms_ + NORM_EPS) * nw_ref[:, sl]).astype(y_ref.dtype)

    if sample:
        for s in range(nseq):
            for pp in range(SSD_PAIRS):
                so_ref[s, pp] = st_scr[s, pp].T
    else:
        @pl.when(c == nchunks - 1)
        def _():
            for pp in range(SSD_PAIRS):
                so_ref[0, pp] = st_scr[0, pp].T


def _ssd_consts(lseq):
    L = SSD_L
    t = jnp.arange(L)
    same = (t[:, None] // lseq) == (t[None, :] // lseq)
    tri = same & (t[None, :] <= t[:, None])
    r = jnp.arange(LANES)
    e64 = (r[:, None] == (jnp.arange(SSD_W)[None, :] // B_HEAD_DIM)) & (r[:, None] < SSD_HG)
    el = (r[:, None] == (jnp.arange(SSD_HG * L)[None, :] // L)) & (r[:, None] < SSD_HG)
    return tri.astype(BF16), same.astype(BF16), e64.astype(BF16), el.astype(BF16)


def ssd_core(pz, dtp, conv_w, conv_b, dtb, alog, dlane, nw, *, lseq, conv_state=None, ssm_state=None):
    m = pz.shape[0]
    sample = conv_state is not None
    nseq = SSD_L // lseq
    nchunks = m // SSD_L
    nslices = B_HEADS // SSD_HG
    tri, same, e64, el = _ssd_consts(lseq)
    nx = 4096 // SSD_W
    rowblk = lambda w, off: pl.BlockSpec((SSD_L, w), lambda j, c, off=off: (c, off + j))
    parblk = lambda r, w, off: pl.BlockSpec((r, w), lambda j, c, off=off: (0, off + j))
    const = lambda a: pl.BlockSpec(a.shape, lambda j, c: (0, 0))
    ob = 8192 // SSD_GB
    oc = 9216 // SSD_GB
    in_specs = [
        rowblk(SSD_W, 0), rowblk(SSD_W, nx), rowblk(SSD_GB, ob), rowblk(SSD_GB, oc), rowblk(LANES, 0),
        parblk(B_CONV, SSD_W, 0), parblk(B_CONV, SSD_GB, 4096 // SSD_GB), parblk(B_CONV, SSD_GB, 5120 // SSD_GB),
        parblk(1, SSD_W, 0), parblk(1, SSD_GB, 4096 // SSD_GB), parblk(1, SSD_GB, 5120 // SSD_GB),
        parblk(1, LANES, 0), parblk(1, LANES, 0), parblk(1, SSD_W, 0), parblk(1, SSD_W, 0),
        const(tri), const(same), const(e64), const(el),
    ]
    args = [pz, pz, pz, pz, dtp, conv_w, conv_w, conv_w, conv_b, conv_b, conv_b, dtb, alog, dlane, nw, tri, same, e64, el]
    if sample:
        hblk = lambda w, off: pl.BlockSpec((nseq, B_CONV - 1, w), lambda j, c, off=off: (c, 0, off + j))
        in_specs += [hblk(SSD_W, 0), hblk(SSD_GB, 4096 // SSD_GB), hblk(SSD_GB, 5120 // SSD_GB),
                     pl.BlockSpec((nseq, SSD_PAIRS, LANES, B_STATE), lambda j, c: (c, j, 0, 0))]
        args += [conv_state, conv_state, conv_state, ssm_state]
        nstates = m // lseq
        so_spec = pl.BlockSpec((nseq, SSD_PAIRS, LANES, B_STATE), lambda j, c: (c, j, 0, 0))
    else:
        nstates = 1
        so_spec = pl.BlockSpec((1, SSD_PAIRS, LANES, B_STATE), lambda j, c: (0, j, 0, 0))
    L = SSD_L
    scratch = [
        pltpu.VMEM((nseq, HALO + lseq, SSD_W), F32), pltpu.VMEM((nseq, HALO + lseq, SSD_GB), F32),
        pltpu.VMEM((nseq, HALO + lseq, SSD_GB), F32),
        pltpu.VMEM((nseq, SSD_PAIRS, B_STATE, LANES), F32),
        pltpu.VMEM((L, SSD_W), F32), pltpu.VMEM((L, SSD_GB), BF16), pltpu.VMEM((L, SSD_GB), BF16),
        pltpu.VMEM((L, SSD_W), F32), pltpu.VMEM((L, SSD_W), F32), pltpu.VMEM((L, SSD_W), F32),
        pltpu.VMEM((L, SSD_HG * L), F32), pltpu.VMEM((LANES, L), F32), pltpu.VMEM((LANES, L), F32),
        pltpu.VMEM((L, SSD_W), F32),
    ]
    return pl.pallas_call(
        functools.partial(_ssd_kernel, nseq=nseq, sample=sample),
        grid=(nslices, nchunks),
        in_specs=in_specs,
        out_specs=[pl.BlockSpec((SSD_L, SSD_W), lambda j, c: (c, j)), so_spec],
        out_shape=[jax.ShapeDtypeStruct((m, B_HEADS * B_HEAD_DIM), BF16),
                   jax.ShapeDtypeStruct((nstates, B_HEADS // 2, LANES, B_STATE), F32)],
        scratch_shapes=scratch,
        compiler_params=_cparams(("arbitrary", "arbitrary"), 48),
        name="ssd_core",
    )(*args)


def _group_heads(v, fill=0.0):
    lead = v.shape[:-1]
    g = v.reshape(*lead, B_HEADS // SSD_HG, SSD_HG)
    g = jnp.pad(g, [(0, 0)] * (len(lead) + 1) + [(0, LANES - SSD_HG)], constant_values=fill)
    return g.reshape(*lead, (B_HEADS // SSD_HG) * LANES)


N_MIXERS = 3
PROJ_TM = 1024
PROJ_TN = 1024
OUT_TM = 256


def _run_stream(x, lseq, tm, par, state_conv=None, state_ssm=None, cache_k=None, cache_v=None):
    sample = state_conv is not None
    depth = par["norm_w"].shape[0]
    new = {"v": [], "ssm": [], "conv": [], "k": [], "v_attn": []}
    h = rms_norm_bf16(x, par["norm_w"][0], tm=tm)
    for i in range(depth):
        kind, j = i % N_MIXERS, i // N_MIXERS
        if kind == 0:
            p = matmul(h, par["a_w_in"], j, tm=tm, tn=PROJ_TN, name="a_in_proj")
            wpos, bias = _gmlp_pos_params(par["a_w_s"][j], par["a_b_s"][j], min(lseq, A_CHUNK))
            r = gmlp_core(p, wpos, bias, par["a_ln_g"][j], par["a_ln_b"][j], emit_v=sample)
            if sample:
                y, v = r
                new["v"].append(v)
            else:
                y = r
            w_out = par["a_w_out"]
        elif kind == 1:
            pz = matmul(h, par["b_w_in"], j, tm=tm, tn=PROJ_TN, n=par["b_zx"], name="b_in_proj")
            dtp = matmul(h, par["b_w_dt"], j, tm=tm, tn=par["b_w_dt"].shape[2], name="b_dt_proj")
            y, s = ssd_core(
                pz, dtp, par["b_conv_w"][j], par["b_conv_b"][j], par["b_dt_bias"][j], par["b_a_log"][j],
                par["b_d_lane"][j], par["b_norm_w"][j], lseq=min(lseq, SSD_L),
                conv_state=state_conv[j] if sample else None,
                ssm_state=state_ssm[j].reshape(-1, B_HEADS // 2, LANES, B_STATE) if sample else None)
            new["ssm"].append(s.reshape(-1, B_HEADS, B_HEAD_DIM, B_STATE))
            new["conv"].append(pz.reshape(-1, lseq, pz.shape[1])[:, lseq - (B_CONV - 1):, 4096:])
            w_out = par["b_w_out"]
        else:
            if sample:
                p = matmul(h, par["c_w_in"], j, tm=tm, tn=PROJ_TN, name="c_in_proj")
                y, k, v = attn_sample(p, cache_k[j], cache_v[j], hb=8)
            else:
                cw = par["c_w_in"].shape[2] // 4
                part = lambda c, hm: matmul(h, par["c_w_in"], j, tm=tm, tn=PROJ_TN, n=cw, col0=c * cw,
                                            head_major=hm, name="c_in_proj")
                q, k, v, zg = part(0, True), part(1, True), part(2, True), part(3, False)
                y = attn_prompt(q, k, v, zg, tq=min(C_TQ, x.shape[0]))
                k, v = k[None], v[None]
            new["k"].append(k)
            new["v_attn"].append(v)
            w_out = par["c_w_out"]
        last = i == depth - 1
        nw = par["final_norm_w"] if last else par["norm_w"][i + 1]
        r = matmul_residual_norm(y, w_out, j, x, nw, tm=OUT_TM, final=last)
        if last:
            return r, new
        x, h = r


def kernel(x_prompt, x_sample, state_ssm, state_conv, cache_k, cache_v, norm_w, final_norm_w, a_w_in, a_ln_g, a_ln_b, a_w_s, a_b_s, a_w_out, b_w_in, b_conv_w, b_conv_b, b_dt_bias, b_a_log, b_d, b_norm_w, b_w_out, c_w_in, c_w_out):
    bp, seq, d = x_prompt.shape
    bs, dseq, _ = x_sample.shape
    nb = b_w_in.shape[0]
    zx = b_w_in.shape[2] - B_HEADS
    par = {
        "norm_w": norm_w, "final_norm_w": final_norm_w,
        "a_w_in": a_w_in, "a_ln_g": a_ln_g, "a_ln_b": a_ln_b, "a_w_s": a_w_s, "a_b_s": a_b_s,
        "a_w_out": a_w_out.astype(BF16),
        "b_w_in": b_w_in, "b_zx": zx, "b_w_dt": _group_heads(b_w_in[:, :, zx:]),
        "b_conv_w": b_conv_w, "b_conv_b": b_conv_b.reshape(nb, 1, -1),
        "b_dt_bias": _group_heads(b_dt_bias).reshape(nb, 1, -1), "b_a_log": _group_heads(b_a_log).reshape(nb, 1, -1),
        "b_d_lane": jnp.repeat(b_d, B_HEAD_DIM, axis=-1).reshape(nb, 1, -1), "b_norm_w": b_norm_w.reshape(nb, 1, -1),
        "b_w_out": b_w_out.astype(BF16), "c_w_in": c_w_in, "c_w_out": c_w_out.astype(BF16),
    }
    assert bp == 1, "the prompt group is one stream"
    yp, newp = _run_stream(x_prompt.reshape(seq, d), seq, min(PROJ_TM, seq), par)
    ys, news = _run_stream(x_sample.reshape(bs * dseq, d), dseq, min(PROJ_TM, bs * dseq), par,
                           state_conv=state_conv, state_ssm=state_ssm, cache_k=cache_k, cache_v=cache_v)
    st = jnp.stack
    return (
        yp.reshape(bp, seq, d),
        ys.reshape(bs, dseq, d),
        st([v.reshape(bs, dseq, -1) for v in news["v"]]),
        st(newp["ssm"]), st(newp["conv"]), st(news["ssm"]), st(news["conv"]),
        st(newp["k"]), st(newp["v_attn"]), st(news["k"]), st(news["v_attn"]),
    )
```

```python
import functools

import jax
import jax.numpy as jnp
from jax import lax
from jax.experimental import pallas as pl
from jax.experimental.pallas import tpu as pltpu

F32 = jnp.float32
BF16 = jnp.bfloat16

NORM_EPS = 1e-6
D_MODEL = 2048
CHUNK = 64
A_GROUPS = 16
A_CHUNK = 128
B_HEADS = 64
B_HEAD_DIM = 64
B_GROUPS = 8
B_STATE = 128
B_CONV = 4
C_HEADS = 16
C_HEAD_DIM = 128

LANES = 128
MIB = 1024 * 1024


def _cparams(sem, vmem_mib):
    return pltpu.CompilerParams(dimension_semantics=sem, vmem_limit_bytes=vmem_mib * MIB)


def _rms_kernel(x_ref, w_ref, h_ref):
    x = x_ref[...]
    ms = jnp.mean(x * x, axis=-1, keepdims=True)
    h_ref[...] = (x * lax.rsqrt(ms + NORM_EPS) * w_ref[...]).astype(h_ref.dtype)


def rms_norm_bf16(x, w, *, tm):
    m, d = x.shape
    return pl.pallas_call(
        _rms_kernel,
        grid=(m // tm,),
        in_specs=[pl.BlockSpec((tm, d), lambda i: (i, 0)), pl.BlockSpec((1, d), lambda i: (0, 0))],
        out_specs=pl.BlockSpec((tm, d), lambda i: (i, 0)),
        out_shape=jax.ShapeDtypeStruct((m, d), BF16),
        compiler_params=_cparams(("parallel",), 32),
        name="rms_norm",
    )(x, w.reshape(1, d))


def _mm_kernel(x_ref, w_ref, o_ref, wb_scr, *, head_major):
    @pl.when(pl.program_id(1) == 0)
    def _():
        wb_scr[...] = w_ref[...].astype(BF16)

    acc = jnp.dot(x_ref[...], wb_scr[...], preferred_element_type=F32)
    if head_major:
        for h in range(o_ref.shape[0]):
            o_ref[h] = acc[:, h * LANES:(h + 1) * LANES].astype(o_ref.dtype)
    else:
        o_ref[...] = acc.astype(o_ref.dtype)


def matmul(x, w, layer, *, tm, tn, n=None, col0=0, head_major=False, out_dtype=F32, name="proj"):
    m, k = x.shape
    n = w.shape[2] if n is None else n
    c0 = col0 // tn
    if head_major:
        hb = tn // LANES
        out_specs = pl.BlockSpec((hb, tm, LANES), lambda j, i: (j, i, 0))
        out_shape = jax.ShapeDtypeStruct((n // LANES, m, LANES), out_dtype)
    else:
        out_specs = pl.BlockSpec((tm, tn), lambda j, i: (i, j))
        out_shape = jax.ShapeDtypeStruct((m, n), out_dtype)
    return pl.pallas_call(
        functools.partial(_mm_kernel, head_major=head_major),
        grid=(n // tn, m // tm),
        in_specs=[pl.BlockSpec((tm, k), lambda j, i: (i, 0)),
                  pl.BlockSpec((None, k, tn), lambda j, i: (layer, 0, c0 + j))],
        out_specs=out_specs,
        out_shape=out_shape,
        scratch_shapes=[pltpu.VMEM((k, tn), BF16)],
        compiler_params=_cparams(("arbitrary", "arbitrary"), 48),
        name=name,
    )(x, w)


def _mm_res_norm_kernel(y_ref, w_ref, x_ref, nw_ref, *out_refs, final):
    xn = x_ref[...] + jnp.dot(y_ref[...], w_ref[...], preferred_element_type=F32)
    ms = jnp.mean(xn * xn, axis=-1, keepdims=True)
    h = xn * lax.rsqrt(ms + NORM_EPS) * nw_ref[...]
    if final:
        out_refs[0][...] = h
    else:
        out_refs[0][...] = xn
        out_refs[1][...] = h.astype(BF16)


def matmul_residual_norm(y, w, layer, x, nw, *, tm, final=False, name="out_proj"):
    m, k = y.shape
    d = w.shape[2]
    row = lambda i: (i, 0)
    if final:
        out_shape = jax.ShapeDtypeStruct((m, d), F32)
        out_specs = pl.BlockSpec((tm, d), row)
    else:
        out_shape = (jax.ShapeDtypeStruct((m, d), F32), jax.ShapeDtypeStruct((m, d), BF16))
        out_specs = (pl.BlockSpec((tm, d), row), pl.BlockSpec((tm, d), row))
    return pl.pallas_call(
        functools.partial(_mm_res_norm_kernel, final=final),
        grid=(m // tm,),
        in_specs=[
            pl.BlockSpec((tm, k), row),
            pl.BlockSpec((None, k, d), lambda i: (layer, 0, 0), pipeline_mode=pl.Buffered(1)),
            pl.BlockSpec((tm, d), row),
            pl.BlockSpec((1, d), lambda i: (0, 0)),
        ],
        out_specs=out_specs,
        out_shape=out_shape,
        compiler_params=_cparams(("parallel",), 56),
        name=name,
    )(y, w, x, nw.reshape(1, d))


def _gelu(x):
    return 0.5 * x * (1.0 + lax.erf(x * (2.0 ** -0.5)))


def _silu(x):
    return x * (1.0 / (1.0 + jnp.exp(-x)))


A_GW = 256


def _gmlp_kernel(u_ref, v_ref, z_ref, wpos_ref, bias_ref, g_ref, b_ref, *refs, emit_v):
    if emit_v:
        y_ref, vout_ref, gv_scr, vn_scr = refs
    else:
        y_ref, gv_scr, vn_scr = refs
    rows, width = gv_scr.shape
    ngroups = width // A_GW
    acc = jnp.zeros((rows, LANES), F32)
    for g in range(ngroups):
        sl = slice(g * A_GW, (g + 1) * A_GW)
        gv = _gelu(v_ref[:, sl])
        gv_scr[:, sl] = gv
        acc = acc + gv[:, :LANES] + gv[:, LANES:]
    mean = jnp.sum(acc, axis=-1, keepdims=True) * (1.0 / width)
    acc = jnp.zeros((rows, LANES), F32)
    for g in range(ngroups):
        sl = slice(g * A_GW, (g + 1) * A_GW)
        vc = gv_scr[:, sl] - mean
        sq = vc * vc
        acc = acc + sq[:, :LANES] + sq[:, LANES:]
    var = jnp.sum(acc, axis=-1, keepdims=True) * (1.0 / width)
    rstd = lax.rsqrt(var + NORM_EPS)
    for g in range(ngroups):
        sl = slice(g * A_GW, (g + 1) * A_GW)
        vn = (gv_scr[:, sl] - mean) * rstd * g_ref[:, sl] + b_ref[:, sl]
        if emit_v:
            vout_ref[:, sl] = vn
        vn_scr[:, sl] = vn.astype(BF16)
    for g in range(ngroups):
        sl = slice(g * A_GW, (g + 1) * A_GW)
        s = jnp.dot(wpos_ref[g], vn_scr[:, sl], preferred_element_type=F32) + bias_ref[:, sl]
        y_ref[:, sl] = (_gelu(u_ref[:, sl]) * s * _silu(z_ref[:, sl])).astype(BF16)


def gmlp_core(p, wpos, bias, ln_g, ln_b, *, emit_v):
    m, w3 = p.shape
    w = w3 // 3
    t = A_CHUNK
    col = lambda c: pl.BlockSpec((t, w), lambda i, c=c: (i, c))
    const2 = lambda shape: pl.BlockSpec(shape, lambda i: (0, 0))
    out_shape = [jax.ShapeDtypeStruct((m, w), BF16)]
    out_specs = [pl.BlockSpec((t, w), lambda i: (i, 0))]
    if emit_v:
        out_shape.append(jax.ShapeDtypeStruct((m, w), F32))
        out_specs.append(pl.BlockSpec((t, w), lambda i: (i, 0)))
    res = pl.pallas_call(
        functools.partial(_gmlp_kernel, emit_v=emit_v),
        grid=(m // t,),
        in_specs=[
            col(0), col(1), col(2),
            pl.BlockSpec(wpos.shape, lambda i: (0, 0, 0)),
            const2((t, w)), const2((1, w)), const2((1, w)),
        ],
        out_specs=out_specs,
        out_shape=out_shape,
        scratch_shapes=[pltpu.VMEM((t, w), F32), pltpu.VMEM((t, w), BF16)],
        compiler_params=_cparams(("parallel",), 40),
        name="gmlp_core",
    )(p, p, p, wpos, bias, ln_g.reshape(1, w), ln_b.reshape(1, w))
    return res if emit_v else res[0]


def _gmlp_pos_params(w_s, b_s, lc):
    pos = jnp.arange(lc)
    mask = (pos[None, :] // CHUNK) <= (pos[:, None] // CHUNK)
    wp = jnp.where(mask[None], w_s[:, :lc, :lc], 0.0)
    reps = A_CHUNK // lc
    if reps > 1:
        eye = jnp.eye(reps, dtype=wp.dtype)
        wp = jnp.einsum("ab,gts->gatbs", eye, wp).reshape(w_s.shape[0], A_CHUNK, A_CHUNK)
    bias_t = jnp.tile(b_s[:, :lc].T, (reps, 1))
    bias = jnp.repeat(bias_t, A_GW, axis=1)
    return wp.astype(BF16), bias.astype(F32)


C_TK = 128
C_SW = 256
C_TQ = 512


def _suffix_sum_matrix():
    j = jnp.arange(C_SW)
    return (j[:, None] >= j[None, :]).astype(BF16)


LOG2E = 1.4426950408889634
C_QSCALE = (C_HEAD_DIM ** -0.5) * LOG2E


def _sb_scores(q, kw):
    return lax.dot_general(q, kw, (((1,), (1,)), ((), ())), preferred_element_type=F32)


def _sb_weights(z, carry, tmat, vis):
    t, n = z.shape
    sp = jnp.maximum(z, 0.0) + jnp.log2(1.0 + jnp.exp2(-jnp.abs(z)))
    if vis is not None:
        sp = jnp.where(vis, sp, 0.0)
    sp = sp.astype(BF16)
    bounds = list(range(0, n, C_SW)) + [n]
    groups = list(zip(bounds[:-1], bounds[1:]))
    sums = {}
    for lo, hi in reversed(groups):
        sums[lo] = jnp.dot(sp[:, lo:hi], tmat[:hi - lo, :hi - lo], preferred_element_type=F32)
    ws = []
    for lo, hi in reversed(groups):
        s = sums[lo]
        for c0 in range(hi - lo - C_TK, -1, -C_TK):
            ws.append(jnp.exp2(z[:, lo + c0:lo + c0 + C_TK] - s[:, c0:c0 + C_TK] - carry))
        carry = carry + jnp.broadcast_to(s[:, 0:1], (t, C_TK))
    w = ws[0] if len(ws) == 1 else jnp.concatenate(ws[::-1], axis=1)
    if vis is not None:
        w = jnp.where(vis, w, 0.0)
    return w.astype(BF16), carry


def _sb_wide(q, kw, vw, carry, tmat, vis):
    w, carry = _sb_weights(_sb_scores(q, kw), carry, tmat, vis)
    return jnp.dot(w, vw, preferred_element_type=F32), carry


def _attn_prompt_kernel(q_ref, k_ref, v_ref, zg_ref, tmat_ref, o_ref,
                        kb_scr, vb_scr, acc_scr, carry_scr, z_scr, w_scr, *, tq):
    i = pl.program_id(1)

    def rows(j):
        return pl.ds(pl.multiple_of(jnp.maximum(j, 0) * tq, tq), tq)

    @pl.when(i == 0)
    def _():
        kb_scr[...] = k_ref[...].astype(BF16)
        vb_scr[...] = v_ref[...].astype(BF16)

    q = (q_ref[...] * C_QSCALE).astype(BF16)
    tmat = tmat_ref[...]

    def stage(j, cur):
        nxt = 1 - cur
        z_scr[nxt] = _sb_scores(q, kb_scr[rows(j - 1), :])
        w_scr[nxt], carry_scr[...] = _sb_weights(z_scr[cur], carry_scr[...], tmat, None)
        acc_scr[...] += jnp.dot(w_scr[cur], vb_scr[rows(j + 1), :], preferred_element_type=F32)

    def finish(cur):
        acc = acc_scr[...] + jnp.dot(w_scr[cur], vb_scr[rows(0), :], preferred_element_type=F32)
        o_ref[...] = (acc * _silu(zg_ref[...])).astype(o_ref.dtype)

    qpos = lax.broadcasted_iota(jnp.int32, (tq, tq), 0)
    kpos = lax.broadcasted_iota(jnp.int32, (tq, tq), 1)
    w_scr[0], carry_scr[...] = _sb_weights(
        _sb_scores(q, kb_scr[rows(i), :]), jnp.zeros((tq, C_TK), F32), tmat, kpos < qpos)
    z_scr[0] = _sb_scores(q, kb_scr[rows(i - 1), :])
    acc_scr[...] = jnp.zeros_like(acc_scr)

    def body(it, c):
        j = i - 1 - 2 * it
        stage(j, 0)
        stage(j - 1, 1)
        return c

    lax.fori_loop(0, i // 2, body, 0)

    @pl.when(i % 2 == 1)
    def _():
        stage(0, 0)
        finish(1)

    @pl.when(i % 2 == 0)
    def _():
        finish(0)


def attn_prompt(q, k, v, zg, *, tq):
    nh, seq, d = q.shape
    hblk = pl.BlockSpec((None, tq, d), lambda h, i: (h, i, 0))
    head = pl.BlockSpec((None, seq, d), lambda h, i: (h, 0, 0))
    tmat = _suffix_sum_matrix()
    return pl.pallas_call(
        functools.partial(_attn_prompt_kernel, tq=tq),
        grid=(nh, seq // tq),
        in_specs=[hblk, head, head, pl.BlockSpec((tq, d), lambda h, i: (i, h)),
                  pl.BlockSpec(tmat.shape, lambda h, i: (0, 0))],
        out_specs=pl.BlockSpec((tq, d), lambda h, i: (i, h)),
        out_shape=jax.ShapeDtypeStruct((seq, nh * d), BF16),
        scratch_shapes=[
            pltpu.VMEM((seq, d), BF16), pltpu.VMEM((seq, d), BF16),
            pltpu.VMEM((tq, d), F32), pltpu.VMEM((tq, C_TK), F32),
            pltpu.VMEM((2, tq, tq), F32), pltpu.VMEM((2, tq, tq), BF16),
        ],
        compiler_params=_cparams(("arbitrary", "arbitrary"), 56),
        name="attn_prompt",
    )(q, k, v, zg, tmat)


def _attn_sample_kernel(q_ref, kn_ref, vn_ref, zg_ref, kc_ref, vc_ref, tmat_ref, o_ref, ko_ref, vo_ref, *, hb, past):
    t = q_ref.shape[0]
    d = C_HEAD_DIM
    tmat = tmat_ref[...]
    qpos = lax.broadcasted_iota(jnp.int32, (t, C_TK), 0)
    kpos = lax.broadcasted_iota(jnp.int32, (t, C_TK), 1)
    vis_new = kpos < qpos
    pad = jnp.zeros((C_TK - t, d), BF16)
    for h in range(hb):
        sl = slice(h * d, (h + 1) * d)
        kn = kn_ref[:, sl]
        vn = vn_ref[:, sl]
        ko_ref[h] = kn
        vo_ref[h] = vn
        q = (q_ref[:, sl] * C_QSCALE).astype(BF16)
        carry = jnp.zeros((t, C_TK), F32)
        acc, carry = _sb_wide(q, jnp.concatenate([kn.astype(BF16), pad], axis=0),
                              jnp.concatenate([vn.astype(BF16), pad], axis=0), carry, tmat, vis_new)
        pv, carry = _sb_wide(q, kc_ref[h].astype(BF16), vc_ref[h].astype(BF16), carry, tmat, None)
        acc = acc + pv
        o_ref[:, sl] = (acc * _silu(zg_ref[:, sl])).astype(o_ref.dtype)


def attn_sample(p, cache_k, cache_v, *, hb):
    bsz, nh, past, d = cache_k.shape
    t = p.shape[0] // bsz
    w = nh * d
    ng = nh // hb
    blk = lambda c: pl.BlockSpec((t, hb * d), lambda b, g, c=c: (b, c * ng + g))
    cspec = pl.BlockSpec((None, hb, past, d), lambda b, g: (b, g, 0, 0))
    nspec = pl.BlockSpec((None, hb, t, d), lambda b, g: (b, g, 0, 0))
    tmat = _suffix_sum_matrix()
    return pl.pallas_call(
        functools.partial(_attn_sample_kernel, hb=hb, past=past),
        grid=(bsz, ng),
        in_specs=[blk(0), blk(1), blk(2), blk(3), cspec, cspec, pl.BlockSpec(tmat.shape, lambda b, g: (0, 0))],
        out_specs=[pl.BlockSpec((t, hb * d), lambda b, g: (b, g)), nspec, nspec],
        out_shape=[
            jax.ShapeDtypeStruct((bsz * t, w), BF16),
            jax.ShapeDtypeStruct((bsz, nh, t, d), F32),
            jax.ShapeDtypeStruct((bsz, nh, t, d), F32),
        ],
        compiler_params=_cparams(("parallel", "parallel"), 40),
        name="attn_sample",
    )(p, p, p, p, cache_k, cache_v, tmat)


SSD_L = 128
SSD_HG = 16
SSD_PAIRS = SSD_HG // 2
SSD_W = SSD_HG * B_HEAD_DIM
SSD_GB = 2 * B_STATE
HALO = 8


def _split3(x):
    hi = x.astype(BF16)
    r = x - hi.astype(F32)
    mid = r.astype(BF16)
    lo = (r - mid.astype(F32)).astype(BF16)
    return hi, mid, lo


def _dot_x01(x, m):
    hi, mid, lo = _split3(x)
    d = lambda a: jnp.dot(a, m, preferred_element_type=F32)
    return (d(hi) + d(mid)) + d(lo)


def _dot_01x(m, x):
    hi, mid, lo = _split3(x)
    d = lambda a: jnp.dot(m, a, preferred_element_type=F32)
    return (d(hi) + d(mid)) + d(lo)


def _softplus(x):
    return jnp.maximum(x, 0.0) + jnp.log1p(jnp.exp(-jnp.abs(x)))


def _ssd_kernel(*refs, nseq, sample):
    (z_ref, x_ref, b_ref, c_ref, dt_ref, cwx_ref, cwb_ref, cwc_ref, cbx_ref, cbb_ref, cbc_ref,
     dtb_ref, alog_ref, dl_ref, nw_ref, tri_ref, same_ref, e64_ref, el_ref) = refs[:19]
    refs = refs[19:]
    if sample:
        hx_ref, hb_ref, hc_ref, s0_ref = refs[:4]
        refs = refs[4:]
    (y_ref, so_ref, extx, extb, extc, st_scr, xs_scr, bs_scr, cs_scr, ecx_scr, wex_scr, dcx_scr,
     cc_scr, cumt_scr, dtt_scr, yacc_scr) = refs
    L = SSD_L
    lseq = L // nseq
    c = pl.program_id(1)
    nchunks = pl.num_programs(1)

    if sample:
        for s in range(nseq):
            extx[s, HALO - 3:HALO, :] = hx_ref[s]
            extb[s, HALO - 3:HALO, :] = hb_ref[s]
            extc[s, HALO - 3:HALO, :] = hc_ref[s]
            for pp in range(SSD_PAIRS):
                st_scr[s, pp] = s0_ref[s, pp].T
    else:
        @pl.when(c == 0)
        def _():
            extx[0, 0:HALO, :] = jnp.zeros((HALO, SSD_W), F32)
            extb[0, 0:HALO, :] = jnp.zeros((HALO, SSD_GB), F32)
            extc[0, 0:HALO, :] = jnp.zeros((HALO, SSD_GB), F32)
            st_scr[...] = jnp.zeros_like(st_scr)

    def conv(ext, src_ref, cw_ref, cb_ref, dst):
        for s in range(nseq):
            ext[s, HALO:HALO + lseq, :] = src_ref[s * lseq:(s + 1) * lseq, :]
        for s in range(nseq):
            acc = cb_ref[...]
            for k in range(B_CONV):
                acc = acc + ext[s, HALO - 3 + k:HALO - 3 + k + lseq, :] * cw_ref[k:k + 1, :]
            dst[s * lseq:(s + 1) * lseq, :] = _silu(acc).astype(dst.dtype)
        if not sample:
            ext[0, 0:HALO, :] = ext[0, lseq:lseq + HALO, :]

    conv(extx, x_ref, cwx_ref, cbx_ref, xs_scr)
    conv(extb, b_ref, cwb_ref, cbb_ref, bs_scr)
    conv(extc, c_ref, cwc_ref, cbc_ref, cs_scr)
    yacc_scr[...] = _silu(z_ref[...])

    dt = _softplus(dt_ref[...] + dtb_ref[...])
    dta = dt * (-jnp.exp(alog_ref[...]))
    cum = _dot_01x(tri_ref[...], dta)
    if nseq == 1:
        ctot = jnp.broadcast_to(cum[L - 1:L, :], (L, LANES))
    else:
        ctot = _dot_01x(same_ref[...], dta)
    cumt_scr[...] = cum.T
    dtt_scr[...] = dt.T
    e64 = e64_ref[...]
    expand = lambda a: jnp.dot(a.astype(BF16), e64, preferred_element_type=F32)
    ecx_scr[...] = expand(jnp.exp(cum))
    wex_scr[...] = expand(jnp.exp(ctot - cum) * dt)
    drows = HALO if nseq == 1 else L
    dcx_scr[0:drows, :] = _dot_x01(jnp.exp(ctot[0:drows]), e64)
    cc_scr[...] = _dot_x01(cum, el_ref[...])
    mask = tri_ref[...] > 0
    lane = lax.broadcasted_iota(jnp.int32, (L, LANES), 1)
    first = lane < B_HEAD_DIM

    for g2 in range(2):
        bg = bs_scr[:, g2 * B_STATE:(g2 + 1) * B_STATE]
        cg = cs_scr[:, g2 * B_STATE:(g2 + 1) * B_STATE]
        cb = lax.dot_general(cg, bg, (((1,), (1,)), ((), ())), preferred_element_type=F32)
        for p in range(SSD_PAIRS // 2):
            pp = g2 * (SSD_PAIRS // 2) + p
            lanes = slice(pp * LANES, (pp + 1) * LANES)
            ms = []
            for r in (2 * pp, 2 * pp + 1):
                seg = cc_scr[:, r * L:(r + 1) * L] - cumt_scr[r:r + 1, :]
                ms.append((cb * jnp.exp(jnp.where(mask, seg, -jnp.inf)) * dtt_scr[r:r + 1, :]).astype(BF16))
            xp = xs_scr[:, lanes]
            xa = jnp.where(first, xp, 0.0).astype(BF16)
            xb = jnp.where(first, 0.0, xp).astype(BF16)
            y = jnp.dot(jnp.concatenate(ms, axis=1), jnp.concatenate([xa, xb], axis=0), preferred_element_type=F32)
            xw = (xp * wex_scr[:, lanes]).astype(BF16)
            ys = []
            for s in range(nseq):
                rows = slice(s * lseq, (s + 1) * lseq)
                st = st_scr[s, pp]
                ys.append(jnp.dot(cg[rows], st.astype(BF16), preferred_element_type=F32))
                upd = lax.dot_general(bg[rows], xw[rows], (((0,), (0,)), ((), ())), preferred_element_type=F32)
                st_scr[s, pp] = st * dcx_scr[s * lseq:s * lseq + 1, lanes] + upd
            ystate = ys[0] if nseq == 1 else jnp.concatenate(ys, axis=0)
            y = y + ystate * ecx_scr[:, lanes] + dl_ref[:, lanes] * xp
            yacc_scr[:, lanes] = y * yacc_scr[:, lanes]

    gw = SSD_W // 2
    for g2 in range(2):
        sl = slice(g2 * gw, (g2 + 1) * gw)
        yg = yacc_scr[:, sl]
        ms_ = jnp.mean(yg * yg, axis=-1, keepdims=True)
        y_ref[:, sl] = (yg * lax.rsqrt(ms_ + NORM_EPS) * nw_ref[:, sl]).astype(y_ref.dtype)

    if sample:
        for s in range(nseq):
            for pp in range(SSD_PAIRS):
                so_ref[s, pp] = st_scr[s, pp].T
    else:
        @pl.when(c == nchunks - 1)
        def _():
            for pp in range(SSD_PAIRS):
                so_ref[0, pp] = st_scr[0, pp].T


def _ssd_consts(lseq):
    L = SSD_L
    t = jnp.arange(L)
    same = (t[:, None] // lseq) == (t[None, :] // lseq)
    tri = same & (t[None, :] <= t[:, None])
    r = jnp.arange(LANES)
    e64 = (r[:, None] == (jnp.arange(SSD_W)[None, :] // B_HEAD_DIM)) & (r[:, None] < SSD_HG)
    el = (r[:, None] == (jnp.arange(SSD_HG * L)[None, :] // L)) & (r[:, None] < SSD_HG)
    return tri.astype(BF16), same.astype(BF16), e64.astype(BF16), el.astype(BF16)


def ssd_core(pz, dtp, conv_w, conv_b, dtb, alog, dlane, nw, *, lseq, conv_state=None, ssm_state=None):
    m = pz.shape[0]
    sample = conv_state is not None
    nseq = SSD_L // lseq
    nchunks = m // SSD_L
    nslices = B_HEADS // SSD_HG
    tri, same, e64, el = _ssd_consts(lseq)
    nx = 4096 // SSD_W
    rowblk = lambda w, off: pl.BlockSpec((SSD_L, w), lambda j, c, off=off: (c, off + j))
    parblk = lambda r, w, off: pl.BlockSpec((r, w), lambda j, c, off=off: (0, off + j))
    const = lambda a: pl.BlockSpec(a.shape, lambda j, c: (0, 0))
    ob = 8192 // SSD_GB
    oc = 9216 // SSD_GB
    in_specs = [
        rowblk(SSD_W, 0), rowblk(SSD_W, nx), rowblk(SSD_GB, ob), rowblk(SSD_GB, oc), rowblk(LANES, 0),
        parblk(B_CONV, SSD_W, 0), parblk(B_CONV, SSD_GB, 4096 // SSD_GB), parblk(B_CONV, SSD_GB, 5120 // SSD_GB),
        parblk(1, SSD_W, 0), parblk(1, SSD_GB, 4096 // SSD_GB), parblk(1, SSD_GB, 5120 // SSD_GB),
        parblk(1, LANES, 0), parblk(1, LANES, 0), parblk(1, SSD_W, 0), parblk(1, SSD_W, 0),
        const(tri), const(same), const(e64), const(el),
    ]
    args = [pz, pz, pz, pz, dtp, conv_w, conv_w, conv_w, conv_b, conv_b, conv_b, dtb, alog, dlane, nw, tri, same, e64, el]
    if sample:
        hblk = lambda w, off: pl.BlockSpec((nseq, B_CONV - 1, w), lambda j, c, off=off: (c, 0, off + j))
        in_specs += [hblk(SSD_W, 0), hblk(SSD_GB, 4096 // SSD_GB), hblk(SSD_GB, 5120 // SSD_GB),
                     pl.BlockSpec((nseq, SSD_PAIRS, LANES, B_STATE), lambda j, c: (c, j, 0, 0))]
        args += [conv_state, conv_state, conv_state, ssm_state]
        nstates = m // lseq
        so_spec = pl.BlockSpec((nseq, SSD_PAIRS, LANES, B_STATE), lambda j, c: (c, j, 0, 0))
    else:
        nstates = 1
        so_spec = pl.BlockSpec((1, SSD_PAIRS, LANES, B_STATE), lambda j, c: (0, j, 0, 0))
    L = SSD_L
    scratch = [
        pltpu.VMEM((nseq, HALO + lseq, SSD_W), F32), pltpu.VMEM((nseq, HALO + lseq, SSD_GB), F32),
        pltpu.VMEM((nseq, HALO + lseq, SSD_GB), F32),
        pltpu.VMEM((nseq, SSD_PAIRS, B_STATE, LANES), F32),
        pltpu.VMEM((L, SSD_W), F32), pltpu.VMEM((L, SSD_GB), BF16), pltpu.VMEM((L, SSD_GB), BF16),
        pltpu.VMEM((L, SSD_W), F32), pltpu.VMEM((L, SSD_W), F32), pltpu.VMEM((L, SSD_W), F32),
        pltpu.VMEM((L, SSD_HG * L), F32), pltpu.VMEM((LANES, L), F32), pltpu.VMEM((LANES, L), F32),
        pltpu.VMEM((L, SSD_W), F32),
    ]
    return pl.pallas_call(
        functools.partial(_ssd_kernel, nseq=nseq, sample=sample),
        grid=(nslices, nchunks),
        in_specs=in_specs,
        out_specs=[pl.BlockSpec((SSD_L, SSD_W), lambda j, c: (c, j)), so_spec],
        out_shape=[jax.ShapeDtypeStruct((m, B_HEADS * B_HEAD_DIM), BF16),
                   jax.ShapeDtypeStruct((nstates, B_HEADS // 2, LANES, B_STATE), F32)],
        scratch_shapes=scratch,
        compiler_params=_cparams(("arbitrary", "arbitrary"), 48),
        name="ssd_core",
    )(*args)


def _group_heads(v, fill=0.0):
    lead = v.shape[:-1]
    g = v.reshape(*lead, B_HEADS // SSD_HG, SSD_HG)
    g = jnp.pad(g, [(0, 0)] * (len(lead) + 1) + [(0, LANES - SSD_HG)], constant_values=fill)
    return g.reshape(*lead, (B_HEADS // SSD_HG) * LANES)


N_MIXERS = 3
PROJ_TM = 1024
PROJ_TN = 1024
OUT_TM = 512


def _run_stream(x, lseq, tm, par, state_conv=None, state_ssm=None, cache_k=None, cache_v=None):
    sample = state_conv is not None
    depth = par["norm_w"].shape[0]
    new = {"v": [], "ssm": [], "conv": [], "k": [], "v_attn": []}
    h = rms_norm_bf16(x, par["norm_w"][0], tm=tm)
    for i in range(depth):
        kind, j = i % N_MIXERS, i // N_MIXERS
        if kind == 0:
            p = matmul(h, par["a_w_in"], j, tm=tm, tn=PROJ_TN, name="a_in_proj")
            wpos, bias = _gmlp_pos_params(par["a_w_s"][j], par["a_b_s"][j], min(lseq, A_CHUNK))
            r = gmlp_core(p, wpos, bias, par["a_ln_g"][j], par["a_ln_b"][j], emit_v=sample)
            if sample:
                y, v = r
                new["v"].append(v)
            else:
                y = r
            w_out = par["a_w_out"]
        elif kind == 1:
            pz = matmul(h, par["b_w_in"], j, tm=tm, tn=PROJ_TN, n=par["b_zx"], name="b_in_proj")
            dtp = matmul(h, par["b_w_dt"], j, tm=tm, tn=par["b_w_dt"].shape[2], name="b_dt_proj")
            y, s = ssd_core(
                pz, dtp, par["b_conv_w"][j], par["b_conv_b"][j], par["b_dt_bias"][j], par["b_a_log"][j],
                par["b_d_lane"][j], par["b_norm_w"][j], lseq=min(lseq, SSD_L),
                conv_state=state_conv[j] if sample else None,
                ssm_state=state_ssm[j].reshape(-1, B_HEADS // 2, LANES, B_STATE) if sample else None)
            new["ssm"].append(s.reshape(-1, B_HEADS, B_HEAD_DIM, B_STATE))
            new["conv"].append(pz.reshape(-1, lseq, pz.shape[1])[:, lseq - (B_CONV - 1):, 4096:])
            w_out = par["b_w_out"]
        else:
            if sample:
                p = matmul(h, par["c_w_in"], j, tm=tm, tn=PROJ_TN, name="c_in_proj")
                y, k, v = attn_sample(p, cache_k[j], cache_v[j], hb=8)
            else:
                cw = par["c_w_in"].shape[2] // 4
                part = lambda c, hm: matmul(h, par["c_w_in"], j, tm=tm, tn=PROJ_TN, n=cw, col0=c * cw,
                                            head_major=hm, name="c_in_proj")
                q, k, v, zg = part(0, True), part(1, True), part(2, True), part(3, False)
                y = attn_prompt(q, k, v, zg, tq=min(C_TQ, x.shape[0]))
                k, v = k[None], v[None]
            new["k"].append(k)
            new["v_attn"].append(v)
            w_out = par["c_w_out"]
        last = i == depth - 1
        nw = par["final_norm_w"] if last else par["norm_w"][i + 1]
        r = matmul_residual_norm(y, w_out, j, x, nw, tm=OUT_TM, final=last)
        if last:
            return r, new
        x, h = r


def kernel(x_prompt, x_sample, state_ssm, state_conv, cache_k, cache_v, norm_w, final_norm_w, a_w_in, a_ln_g, a_ln_b, a_w_s, a_b_s, a_w_out, b_w_in, b_conv_w, b_conv_b, b_dt_bias, b_a_log, b_d, b_norm_w, b_w_out, c_w_in, c_w_out):
    bp, seq, d = x_prompt.shape
    bs, dseq, _ = x_sample.shape
    nb = b_w_in.shape[0]
    zx = b_w_in.shape[2] - B_HEADS
    par = {
        "norm_w": norm_w, "final_norm_w": final_norm_w,
        "a_w_in": a_w_in, "a_ln_g": a_ln_g, "a_ln_b": a_ln_b, "a_w_s": a_w_s, "a_b_s": a_b_s,
        "a_w_out": a_w_out.astype(BF16),
        "b_w_in": b_w_in, "b_zx": zx, "b_w_dt": _group_heads(b_w_in[:, :, zx:]),
        "b_conv_w": b_conv_w, "b_conv_b": b_conv_b.reshape(nb, 1, -1),
        "b_dt_bias": _group_heads(b_dt_bias).reshape(nb, 1, -1), "b_a_log": _group_heads(b_a_log).reshape(nb, 1, -1),
        "b_d_lane": jnp.repeat(b_d, B_HEAD_DIM, axis=-1).reshape(nb, 1, -1), "b_norm_w": b_norm_w.reshape(nb, 1, -1),
        "b_w_out": b_w_out.astype(BF16), "c_w_in": c_w_in, "c_w_out": c_w_out.astype(BF16),
    }
    assert bp == 1, "the prompt group is one stream"
    yp, newp = _run_stream(x_prompt.reshape(seq, d), seq, min(PROJ_TM, seq), par)
    ys, news = _run_stream(x_sample.reshape(bs * dseq, d), dseq, min(PROJ_TM, bs * dseq), par,
                           state_conv=state_conv, state_ssm=state_ssm, cache_k=cache_k, cache_v=cache_v)
    st = jnp.stack
    return (
        yp.reshape(bp, seq, d),
        ys.reshape(bs, dseq, d),
        st([v.reshape(bs, dseq, -1) for v in news["v"]]),
        st(newp["ssm"]), st(newp["conv"]), st(news["ssm"]), st(news["conv"]),
        st(newp["k"]), st(newp["v_attn"]), st(news["k"]), st(news["v_attn"]),
    )
```

```python
import functools

import jax
import jax.numpy as jnp
from jax import lax
from jax.experimental import pallas as pl
from jax.experimental.pallas import tpu as pltpu

F32 = jnp.float32
BF16 = jnp.bfloat16

NORM_EPS = 1e-6
D_MODEL = 2048
CHUNK = 64
A_GROUPS = 16
A_CHUNK = 128
B_HEADS = 64
B_HEAD_DIM = 64
B_GROUPS = 8
B_STATE = 128
B_CONV = 4
C_HEADS = 16
C_HEAD_DIM = 128

LANES = 128
MIB = 1024 * 1024


def _cparams(sem, vmem_mib):
    return pltpu.CompilerParams(dimension_semantics=sem, vmem_limit_bytes=vmem_mib * MIB)


def _rms_kernel(x_ref, w_ref, h_ref):
    x = x_ref[...]
    ms = jnp.mean(x * x, axis=-1, keepdims=True)
    h_ref[...] = (x * lax.rsqrt(ms + NORM_EPS) * w_ref[...]).astype(h_ref.dtype)


def rms_norm_bf16(x, w, *, tm):
    m, d = x.shape
    return pl.pallas_call(
        _rms_kernel,
        grid=(m // tm,),
        in_specs=[pl.BlockSpec((tm, d), lambda i: (i, 0)), pl.BlockSpec((1, d), lambda i: (0, 0))],
        out_specs=pl.BlockSpec((tm, d), lambda i: (i, 0)),
        out_shape=jax.ShapeDtypeStruct((m, d), BF16),
        compiler_params=_cparams(("parallel",), 32),
        name="rms_norm",
    )(x, w.reshape(1, d))


def _mm_kernel(x_ref, w_ref, o_ref, wb_scr, *, head_major):
    @pl.when(pl.program_id(1) == 0)
    def _():
        wb_scr[...] = w_ref[...].astype(BF16)

    acc = jnp.dot(x_ref[...], wb_scr[...], preferred_element_type=F32)
    if head_major:
        for h in range(o_ref.shape[0]):
            o_ref[h] = acc[:, h * LANES:(h + 1) * LANES].astype(o_ref.dtype)
    else:
        o_ref[...] = acc.astype(o_ref.dtype)


def matmul(x, w, layer, *, tm, tn, n=None, col0=0, head_major=False, out_dtype=F32, name="proj"):
    m, k = x.shape
    n = w.shape[2] if n is None else n
    c0 = col0 // tn
    if head_major:
        hb = tn // LANES
        out_specs = pl.BlockSpec((hb, tm, LANES), lambda j, i: (j, i, 0))
        out_shape = jax.ShapeDtypeStruct((n // LANES, m, LANES), out_dtype)
    else:
        out_specs = pl.BlockSpec((tm, tn), lambda j, i: (i, j))
        out_shape = jax.ShapeDtypeStruct((m, n), out_dtype)
    return pl.pallas_call(
        functools.partial(_mm_kernel, head_major=head_major),
        grid=(n // tn, m // tm),
        in_specs=[pl.BlockSpec((tm, k), lambda j, i: (i, 0)),
                  pl.BlockSpec((None, k, tn), lambda j, i: (layer, 0, c0 + j))],
        out_specs=out_specs,
        out_shape=out_shape,
        scratch_shapes=[pltpu.VMEM((k, tn), BF16)],
        compiler_params=_cparams(("arbitrary", "arbitrary"), 48),
        name=name,
    )(x, w)


def _mm_res_norm_kernel(y_ref, w_ref, x_ref, nw_ref, *out_refs, final):
    xn = x_ref[...] + jnp.dot(y_ref[...], w_ref[...], preferred_element_type=F32)
    ms = jnp.mean(xn * xn, axis=-1, keepdims=True)
    h = xn * lax.rsqrt(ms + NORM_EPS) * nw_ref[...]
    if final:
        out_refs[0][...] = h
    else:
        out_refs[0][...] = xn
        out_refs[1][...] = h.astype(BF16)


def matmul_residual_norm(y, w, layer, x, nw, *, tm, final=False, name="out_proj"):
    m, k = y.shape
    d = w.shape[2]
    row = lambda i: (i, 0)
    if final:
        out_shape = jax.ShapeDtypeStruct((m, d), F32)
        out_specs = pl.BlockSpec((tm, d), row)
    else:
        out_shape = (jax.ShapeDtypeStruct((m, d), F32), jax.ShapeDtypeStruct((m, d), BF16))
        out_specs = (pl.BlockSpec((tm, d), row), pl.BlockSpec((tm, d), row))
    return pl.pallas_call(
        functools.partial(_mm_res_norm_kernel, final=final),
        grid=(m // tm,),
        in_specs=[
            pl.BlockSpec((tm, k), row),
            pl.BlockSpec((None, k, d), lambda i: (layer, 0, 0), pipeline_mode=pl.Buffered(1)),
            pl.BlockSpec((tm, d), row),
            pl.BlockSpec((1, d), lambda i: (0, 0)),
        ],
        out_specs=out_specs,
        out_shape=out_shape,
        compiler_params=_cparams(("parallel",), 56),
        name=name,
    )(y, w, x, nw.reshape(1, d))


def _gelu2(x):
    return x * (1.0 + lax.erf(x * (2.0 ** -0.5)))


def _silu(x):
    return x * (1.0 / (1.0 + jnp.exp(-x)))


A_GW = 256


def _gmlp_kernel(u_ref, v_ref, z_ref, wpos_ref, bias_ref, g_ref, b_ref, *refs, emit_v):
    if emit_v:
        y_ref, vout_ref, gv_scr, vn_scr = refs
    else:
        y_ref, gv_scr, vn_scr = refs
    rows, width = gv_scr.shape
    ngroups = width // A_GW
    acc = jnp.zeros((rows, LANES), F32)
    for g in range(ngroups):
        sl = slice(g * A_GW, (g + 1) * A_GW)
        gv = _gelu2(v_ref[:, sl].astype(F32))
        gv_scr[:, sl] = gv
        acc = acc + gv[:, :LANES] + gv[:, LANES:]
    mean = jnp.sum(acc, axis=-1, keepdims=True) * (1.0 / width)
    acc = jnp.zeros((rows, LANES), F32)
    for g in range(ngroups):
        sl = slice(g * A_GW, (g + 1) * A_GW)
        vc = gv_scr[:, sl] - mean
        sq = vc * vc
        acc = acc + sq[:, :LANES] + sq[:, LANES:]
    var = jnp.sum(acc, axis=-1, keepdims=True) * (1.0 / width)
    rstd = lax.rsqrt(var + 4.0 * NORM_EPS)
    for g in range(ngroups):
        sl = slice(g * A_GW, (g + 1) * A_GW)
        vn = (gv_scr[:, sl] - mean) * rstd * g_ref[:, sl] + b_ref[:, sl]
        if emit_v:
            vout_ref[:, sl] = vn
        vn_scr[:, sl] = vn.astype(BF16)
    for g in range(ngroups):
        sl = slice(g * A_GW, (g + 1) * A_GW)
        s = jnp.dot(wpos_ref[g], vn_scr[:, sl], preferred_element_type=F32) + bias_ref[:, sl]
        y_ref[:, sl] = (_gelu2(u_ref[:, sl].astype(F32)) * s * _silu(z_ref[:, sl].astype(F32))).astype(BF16)


def gmlp_core(p, wpos, bias, ln_g, ln_b, *, emit_v):
    m, w3 = p.shape
    w = w3 // 3
    t = A_CHUNK
    col = lambda c: pl.BlockSpec((t, w), lambda i, c=c: (i, c))
    const2 = lambda shape: pl.BlockSpec(shape, lambda i: (0, 0))
    out_shape = [jax.ShapeDtypeStruct((m, w), BF16)]
    out_specs = [pl.BlockSpec((t, w), lambda i: (i, 0))]
    if emit_v:
        out_shape.append(jax.ShapeDtypeStruct((m, w), F32))
        out_specs.append(pl.BlockSpec((t, w), lambda i: (i, 0)))
    res = pl.pallas_call(
        functools.partial(_gmlp_kernel, emit_v=emit_v),
        grid=(m // t,),
        in_specs=[
            col(0), col(1), col(2),
            pl.BlockSpec(wpos.shape, lambda i: (0, 0, 0)),
            const2((t, w)), const2((1, w)), const2((1, w)),
        ],
        out_specs=out_specs,
        out_shape=out_shape,
        scratch_shapes=[pltpu.VMEM((t, w), F32), pltpu.VMEM((t, w), BF16)],
        compiler_params=_cparams(("parallel",), 40),
        name="gmlp_core",
    )(p, p, p, wpos, bias, ln_g.reshape(1, w), ln_b.reshape(1, w))
    return res if emit_v else res[0]


def _gmlp_pos_params(w_s, b_s, lc):
    pos = jnp.arange(lc)
    mask = (pos[None, :] // CHUNK) <= (pos[:, None] // CHUNK)
    wp = jnp.where(mask[None], w_s[:, :lc, :lc], 0.0)
    reps = A_CHUNK // lc
    if reps > 1:
        eye = jnp.eye(reps, dtype=wp.dtype)
        wp = jnp.einsum("ab,gts->gatbs", eye, wp).reshape(w_s.shape[0], A_CHUNK, A_CHUNK)
    bias_t = jnp.tile(b_s[:, :lc].T, (reps, 1))
    bias = jnp.repeat(bias_t, A_GW, axis=1)
    return (0.5 * wp).astype(BF16), (0.5 * bias).astype(F32)


C_TK = 128
C_SW = 256
C_TQ = 512


def _suffix_sum_matrix():
    j = jnp.arange(C_SW)
    return (j[:, None] >= j[None, :]).astype(BF16)


LOG2E = 1.4426950408889634
C_QSCALE = (C_HEAD_DIM ** -0.5) * LOG2E


def _sb_scores(q, kw):
    return lax.dot_general(q, kw, (((1,), (1,)), ((), ())), preferred_element_type=F32)


def _sb_weights(z, carry, tmat, vis):
    t, n = z.shape
    sp = jnp.maximum(z, 0.0) + jnp.log2(1.0 + jnp.exp2(-jnp.abs(z)))
    if vis is not None:
        sp = jnp.where(vis, sp, 0.0)
    sp = sp.astype(BF16)
    bounds = list(range(0, n, C_SW)) + [n]
    groups = list(zip(bounds[:-1], bounds[1:]))
    sums = {}
    for lo, hi in reversed(groups):
        sums[lo] = jnp.dot(sp[:, lo:hi], tmat[:hi - lo, :hi - lo], preferred_element_type=F32)
    ws = []
    for lo, hi in reversed(groups):
        s = sums[lo]
        for c0 in range(hi - lo - C_TK, -1, -C_TK):
            ws.append(jnp.exp2(z[:, lo + c0:lo + c0 + C_TK] - s[:, c0:c0 + C_TK] - carry))
        carry = carry + jnp.broadcast_to(s[:, 0:1], (t, C_TK))
    w = ws[0] if len(ws) == 1 else jnp.concatenate(ws[::-1], axis=1)
    if vis is not None:
        w = jnp.where(vis, w, 0.0)
    return w.astype(BF16), carry


def _sb_wide(q, kw, vw, carry, tmat, vis):
    w, carry = _sb_weights(_sb_scores(q, kw), carry, tmat, vis)
    return jnp.dot(w, vw, preferred_element_type=F32), carry


def _attn_prompt_kernel(q_ref, k_ref, v_ref, zg_ref, tmat_ref, o_ref,
                        kb_scr, vb_scr, acc_scr, carry_scr, z_scr, w_scr, *, tq):
    i = pl.program_id(1)

    def rows(j):
        return pl.ds(pl.multiple_of(jnp.maximum(j, 0) * tq, tq), tq)

    @pl.when(i == 0)
    def _():
        kb_scr[...] = k_ref[...].astype(BF16)
        vb_scr[...] = v_ref[...].astype(BF16)

    q = (q_ref[...] * C_QSCALE).astype(BF16)
    tmat = tmat_ref[...]

    def stage(j, cur):
        nxt = 1 - cur
        z_scr[nxt] = _sb_scores(q, kb_scr[rows(j - 1), :])
        w_scr[nxt], carry_scr[...] = _sb_weights(z_scr[cur], carry_scr[...], tmat, None)
        acc_scr[...] += jnp.dot(w_scr[cur], vb_scr[rows(j + 1), :], preferred_element_type=F32)

    def finish(cur):
        acc = acc_scr[...] + jnp.dot(w_scr[cur], vb_scr[rows(0), :], preferred_element_type=F32)
        o_ref[...] = (acc * _silu(zg_ref[...])).astype(o_ref.dtype)

    qpos = lax.broadcasted_iota(jnp.int32, (tq, tq), 0)
    kpos = lax.broadcasted_iota(jnp.int32, (tq, tq), 1)
    w_scr[0], carry_scr[...] = _sb_weights(
        _sb_scores(q, kb_scr[rows(i), :]), jnp.zeros((tq, C_TK), F32), tmat, kpos < qpos)
    z_scr[0] = _sb_scores(q, kb_scr[rows(i - 1), :])
    acc_scr[...] = jnp.zeros_like(acc_scr)

    def body(it, c):
        j = i - 1 - 2 * it
        stage(j, 0)
        stage(j - 1, 1)
        return c

    lax.fori_loop(0, i // 2, body, 0)

    @pl.when(i % 2 == 1)
    def _():
        stage(0, 0)
        finish(1)

    @pl.when(i % 2 == 0)
    def _():
        finish(0)


def attn_prompt(q, k, v, zg, *, tq):
    nh, seq, d = q.shape
    hblk = pl.BlockSpec((None, tq, d), lambda h, i: (h, i, 0))
    head = pl.BlockSpec((None, seq, d), lambda h, i: (h, 0, 0))
    tmat = _suffix_sum_matrix()
    return pl.pallas_call(
        functools.partial(_attn_prompt_kernel, tq=tq),
        grid=(nh, seq // tq),
        in_specs=[hblk, head, head, pl.BlockSpec((tq, d), lambda h, i: (i, h)),
                  pl.BlockSpec(tmat.shape, lambda h, i: (0, 0))],
        out_specs=pl.BlockSpec((tq, d), lambda h, i: (i, h)),
        out_shape=jax.ShapeDtypeStruct((seq, nh * d), BF16),
        scratch_shapes=[
            pltpu.VMEM((seq, d), BF16), pltpu.VMEM((seq, d), BF16),
            pltpu.VMEM((tq, d), F32), pltpu.VMEM((tq, C_TK), F32),
            pltpu.VMEM((2, tq, tq), F32), pltpu.VMEM((2, tq, tq), BF16),
        ],
        compiler_params=_cparams(("arbitrary", "arbitrary"), 56),
        name="attn_prompt",
    )(q, k, v, zg, tmat)


def _attn_sample_kernel(q_ref, kn_ref, vn_ref, zg_ref, kc_ref, vc_ref, tmat_ref, o_ref, ko_ref, vo_ref, *, hb, past):
    t = q_ref.shape[0]
    d = C_HEAD_DIM
    tmat = tmat_ref[...]
    qpos = lax.broadcasted_iota(jnp.int32, (t, C_TK), 0)
    kpos = lax.broadcasted_iota(jnp.int32, (t, C_TK), 1)
    vis_new = kpos < qpos
    pad = jnp.zeros((C_TK - t, d), BF16)
    for h in range(hb):
        sl = slice(h * d, (h + 1) * d)
        kn = kn_ref[:, sl]
        vn = vn_ref[:, sl]
        ko_ref[h] = kn
        vo_ref[h] = vn
        q = (q_ref[:, sl] * C_QSCALE).astype(BF16)
        carry = jnp.zeros((t, C_TK), F32)
        acc, carry = _sb_wide(q, jnp.concatenate([kn.astype(BF16), pad], axis=0),
                              jnp.concatenate([vn.astype(BF16), pad], axis=0), carry, tmat, vis_new)
        pv, carry = _sb_wide(q, kc_ref[h].astype(BF16), vc_ref[h].astype(BF16), carry, tmat, None)
        acc = acc + pv
        o_ref[:, sl] = (acc * _silu(zg_ref[:, sl])).astype(o_ref.dtype)


def attn_sample(p, cache_k, cache_v, *, hb):
    bsz, nh, past, d = cache_k.shape
    t = p.shape[0] // bsz
    w = nh * d
    ng = nh // hb
    blk = lambda c: pl.BlockSpec((t, hb * d), lambda b, g, c=c: (b, c * ng + g))
    cspec = pl.BlockSpec((None, hb, past, d), lambda b, g: (b, g, 0, 0))
    nspec = pl.BlockSpec((None, hb, t, d), lambda b, g: (b, g, 0, 0))
    tmat = _suffix_sum_matrix()
    return pl.pallas_call(
        functools.partial(_attn_sample_kernel, hb=hb, past=past),
        grid=(bsz, ng),
        in_specs=[blk(0), blk(1), blk(2), blk(3), cspec, cspec, pl.BlockSpec(tmat.shape, lambda b, g: (0, 0))],
        out_specs=[pl.BlockSpec((t, hb * d), lambda b, g: (b, g)), nspec, nspec],
        out_shape=[
            jax.ShapeDtypeStruct((bsz * t, w), BF16),
            jax.ShapeDtypeStruct((bsz, nh, t, d), F32),
            jax.ShapeDtypeStruct((bsz, nh, t, d), F32),
        ],
        compiler_params=_cparams(("parallel", "parallel"), 40),
        name="attn_sample",
    )(p, p, p, p, cache_k, cache_v, tmat)


SSD_L = 128
SSD_HG = 16
SSD_PAIRS = SSD_HG // 2
SSD_W = SSD_HG * B_HEAD_DIM
SSD_GB = 2 * B_STATE
HALO = 8


def _split3(x):
    hi = x.astype(BF16)
    r = x - hi.astype(F32)
    mid = r.astype(BF16)
    lo = (r - mid.astype(F32)).astype(BF16)
    return hi, mid, lo


def _dot_x01(x, m):
    hi, mid, lo = _split3(x)
    d = lambda a: jnp.dot(a, m, preferred_element_type=F32)
    return (d(hi) + d(mid)) + d(lo)


def _dot_01x(m, x):
    hi, mid, lo = _split3(x)
    d = lambda a: jnp.dot(m, a, preferred_element_type=F32)
    return (d(hi) + d(mid)) + d(lo)


def _softplus(x):
    return jnp.maximum(x, 0.0) + jnp.log1p(jnp.exp(-jnp.abs(x)))


def _ssd_kernel(*refs, nseq, sample):
    (z_ref, x_ref, b_ref, c_ref, dt_ref, cwx_ref, cwb_ref, cwc_ref, cbx_ref, cbb_ref, cbc_ref,
     dtb_ref, alog_ref, dl_ref, nw_ref, tri_ref, same_ref, e64_ref, el_ref) = refs[:19]
    refs = refs[19:]
    if sample:
        hx_ref, hb_ref, hc_ref, s0_ref = refs[:4]
        refs = refs[4:]
    (y_ref, so_ref, extx, extb, extc, st_scr, xs_scr, bs_scr, cs_scr, ecx_scr, wex_scr, dcx_scr,
     cc_scr, cumt_scr, dtt_scr, yacc_scr) = refs
    L = SSD_L
    lseq = L // nseq
    c = pl.program_id(1)
    nchunks = pl.num_programs(1)

    if sample:
        for s in range(nseq):
            extx[s, HALO - 3:HALO, :] = hx_ref[s]
            extb[s, HALO - 3:HALO, :] = hb_ref[s]
            extc[s, HALO - 3:HALO, :] = hc_ref[s]
            for pp in range(SSD_PAIRS):
                st_scr[s, pp] = s0_ref[s, pp].T
    else:
        @pl.when(c == 0)
        def _():
            extx[0, 0:HALO, :] = jnp.zeros((HALO, SSD_W), F32)
            extb[0, 0:HALO, :] = jnp.zeros((HALO, SSD_GB), F32)
            extc[0, 0:HALO, :] = jnp.zeros((HALO, SSD_GB), F32)
            st_scr[...] = jnp.zeros_like(st_scr)

    def conv(ext, src_ref, cw_ref, cb_ref, dst):
        for s in range(nseq):
            ext[s, HALO:HALO + lseq, :] = src_ref[s * lseq:(s + 1) * lseq, :]
        for s in range(nseq):
            acc = cb_ref[...]
            for k in range(B_CONV):
                acc = acc + ext[s, HALO - 3 + k:HALO - 3 + k + lseq, :] * cw_ref[k:k + 1, :]
            dst[s * lseq:(s + 1) * lseq, :] = _silu(acc).astype(dst.dtype)
        if not sample:
            ext[0, 0:HALO, :] = ext[0, lseq:lseq + HALO, :]

    conv(extx, x_ref, cwx_ref, cbx_ref, xs_scr)
    conv(extb, b_ref, cwb_ref, cbb_ref, bs_scr)
    conv(extc, c_ref, cwc_ref, cbc_ref, cs_scr)
    yacc_scr[...] = _silu(z_ref[...])

    dt = _softplus(dt_ref[...] + dtb_ref[...])
    dta = dt * (-jnp.exp(alog_ref[...]))
    cum = _dot_01x(tri_ref[...], dta)
    if nseq == 1:
        ctot = jnp.broadcast_to(cum[L - 1:L, :], (L, LANES))
    else:
        ctot = _dot_01x(same_ref[...], dta)
    cumt_scr[...] = cum.T
    dtt_scr[...] = dt.T
    e64 = e64_ref[...]
    expand = lambda a: jnp.dot(a.astype(BF16), e64, preferred_element_type=F32)
    ecx_scr[...] = expand(jnp.exp(cum))
    wex_scr[...] = expand(jnp.exp(ctot - cum) * dt)
    drows = HALO if nseq == 1 else L
    dcx_scr[0:drows, :] = _dot_x01(jnp.exp(ctot[0:drows]), e64)
    cc_scr[...] = _dot_x01(cum, el_ref[...])
    mask = tri_ref[...] > 0
    lane = lax.broadcasted_iota(jnp.int32, (L, LANES), 1)
    first = lane < B_HEAD_DIM

    for g2 in range(2):
        bg = bs_scr[:, g2 * B_STATE:(g2 + 1) * B_STATE]
        cg = cs_scr[:, g2 * B_STATE:(g2 + 1) * B_STATE]
        cb = lax.dot_general(cg, bg, (((1,), (1,)), ((), ())), preferred_element_type=F32)
        for p in range(SSD_PAIRS // 2):
            pp = g2 * (SSD_PAIRS // 2) + p
            lanes = slice(pp * LANES, (pp + 1) * LANES)
            ms = []
            for r in (2 * pp, 2 * pp + 1):
                seg = cc_scr[:, r * L:(r + 1) * L] - cumt_scr[r:r + 1, :]
                ms.append((cb * jnp.exp(jnp.where(mask, seg, -jnp.inf)) * dtt_scr[r:r + 1, :]).astype(BF16))
            xp = xs_scr[:, lanes]
            xa = jnp.where(first, xp, 0.0).astype(BF16)
            xb = jnp.where(first, 0.0, xp).astype(BF16)
            y = jnp.dot(jnp.concatenate(ms, axis=1), jnp.concatenate([xa, xb], axis=0), preferred_element_type=F32)
            xw = (xp * wex_scr[:, lanes]).astype(BF16)
            ys = []
            for s in range(nseq):
                rows = slice(s * lseq, (s + 1) * lseq)
                st = st_scr[s, pp]
                ys.append(jnp.dot(cg[rows], st.astype(BF16), preferred_element_type=F32))
                upd = lax.dot_general(bg[rows], xw[rows], (((0,), (0,)), ((), ())), preferred_element_type=F32)
                st_scr[s, pp] = st * dcx_scr[s * lseq:s * lseq + 1, lanes] + upd
            ystate = ys[0] if nseq == 1 else jnp.concatenate(ys, axis=0)
            y = y + ystate * ecx_scr[:, lanes] + dl_ref[:, lanes] * xp
            yacc_scr[:, lanes] = y * yacc_scr[:, lanes]

    gw = SSD_W // 2
    for g2 in range(2):
        sl = slice(g2 * gw, (g2 + 1) * gw)
        yg = yacc_scr[:, sl]
        ms_ = jnp.mean(yg * yg, axis=-1, keepdims=True)
        y_ref[:, sl] = (yg * lax.rsqrt(ms_ + NORM_EPS) * nw_ref[:, sl]).astype(y_ref.dtype)

    if sample:
        for s in range(nseq):
            for pp in range(SSD_PAIRS):
                so_ref[s, pp] = st_scr[s, pp].T
    else:
        @pl.when(c == nchunks - 1)
        def _():
            for pp in range(SSD_PAIRS):
                so_ref[0, pp] = st_scr[0, pp].T


def _ssd_consts(lseq):
    L = SSD_L
    t = jnp.arange(L)
    same = (t[:, None] // lseq) == (t[None, :] // lseq)
    tri = same & (t[None, :] <= t[:, None])
    r = jnp.arange(LANES)
    e64 = (r[:, None] == (jnp.arange(SSD_W)[None, :] // B_HEAD_DIM)) & (r[:, None] < SSD_HG)
    el = (r[:, None] == (jnp.arange(SSD_HG * L)[None, :] // L)) & (r[:, None] < SSD_HG)
    return tri.astype(BF16), same.astype(BF16), e64.astype(BF16), el.astype(BF16)


def ssd_core(pz, dtp, conv_w, conv_b, dtb, alog, dlane, nw, *, lseq, conv_state=None, ssm_state=None):
    m = pz.shape[0]
    sample = conv_state is not None
    nseq = SSD_L // lseq
    nchunks = m // SSD_L
    nslices = B_HEADS // SSD_HG
    tri, same, e64, el = _ssd_consts(lseq)
    nx = 4096 // SSD_W
    rowblk = lambda w, off: pl.BlockSpec((SSD_L, w), lambda j, c, off=off: (c, off + j))
    parblk = lambda r, w, off: pl.BlockSpec((r, w), lambda j, c, off=off: (0, off + j))
    const = lambda a: pl.BlockSpec(a.shape, lambda j, c: (0, 0))
    ob = 8192 // SSD_GB
    oc = 9216 // SSD_GB
    in_specs = [
        rowblk(SSD_W, 0), rowblk(SSD_W, nx), rowblk(SSD_GB, ob), rowblk(SSD_GB, oc), rowblk(LANES, 0),
        parblk(B_CONV, SSD_W, 0), parblk(B_CONV, SSD_GB, 4096 // SSD_GB), parblk(B_CONV, SSD_GB, 5120 // SSD_GB),
        parblk(1, SSD_W, 0), parblk(1, SSD_GB, 4096 // SSD_GB), parblk(1, SSD_GB, 5120 // SSD_GB),
        parblk(1, LANES, 0), parblk(1, LANES, 0), parblk(1, SSD_W, 0), parblk(1, SSD_W, 0),
        const(tri), const(same), const(e64), const(el),
    ]
    args = [pz, pz, pz, pz, dtp, conv_w, conv_w, conv_w, conv_b, conv_b, conv_b, dtb, alog, dlane, nw, tri, same, e64, el]
    if sample:
        hblk = lambda w, off: pl.BlockSpec((nseq, B_CONV - 1, w), lambda j, c, off=off: (c, 0, off + j))
        in_specs += [hblk(SSD_W, 0), hblk(SSD_GB, 4096 // SSD_GB), hblk(SSD_GB, 5120 // SSD_GB),
                     pl.BlockSpec((nseq, SSD_PAIRS, LANES, B_STATE), lambda j, c: (c, j, 0, 0))]
        args += [conv_state, conv_state, conv_state, ssm_state]
        nstates = m // lseq
        so_spec = pl.BlockSpec((nseq, SSD_PAIRS, LANES, B_STATE), lambda j, c: (c, j, 0, 0))
    else:
        nstates = 1
        so_spec = pl.BlockSpec((1, SSD_PAIRS, LANES, B_STATE), lambda j, c: (0, j, 0, 0))
    L = SSD_L
    scratch = [
        pltpu.VMEM((nseq, HALO + lseq, SSD_W), F32), pltpu.VMEM((nseq, HALO + lseq, SSD_GB), F32),
        pltpu.VMEM((nseq, HALO + lseq, SSD_GB), F32),
        pltpu.VMEM((nseq, SSD_PAIRS, B_STATE, LANES), F32),
        pltpu.VMEM((L, SSD_W), F32), pltpu.VMEM((L, SSD_GB), BF16), pltpu.VMEM((L, SSD_GB), BF16),
        pltpu.VMEM((L, SSD_W), F32), pltpu.VMEM((L, SSD_W), F32), pltpu.VMEM((L, SSD_W), F32),
        pltpu.VMEM((L, SSD_HG * L), F32), pltpu.VMEM((LANES, L), F32), pltpu.VMEM((LANES, L), F32),
        pltpu.VMEM((L, SSD_W), F32),
    ]
    return pl.pallas_call(
        functools.partial(_ssd_kernel, nseq=nseq, sample=sample),
        grid=(nslices, nchunks),
        in_specs=in_specs,
        out_specs=[pl.BlockSpec((SSD_L, SSD_W), lambda j, c: (c, j)), so_spec],
        out_shape=[jax.ShapeDtypeStruct((m, B_HEADS * B_HEAD_DIM), BF16),
                   jax.ShapeDtypeStruct((nstates, B_HEADS // 2, LANES, B_STATE), F32)],
        scratch_shapes=scratch,
        compiler_params=_cparams(("arbitrary", "arbitrary"), 48),
        name="ssd_core",
    )(*args)


def _group_heads(v, fill=0.0):
    lead = v.shape[:-1]
    g = v.reshape(*lead, B_HEADS // SSD_HG, SSD_HG)
    g = jnp.pad(g, [(0, 0)] * (len(lead) + 1) + [(0, LANES - SSD_HG)], constant_values=fill)
    return g.reshape(*lead, (B_HEADS // SSD_HG) * LANES)


N_MIXERS = 3
PROJ_TM = 1024
PROJ_TN = 1024
OUT_TM = 512


def _run_stream(x, lseq, tm, par, state_conv=None, state_ssm=None, cache_k=None, cache_v=None):
    sample = state_conv is not None
    depth = par["norm_w"].shape[0]
    new = {"v": [], "ssm": [], "conv": [], "k": [], "v_attn": []}
    h = rms_norm_bf16(x, par["norm_w"][0], tm=tm)
    for i in range(depth):
        kind, j = i % N_MIXERS, i // N_MIXERS
        if kind == 0:
            p = matmul(h, par["a_w_in"], j, tm=tm, tn=PROJ_TN, out_dtype=BF16, name="a_in_proj")
            wpos, bias = _gmlp_pos_params(par["a_w_s"][j], par["a_b_s"][j], min(lseq, A_CHUNK))
            r = gmlp_core(p, wpos, bias, par["a_ln_g"][j], par["a_ln_b"][j], emit_v=sample)
            if sample:
                y, v = r
                new["v"].append(v)
            else:
                y = r
            w_out = par["a_w_out"]
        elif kind == 1:
            pz = matmul(h, par["b_w_in"], j, tm=tm, tn=PROJ_TN, n=par["b_zx"], name="b_in_proj")
            dtp = matmul(h, par["b_w_dt"], j, tm=tm, tn=par["b_w_dt"].shape[2], name="b_dt_proj")
            y, s = ssd_core(
                pz, dtp, par["b_conv_w"][j], par["b_conv_b"][j], par["b_dt_bias"][j], par["b_a_log"][j],
                par["b_d_lane"][j], par["b_norm_w"][j], lseq=min(lseq, SSD_L),
                conv_state=state_conv[j] if sample else None,
                ssm_state=state_ssm[j].reshape(-1, B_HEADS // 2, LANES, B_STATE) if sample else None)
            new["ssm"].append(s.reshape(-1, B_HEADS, B_HEAD_DIM, B_STATE))
            new["conv"].append(pz.reshape(-1, lseq, pz.shape[1])[:, lseq - (B_CONV - 1):, 4096:])
            w_out = par["b_w_out"]
        else:
            if sample:
                p = matmul(h, par["c_w_in"], j, tm=tm, tn=PROJ_TN, name="c_in_proj")
                y, k, v = attn_sample(p, cache_k[j], cache_v[j], hb=8)
            else:
                cw = par["c_w_in"].shape[2] // 4
                part = lambda c, hm: matmul(h, par["c_w_in"], j, tm=tm, tn=PROJ_TN, n=cw, col0=c * cw,
                                            head_major=hm, name="c_in_proj")
                q, k, v, zg = part(0, True), part(1, True), part(2, True), part(3, False)
                y = attn_prompt(q, k, v, zg, tq=min(C_TQ, x.shape[0]))
                k, v = k[None], v[None]
            new["k"].append(k)
            new["v_attn"].append(v)
            w_out = par["c_w_out"]
        last = i == depth - 1
        nw = par["final_norm_w"] if last else par["norm_w"][i + 1]
        r = matmul_residual_norm(y, w_out, j, x, nw, tm=OUT_TM, final=last)
        if last:
            return r, new
        x, h = r


def kernel(x_prompt, x_sample, state_ssm, state_conv, cache_k, cache_v, norm_w, final_norm_w, a_w_in, a_ln_g, a_ln_b, a_w_s, a_b_s, a_w_out, b_w_in, b_conv_w, b_conv_b, b_dt_bias, b_a_log, b_d, b_norm_w, b_w_out, c_w_in, c_w_out):
    bp, seq, d = x_prompt.shape
    bs, dseq, _ = x_sample.shape
    nb = b_w_in.shape[0]
    zx = b_w_in.shape[2] - B_HEADS
    par = {
        "norm_w": norm_w, "final_norm_w": final_norm_w,
        "a_w_in": a_w_in, "a_ln_g": a_ln_g, "a_ln_b": a_ln_b, "a_w_s": a_w_s, "a_b_s": a_b_s,
        "a_w_out": a_w_out.astype(BF16),
        "b_w_in": b_w_in, "b_zx": zx, "b_w_dt": _group_heads(b_w_in[:, :, zx:]),
        "b_conv_w": b_conv_w, "b_conv_b": b_conv_b.reshape(nb, 1, -1),
        "b_dt_bias": _group_heads(b_dt_bias).reshape(nb, 1, -1), "b_a_log": _group_heads(b_a_log).reshape(nb, 1, -1),
        "b_d_lane": jnp.repeat(b_d, B_HEAD_DIM, axis=-1).reshape(nb, 1, -1), "b_norm_w": b_norm_w.reshape(nb, 1, -1),
        "b_w_out": b_w_out.astype(BF16), "c_w_in": c_w_in, "c_w_out": c_w_out.astype(BF16),
    }
    assert bp == 1, "the prompt group is one stream"
    yp, newp = _run_stream(x_prompt.reshape(seq, d), seq, min(PROJ_TM, seq), par)
    ys, news = _run_stream(x_sample.reshape(bs * dseq, d), dseq, min(PROJ_TM, bs * dseq), par,
                           state_conv=state_conv, state_ssm=state_ssm, cache_k=cache_k, cache_v=cache_v)
    st = jnp.stack
    return (
        yp.reshape(bp, seq, d),
        ys.reshape(bs, dseq, d),
        st([v.reshape(bs, dseq, -1) for v in news["v"]]),
        st(newp["ssm"]), st(newp["conv"]), st(news["ssm"]), st(news["conv"]),
        st(newp["k"]), st(newp["v_attn"]), st(news["k"]), st(news["v_attn"]),
    )
```

```python
import functools

import jax
import jax.numpy as jnp
from jax import lax
from jax.experimental import pallas as pl
from jax.experimental.pallas import tpu as pltpu

F32 = jnp.float32
BF16 = jnp.bfloat16

NORM_EPS = 1e-6
D_MODEL = 2048
CHUNK = 64
A_GROUPS = 16
A_CHUNK = 128
B_HEADS = 64
B_HEAD_DIM = 64
B_GROUPS = 8
B_STATE = 128
B_CONV = 4
C_HEADS = 16
C_HEAD_DIM = 128

LANES = 128
MIB = 1024 * 1024


def _cparams(sem, vmem_mib):
    return pltpu.CompilerParams(dimension_semantics=sem, vmem_limit_bytes=vmem_mib * MIB)


def _rms_kernel(x_ref, w_ref, h_ref):
    x = x_ref[...]
    ms = jnp.mean(x * x, axis=-1, keepdims=True)
    h_ref[...] = (x * lax.rsqrt(ms + NORM_EPS) * w_ref[...]).astype(h_ref.dtype)


def rms_norm_bf16(x, w, *, tm):
    m, d = x.shape
    return pl.pallas_call(
        _rms_kernel,
        grid=(m // tm,),
        in_specs=[pl.BlockSpec((tm, d), lambda i: (i, 0)), pl.BlockSpec((1, d), lambda i: (0, 0))],
        out_specs=pl.BlockSpec((tm, d), lambda i: (i, 0)),
        out_shape=jax.ShapeDtypeStruct((m, d), BF16),
        compiler_params=_cparams(("parallel",), 32),
        name="rms_norm",
    )(x, w.reshape(1, d))


def _mm_kernel(x_ref, w_ref, o_ref, wb_scr, *, head_major):
    @pl.when(pl.program_id(1) == 0)
    def _():
        wb_scr[...] = w_ref[...].astype(BF16)

    acc = jnp.dot(x_ref[...], wb_scr[...], preferred_element_type=F32)
    if head_major:
        for h in range(o_ref.shape[0]):
            o_ref[h] = acc[:, h * LANES:(h + 1) * LANES].astype(o_ref.dtype)
    else:
        o_ref[...] = acc.astype(o_ref.dtype)


def matmul(x, w, layer, *, tm, tn, n=None, col0=0, head_major=False, out_dtype=F32, name="proj"):
    m, k = x.shape
    n = w.shape[2] if n is None else n
    c0 = col0 // tn
    if head_major:
        hb = tn // LANES
        out_specs = pl.BlockSpec((hb, tm, LANES), lambda j, i: (j, i, 0))
        out_shape = jax.ShapeDtypeStruct((n // LANES, m, LANES), out_dtype)
    else:
        out_specs = pl.BlockSpec((tm, tn), lambda j, i: (i, j))
        out_shape = jax.ShapeDtypeStruct((m, n), out_dtype)
    return pl.pallas_call(
        functools.partial(_mm_kernel, head_major=head_major),
        grid=(n // tn, m // tm),
        in_specs=[pl.BlockSpec((tm, k), lambda j, i: (i, 0)),
                  pl.BlockSpec((None, k, tn), lambda j, i: (layer, 0, c0 + j))],
        out_specs=out_specs,
        out_shape=out_shape,
        scratch_shapes=[pltpu.VMEM((k, tn), BF16)],
        compiler_params=_cparams(("arbitrary", "arbitrary"), 48),
        name=name,
    )(x, w)


def _mm_res_norm_kernel(y_ref, w_ref, x_ref, nw_ref, *out_refs, final):
    xn = x_ref[...] + jnp.dot(y_ref[...], w_ref[...], preferred_element_type=F32)
    ms = jnp.mean(xn * xn, axis=-1, keepdims=True)
    h = xn * lax.rsqrt(ms + NORM_EPS) * nw_ref[...]
    if final:
        out_refs[0][...] = h
    else:
        out_refs[0][...] = xn
        out_refs[1][...] = h.astype(BF16)


def matmul_residual_norm(y, w, layer, x, nw, *, tm, final=False, name="out_proj"):
    m, k = y.shape
    d = w.shape[2]
    row = lambda i: (i, 0)
    if final:
        out_shape = jax.ShapeDtypeStruct((m, d), F32)
        out_specs = pl.BlockSpec((tm, d), row)
    else:
        out_shape = (jax.ShapeDtypeStruct((m, d), F32), jax.ShapeDtypeStruct((m, d), BF16))
        out_specs = (pl.BlockSpec((tm, d), row), pl.BlockSpec((tm, d), row))
    return pl.pallas_call(
        functools.partial(_mm_res_norm_kernel, final=final),
        grid=(m // tm,),
        in_specs=[
            pl.BlockSpec((tm, k), row),
            pl.BlockSpec((None, k, d), lambda i: (layer, 0, 0), pipeline_mode=pl.Buffered(1)),
            pl.BlockSpec((tm, d), row),
            pl.BlockSpec((1, d), lambda i: (0, 0)),
        ],
        out_specs=out_specs,
        out_shape=out_shape,
        compiler_params=_cparams(("parallel",), 56),
        name=name,
    )(y, w, x, nw.reshape(1, d))


def _gelu2(x):
    return x * (1.0 + lax.erf(x * (2.0 ** -0.5)))


def _silu(x):
    return x * (1.0 / (1.0 + jnp.exp(-x)))


A_GW = 256


def _gmlp_kernel(u_ref, v_ref, z_ref, wpos_ref, bias_ref, g_ref, b_ref, *refs, emit_v):
    if emit_v:
        y_ref, vout_ref, gv_scr, vn_scr = refs
    else:
        y_ref, gv_scr, vn_scr = refs
    rows, width = gv_scr.shape
    ngroups = width // A_GW
    acc = jnp.zeros((rows, LANES), F32)
    for g in range(ngroups):
        sl = slice(g * A_GW, (g + 1) * A_GW)
        gv = _gelu2(v_ref[:, sl].astype(F32))
        gv_scr[:, sl] = gv
        acc = acc + gv[:, :LANES] + gv[:, LANES:]
    mean = jnp.sum(acc, axis=-1, keepdims=True) * (1.0 / width)
    acc = jnp.zeros((rows, LANES), F32)
    for g in range(ngroups):
        sl = slice(g * A_GW, (g + 1) * A_GW)
        vc = gv_scr[:, sl] - mean
        sq = vc * vc
        acc = acc + sq[:, :LANES] + sq[:, LANES:]
    var = jnp.sum(acc, axis=-1, keepdims=True) * (1.0 / width)
    rstd = lax.rsqrt(var + 4.0 * NORM_EPS)
    for g in range(ngroups):
        sl = slice(g * A_GW, (g + 1) * A_GW)
        vn = (gv_scr[:, sl] - mean) * rstd * g_ref[:, sl] + b_ref[:, sl]
        if emit_v:
            vout_ref[:, sl] = vn
        vn_scr[:, sl] = vn.astype(BF16)
    for g in range(ngroups):
        sl = slice(g * A_GW, (g + 1) * A_GW)
        s = jnp.dot(wpos_ref[g], vn_scr[:, sl], preferred_element_type=F32) + bias_ref[:, sl]
        y_ref[:, sl] = (_gelu2(u_ref[:, sl].astype(F32)) * s * _silu(z_ref[:, sl].astype(F32))).astype(BF16)


def gmlp_core(p, wpos, bias, ln_g, ln_b, *, emit_v):
    m, w3 = p.shape
    w = w3 // 3
    t = A_CHUNK
    col = lambda c: pl.BlockSpec((t, w), lambda i, c=c: (i, c))
    const2 = lambda shape: pl.BlockSpec(shape, lambda i: (0, 0))
    out_shape = [jax.ShapeDtypeStruct((m, w), BF16)]
    out_specs = [pl.BlockSpec((t, w), lambda i: (i, 0))]
    if emit_v:
        out_shape.append(jax.ShapeDtypeStruct((m, w), F32))
        out_specs.append(pl.BlockSpec((t, w), lambda i: (i, 0)))
    res = pl.pallas_call(
        functools.partial(_gmlp_kernel, emit_v=emit_v),
        grid=(m // t,),
        in_specs=[
            col(0), col(1), col(2),
            pl.BlockSpec(wpos.shape, lambda i: (0, 0, 0)),
            const2((t, w)), const2((1, w)), const2((1, w)),
        ],
        out_specs=out_specs,
        out_shape=out_shape,
        scratch_shapes=[pltpu.VMEM((t, w), F32), pltpu.VMEM((t, w), BF16)],
        compiler_params=_cparams(("parallel",), 40),
        name="gmlp_core",
    )(p, p, p, wpos, bias, ln_g.reshape(1, w), ln_b.reshape(1, w))
    return res if emit_v else res[0]


def _gmlp_pos_params(w_s, b_s, lc):
    pos = jnp.arange(lc)
    mask = (pos[None, :] // CHUNK) <= (pos[:, None] // CHUNK)
    wp = jnp.where(mask[None], w_s[:, :lc, :lc], 0.0)
    reps = A_CHUNK // lc
    if reps > 1:
        eye = jnp.eye(reps, dtype=wp.dtype)
        wp = jnp.einsum("ab,gts->gatbs", eye, wp).reshape(w_s.shape[0], A_CHUNK, A_CHUNK)
    bias_t = jnp.tile(b_s[:, :lc].T, (reps, 1))
    bias = jnp.repeat(bias_t, A_GW, axis=1)
    return (0.5 * wp).astype(BF16), (0.5 * bias).astype(F32)


C_TK = 128
C_SW = 256
C_TQ = 512


def _suffix_sum_matrix():
    j = jnp.arange(C_SW)
    return (j[:, None] >= j[None, :]).astype(BF16)


LOG2E = 1.4426950408889634
C_QSCALE = (C_HEAD_DIM ** -0.5) * LOG2E
C_ZMAX = 126.0


def _sb_scores(q, kw):
    return lax.dot_general(q, kw, (((1,), (1,)), ((), ())), preferred_element_type=F32)


def _sb_weights(z, carry, tmat, vis):
    t, n = z.shape
    sp = jnp.maximum(z, jnp.log2(1.0 + jnp.exp2(jnp.minimum(z, C_ZMAX))))
    if vis is not None:
        sp = jnp.where(vis, sp, 0.0)
    sp = sp.astype(BF16)
    bounds = list(range(0, n, C_SW)) + [n]
    groups = list(zip(bounds[:-1], bounds[1:]))
    sums = {}
    for lo, hi in reversed(groups):
        sums[lo] = jnp.dot(sp[:, lo:hi], tmat[:hi - lo, :hi - lo], preferred_element_type=F32)
    ws = []
    for lo, hi in reversed(groups):
        s = sums[lo]
        for c0 in range(hi - lo - C_TK, -1, -C_TK):
            ws.append(jnp.exp2(z[:, lo + c0:lo + c0 + C_TK] - s[:, c0:c0 + C_TK] - carry))
        carry = carry + jnp.broadcast_to(s[:, 0:1], (t, C_TK))
    w = ws[0] if len(ws) == 1 else jnp.concatenate(ws[::-1], axis=1)
    if vis is not None:
        w = jnp.where(vis, w, 0.0)
    return w.astype(BF16), carry


def _sb_wide(q, kw, vw, carry, tmat, vis):
    w, carry = _sb_weights(_sb_scores(q, kw), carry, tmat, vis)
    return jnp.dot(w, vw, preferred_element_type=F32), carry


def _attn_prompt_kernel(q_ref, k_ref, v_ref, zg_ref, tmat_ref, o_ref,
                        kb_scr, vb_scr, acc_scr, carry_scr, z_scr, w_scr, *, tq):
    i = pl.program_id(1)

    def rows(j):
        return pl.ds(pl.multiple_of(jnp.maximum(j, 0) * tq, tq), tq)

    @pl.when(i == 0)
    def _():
        kb_scr[...] = k_ref[...].astype(BF16)
        vb_scr[...] = v_ref[...].astype(BF16)

    q = (q_ref[...] * C_QSCALE).astype(BF16)
    tmat = tmat_ref[...]

    def stage(j, cur):
        nxt = 1 - cur
        z_scr[nxt] = _sb_scores(q, kb_scr[rows(j - 1), :])
        w_scr[nxt], carry_scr[...] = _sb_weights(z_scr[cur], carry_scr[...], tmat, None)
        acc_scr[...] += jnp.dot(w_scr[cur], vb_scr[rows(j + 1), :], preferred_element_type=F32)

    def finish(cur):
        acc = acc_scr[...] + jnp.dot(w_scr[cur], vb_scr[rows(0), :], preferred_element_type=F32)
        o_ref[...] = (acc * _silu(zg_ref[...])).astype(o_ref.dtype)

    qpos = lax.broadcasted_iota(jnp.int32, (tq, tq), 0)
    kpos = lax.broadcasted_iota(jnp.int32, (tq, tq), 1)
    w_scr[0], carry_scr[...] = _sb_weights(
        _sb_scores(q, kb_scr[rows(i), :]), jnp.zeros((tq, C_TK), F32), tmat, kpos < qpos)
    z_scr[0] = _sb_scores(q, kb_scr[rows(i - 1), :])
    acc_scr[...] = jnp.zeros_like(acc_scr)

    def body(it, c):
        j = i - 1 - 2 * it
        stage(j, 0)
        stage(j - 1, 1)
        return c

    lax.fori_loop(0, i // 2, body, 0)

    @pl.when(i % 2 == 1)
    def _():
        stage(0, 0)
        finish(1)

    @pl.when(i % 2 == 0)
    def _():
        finish(0)


def attn_prompt(q, k, v, zg, *, tq):
    nh, seq, d = q.shape
    hblk = pl.BlockSpec((None, tq, d), lambda h, i: (h, i, 0))
    head = pl.BlockSpec((None, seq, d), lambda h, i: (h, 0, 0))
    tmat = _suffix_sum_matrix()
    return pl.pallas_call(
        functools.partial(_attn_prompt_kernel, tq=tq),
        grid=(nh, seq // tq),
        in_specs=[hblk, head, head, pl.BlockSpec((tq, d), lambda h, i: (i, h)),
                  pl.BlockSpec(tmat.shape, lambda h, i: (0, 0))],
        out_specs=pl.BlockSpec((tq, d), lambda h, i: (i, h)),
        out_shape=jax.ShapeDtypeStruct((seq, nh * d), BF16),
        scratch_shapes=[
            pltpu.VMEM((seq, d), BF16), pltpu.VMEM((seq, d), BF16),
            pltpu.VMEM((tq, d), F32), pltpu.VMEM((tq, C_TK), F32),
            pltpu.VMEM((2, tq, tq), F32), pltpu.VMEM((2, tq, tq), BF16),
        ],
        compiler_params=_cparams(("arbitrary", "arbitrary"), 56),
        name="attn_prompt",
    )(q, k, v, zg, tmat)


def _attn_sample_kernel(q_ref, kn_ref, vn_ref, zg_ref, kc_ref, vc_ref, tmat_ref, o_ref, ko_ref, vo_ref, *, hb, past):
    t = q_ref.shape[0]
    d = C_HEAD_DIM
    tmat = tmat_ref[...]
    qpos = lax.broadcasted_iota(jnp.int32, (t, C_TK), 0)
    kpos = lax.broadcasted_iota(jnp.int32, (t, C_TK), 1)
    vis_new = kpos < qpos
    pad = jnp.zeros((C_TK - t, d), BF16)
    for h in range(hb):
        sl = slice(h * d, (h + 1) * d)
        kn = kn_ref[:, sl]
        vn = vn_ref[:, sl]
        ko_ref[h] = kn
        vo_ref[h] = vn
        q = (q_ref[:, sl] * C_QSCALE).astype(BF16)
        carry = jnp.zeros((t, C_TK), F32)
        acc, carry = _sb_wide(q, jnp.concatenate([kn.astype(BF16), pad], axis=0),
                              jnp.concatenate([vn.astype(BF16), pad], axis=0), carry, tmat, vis_new)
        pv, carry = _sb_wide(q, kc_ref[h].astype(BF16), vc_ref[h].astype(BF16), carry, tmat, None)
        acc = acc + pv
        o_ref[:, sl] = (acc * _silu(zg_ref[:, sl])).astype(o_ref.dtype)


def attn_sample(p, cache_k, cache_v, *, hb):
    bsz, nh, past, d = cache_k.shape
    t = p.shape[0] // bsz
    w = nh * d
    ng = nh // hb
    blk = lambda c: pl.BlockSpec((t, hb * d), lambda b, g, c=c: (b, c * ng + g))
    cspec = pl.BlockSpec((None, hb, past, d), lambda b, g: (b, g, 0, 0))
    nspec = pl.BlockSpec((None, hb, t, d), lambda b, g: (b, g, 0, 0))
    tmat = _suffix_sum_matrix()
    return pl.pallas_call(
        functools.partial(_attn_sample_kernel, hb=hb, past=past),
        grid=(bsz, ng),
        in_specs=[blk(0), blk(1), blk(2), blk(3), cspec, cspec, pl.BlockSpec(tmat.shape, lambda b, g: (0, 0))],
        out_specs=[pl.BlockSpec((t, hb * d), lambda b, g: (b, g)), nspec, nspec],
        out_shape=[
            jax.ShapeDtypeStruct((bsz * t, w), BF16),
            jax.ShapeDtypeStruct((bsz, nh, t, d), F32),
            jax.ShapeDtypeStruct((bsz, nh, t, d), F32),
        ],
        compiler_params=_cparams(("parallel", "parallel"), 40),
        name="attn_sample",
    )(p, p, p, p, cache_k, cache_v, tmat)


SSD_L = 128
SSD_HG = 16
SSD_PAIRS = SSD_HG // 2
SSD_W = SSD_HG * B_HEAD_DIM
SSD_GB = 2 * B_STATE
HALO = 8


def _split3(x):
    hi = x.astype(BF16)
    r = x - hi.astype(F32)
    mid = r.astype(BF16)
    lo = (r - mid.astype(F32)).astype(BF16)
    return hi, mid, lo


def _dot_x01(x, m):
    hi, mid, lo = _split3(x)
    d = lambda a: jnp.dot(a, m, preferred_element_type=F32)
    return (d(hi) + d(mid)) + d(lo)


def _dot_01x(m, x):
    hi, mid, lo = _split3(x)
    d = lambda a: jnp.dot(m, a, preferred_element_type=F32)
    return (d(hi) + d(mid)) + d(lo)


def _softplus(x):
    return jnp.maximum(x, 0.0) + jnp.log1p(jnp.exp(-jnp.abs(x)))


def _ssd_kernel(*refs, nseq, sample):
    (z_ref, x_ref, b_ref, c_ref, dt_ref, cwx_ref, cwb_ref, cwc_ref, cbx_ref, cbb_ref, cbc_ref,
     dtb_ref, alog_ref, dl_ref, nw_ref, tri_ref, same_ref, e64_ref, el_ref) = refs[:19]
    refs = refs[19:]
    if sample:
        hx_ref, hb_ref, hc_ref, s0_ref = refs[:4]
        refs = refs[4:]
    (y_ref, so_ref, extx, extb, extc, st_scr, xs_scr, bs_scr, cs_scr, ecx_scr, wex_scr, dcx_scr,
     cc_scr, cumt_scr, dtt_scr, yacc_scr) = refs
    L = SSD_L
    lseq = L // nseq
    c = pl.program_id(1)
    nchunks = pl.num_programs(1)

    if sample:
        for s in range(nseq):
            extx[s, HALO - 3:HALO, :] = hx_ref[s]
            extb[s, HALO - 3:HALO, :] = hb_ref[s]
            extc[s, HALO - 3:HALO, :] = hc_ref[s]
            for pp in range(SSD_PAIRS):
                st_scr[s, pp] = s0_ref[s, pp].T
    else:
        @pl.when(c == 0)
        def _():
            extx[0, 0:HALO, :] = jnp.zeros((HALO, SSD_W), F32)
            extb[0, 0:HALO, :] = jnp.zeros((HALO, SSD_GB), F32)
            extc[0, 0:HALO, :] = jnp.zeros((HALO, SSD_GB), F32)
            st_scr[...] = jnp.zeros_like(st_scr)

    def conv(ext, src_ref, cw_ref, cb_ref, dst):
        for s in range(nseq):
            ext[s, HALO:HALO + lseq, :] = src_ref[s * lseq:(s + 1) * lseq, :]
        for s in range(nseq):
            acc = cb_ref[...]
            for k in range(B_CONV):
                acc = acc + ext[s, HALO - 3 + k:HALO - 3 + k + lseq, :] * cw_ref[k:k + 1, :]
            dst[s * lseq:(s + 1) * lseq, :] = _silu(acc).astype(dst.dtype)
        if not sample:
            ext[0, 0:HALO, :] = ext[0, lseq:lseq + HALO, :]

    conv(extx, x_ref, cwx_ref, cbx_ref, xs_scr)
    conv(extb, b_ref, cwb_ref, cbb_ref, bs_scr)
    conv(extc, c_ref, cwc_ref, cbc_ref, cs_scr)
    yacc_scr[...] = _silu(z_ref[...])

    dt = _softplus(dt_ref[...] + dtb_ref[...])
    dta = dt * (-jnp.exp(alog_ref[...]))
    cum = _dot_01x(tri_ref[...], dta)
    if nseq == 1:
        ctot = jnp.broadcast_to(cum[L - 1:L, :], (L, LANES))
    else:
        ctot = _dot_01x(same_ref[...], dta)
    cumt_scr[...] = cum.T
    dtt_scr[...] = dt.T
    e64 = e64_ref[...]
    expand = lambda a: jnp.dot(a.astype(BF16), e64, preferred_element_type=F32)
    ecx_scr[...] = expand(jnp.exp(cum))
    wex_scr[...] = expand(jnp.exp(ctot - cum) * dt)
    drows = HALO if nseq == 1 else L
    dcx_scr[0:drows, :] = _dot_x01(jnp.exp(ctot[0:drows]), e64)
    cc_scr[...] = _dot_x01(cum, el_ref[...])
    mask = tri_ref[...] > 0
    lane = lax.broadcasted_iota(jnp.int32, (L, LANES), 1)
    first = lane < B_HEAD_DIM

    for g2 in range(2):
        bg = bs_scr[:, g2 * B_STATE:(g2 + 1) * B_STATE]
        cg = cs_scr[:, g2 * B_STATE:(g2 + 1) * B_STATE]
        cb = lax.dot_general(cg, bg, (((1,), (1,)), ((), ())), preferred_element_type=F32)
        for p in range(SSD_PAIRS // 2):
            pp = g2 * (SSD_PAIRS // 2) + p
            lanes = slice(pp * LANES, (pp + 1) * LANES)
            ms = []
            for r in (2 * pp, 2 * pp + 1):
                seg = cc_scr[:, r * L:(r + 1) * L] - cumt_scr[r:r + 1, :]
                ms.append((cb * jnp.exp(jnp.where(mask, seg, -jnp.inf)) * dtt_scr[r:r + 1, :]).astype(BF16))
            xp = xs_scr[:, lanes]
            xa = jnp.where(first, xp, 0.0).astype(BF16)
            xb = jnp.where(first, 0.0, xp).astype(BF16)
            y = jnp.dot(jnp.concatenate(ms, axis=1), jnp.concatenate([xa, xb], axis=0), preferred_element_type=F32)
            xw = (xp * wex_scr[:, lanes]).astype(BF16)
            ys = []
            for s in range(nseq):
                rows = slice(s * lseq, (s + 1) * lseq)
                st = st_scr[s, pp]
                ys.append(jnp.dot(cg[rows], st.astype(BF16), preferred_element_type=F32))
                upd = lax.dot_general(bg[rows], xw[rows], (((0,), (0,)), ((), ())), preferred_element_type=F32)
                st_scr[s, pp] = st * dcx_scr[s * lseq:s * lseq + 1, lanes] + upd
            ystate = ys[0] if nseq == 1 else jnp.concatenate(ys, axis=0)
            y = y + ystate * ecx_scr[:, lanes] + dl_ref[:, lanes] * xp
            yacc_scr[:, lanes] = y * yacc_scr[:, lanes]

    gw = SSD_W // 2
    for g2 in range(2):
        sl = slice(g2 * gw, (g2 + 1) * gw)
        yg = yacc_scr[:, sl]
        ms_ = jnp.mean(yg * yg, axis=-1, keepdims=True)
        y_ref[:, sl] = (yg * lax.rsqrt(ms_ + NORM_EPS) * nw_ref[:, sl]).astype(y_ref.dtype)

    if sample:
        for s in range(nseq):
            for pp in range(SSD_PAIRS):
                so_ref[s, pp] = st_scr[s, pp].T
    else:
        @pl.when(c == nchunks - 1)
        def _():
            for pp in range(SSD_PAIRS):
                so_ref[0, pp] = st_scr[0, pp].T


def _ssd_consts(lseq):
    L = SSD_L
    t = jnp.arange(L)
    same = (t[:, None] // lseq) == (t[None, :] // lseq)
    tri = same & (t[None, :] <= t[:, None])
    r = jnp.arange(LANES)
    e64 = (r[:, None] == (jnp.arange(SSD_W)[None, :] // B_HEAD_DIM)) & (r[:, None] < SSD_HG)
    el = (r[:, None] == (jnp.arange(SSD_HG * L)[None, :] // L)) & (r[:, None] < SSD_HG)
    return tri.astype(BF16), same.astype(BF16), e64.astype(BF16), el.astype(BF16)


def ssd_core(pz, dtp, conv_w, conv_b, dtb, alog, dlane, nw, *, lseq, conv_state=None, ssm_state=None):
    m = pz.shape[0]
    sample = conv_state is not None
    nseq = SSD_L // lseq
    nchunks = m // SSD_L
    nslices = B_HEADS // SSD_HG
    tri, same, e64, el = _ssd_consts(lseq)
    nx = 4096 // SSD_W
    rowblk = lambda w, off: pl.BlockSpec((SSD_L, w), lambda j, c, off=off: (c, off + j))
    parblk = lambda r, w, off: pl.BlockSpec((r, w), lambda j, c, off=off: (0, off + j))
    const = lambda a: pl.BlockSpec(a.shape, lambda j, c: (0, 0))
    ob = 8192 // SSD_GB
    oc = 9216 // SSD_GB
    in_specs = [
        rowblk(SSD_W, 0), rowblk(SSD_W, nx), rowblk(SSD_GB, ob), rowblk(SSD_GB, oc), rowblk(LANES, 0),
        parblk(B_CONV, SSD_W, 0), parblk(B_CONV, SSD_GB, 4096 // SSD_GB), parblk(B_CONV, SSD_GB, 5120 // SSD_GB),
        parblk(1, SSD_W, 0), parblk(1, SSD_GB, 4096 // SSD_GB), parblk(1, SSD_GB, 5120 // SSD_GB),
        parblk(1, LANES, 0), parblk(1, LANES, 0), parblk(1, SSD_W, 0), parblk(1, SSD_W, 0),
        const(tri), const(same), const(e64), const(el),
    ]
    args = [pz, pz, pz, pz, dtp, conv_w, conv_w, conv_w, conv_b, conv_b, conv_b, dtb, alog, dlane, nw, tri, same, e64, el]
    if sample:
        hblk = lambda w, off: pl.BlockSpec((nseq, B_CONV - 1, w), lambda j, c, off=off: (c, 0, off + j))
        in_specs += [hblk(SSD_W, 0), hblk(SSD_GB, 4096 // SSD_GB), hblk(SSD_GB, 5120 // SSD_GB),
                     pl.BlockSpec((nseq, SSD_PAIRS, LANES, B_STATE), lambda j, c: (c, j, 0, 0))]
        args += [conv_state, conv_state, conv_state, ssm_state]
        nstates = m // lseq
        so_spec = pl.BlockSpec((nseq, SSD_PAIRS, LANES, B_STATE), lambda j, c: (c, j, 0, 0))
    else:
        nstates = 1
        so_spec = pl.BlockSpec((1, SSD_PAIRS, LANES, B_STATE), lambda j, c: (0, j, 0, 0))
    L = SSD_L
    scratch = [
        pltpu.VMEM((nseq, HALO + lseq, SSD_W), F32), pltpu.VMEM((nseq, HALO + lseq, SSD_GB), F32),
        pltpu.VMEM((nseq, HALO + lseq, SSD_GB), F32),
        pltpu.VMEM((nseq, SSD_PAIRS, B_STATE, LANES), F32),
        pltpu.VMEM((L, SSD_W), F32), pltpu.VMEM((L, SSD_GB), BF16), pltpu.VMEM((L, SSD_GB), BF16),
        pltpu.VMEM((L, SSD_W), F32), pltpu.VMEM((L, SSD_W), F32), pltpu.VMEM((L, SSD_W), F32),
        pltpu.VMEM((L, SSD_HG * L), F32), pltpu.VMEM((LANES, L), F32), pltpu.VMEM((LANES, L), F32),
        pltpu.VMEM((L, SSD_W), F32),
    ]
    return pl.pallas_call(
        functools.partial(_ssd_kernel, nseq=nseq, sample=sample),
        grid=(nslices, nchunks),
        in_specs=in_specs,
        out_specs=[pl.BlockSpec((SSD_L, SSD_W), lambda j, c: (c, j)), so_spec],
        out_shape=[jax.ShapeDtypeStruct((m, B_HEADS * B_HEAD_DIM), BF16),
                   jax.ShapeDtypeStruct((nstates, B_HEADS // 2, LANES, B_STATE), F32)],
        scratch_shapes=scratch,
        compiler_params=_cparams(("arbitrary", "arbitrary"), 48),
        name="ssd_core",
    )(*args)


def _group_heads(v, fill=0.0):
    lead = v.shape[:-1]
    g = v.reshape(*lead, B_HEADS // SSD_HG, SSD_HG)
    g = jnp.pad(g, [(0, 0)] * (len(lead) + 1) + [(0, LANES - SSD_HG)], constant_values=fill)
    return g.reshape(*lead, (B_HEADS // SSD_HG) * LANES)


N_MIXERS = 3
PROJ_TM = 1024
PROJ_TN = 1024
OUT_TM = 512


def _run_stream(x, lseq, tm, par, state_conv=None, state_ssm=None, cache_k=None, cache_v=None):
    sample = state_conv is not None
    depth = par["norm_w"].shape[0]
    new = {"v": [], "ssm": [], "conv": [], "k": [], "v_attn": []}
    h = rms_norm_bf16(x, par["norm_w"][0], tm=tm)
    for i in range(depth):
        kind, j = i % N_MIXERS, i // N_MIXERS
        if kind == 0:
            p = matmul(h, par["a_w_in"], j, tm=tm, tn=PROJ_TN, out_dtype=BF16, name="a_in_proj")
            wpos, bias = _gmlp_pos_params(par["a_w_s"][j], par["a_b_s"][j], min(lseq, A_CHUNK))
            r = gmlp_core(p, wpos, bias, par["a_ln_g"][j], par["a_ln_b"][j], emit_v=sample)
            if sample:
                y, v = r
                new["v"].append(v)
            else:
                y = r
            w_out = par["a_w_out"]
        elif kind == 1:
            pz = matmul(h, par["b_w_in"], j, tm=tm, tn=PROJ_TN, n=par["b_zx"], name="b_in_proj")
            dtp = matmul(h, par["b_w_dt"], j, tm=tm, tn=par["b_w_dt"].shape[2], name="b_dt_proj")
            y, s = ssd_core(
                pz, dtp, par["b_conv_w"][j], par["b_conv_b"][j], par["b_dt_bias"][j], par["b_a_log"][j],
                par["b_d_lane"][j], par["b_norm_w"][j], lseq=min(lseq, SSD_L),
                conv_state=state_conv[j] if sample else None,
                ssm_state=state_ssm[j].reshape(-1, B_HEADS // 2, LANES, B_STATE) if sample else None)
            new["ssm"].append(s.reshape(-1, B_HEADS, B_HEAD_DIM, B_STATE))
            new["conv"].append(pz.reshape(-1, lseq, pz.shape[1])[:, lseq - (B_CONV - 1):, 4096:])
            w_out = par["b_w_out"]
        else:
            if sample:
                p = matmul(h, par["c_w_in"], j, tm=tm, tn=PROJ_TN, name="c_in_proj")
                y, k, v = attn_sample(p, cache_k[j], cache_v[j], hb=8)
            else:
                cw = par["c_w_in"].shape[2] // 4
                part = lambda c, hm: matmul(h, par["c_w_in"], j, tm=tm, tn=PROJ_TN, n=cw, col0=c * cw,
                                            head_major=hm, name="c_in_proj")
                q, k, v, zg = part(0, True), part(1, True), part(2, True), part(3, False)
                y = attn_prompt(q, k, v, zg, tq=min(C_TQ, x.shape[0]))
                k, v = k[None], v[None]
            new["k"].append(k)
            new["v_attn"].append(v)
            w_out = par["c_w_out"]
        last = i == depth - 1
        nw = par["final_norm_w"] if last else par["norm_w"][i + 1]
        r = matmul_residual_norm(y, w_out, j, x, nw, tm=OUT_TM, final=last)
        if last:
            return r, new
        x, h = r


def kernel(x_prompt, x_sample, state_ssm, state_conv, cache_k, cache_v, norm_w, final_norm_w, a_w_in, a_ln_g, a_ln_b, a_w_s, a_b_s, a_w_out, b_w_in, b_conv_w, b_conv_b, b_dt_bias, b_a_log, b_d, b_norm_w, b_w_out, c_w_in, c_w_out):
    bp, seq, d = x_prompt.shape
    bs, dseq, _ = x_sample.shape
    nb = b_w_in.shape[0]
    zx = b_w_in.shape[2] - B_HEADS
    par = {
        "norm_w": norm_w, "final_norm_w": final_norm_w,
        "a_w_in": a_w_in, "a_ln_g": a_ln_g, "a_ln_b": a_ln_b, "a_w_s": a_w_s, "a_b_s": a_b_s,
        "a_w_out": a_w_out.astype(BF16),
        "b_w_in": b_w_in, "b_zx": zx, "b_w_dt": _group_heads(b_w_in[:, :, zx:]),
        "b_conv_w": b_conv_w, "b_conv_b": b_conv_b.reshape(nb, 1, -1),
        "b_dt_bias": _group_heads(b_dt_bias).reshape(nb, 1, -1), "b_a_log": _group_heads(b_a_log).reshape(nb, 1, -1),
        "b_d_lane": jnp.repeat(b_d, B_HEAD_DIM, axis=-1).reshape(nb, 1, -1), "b_norm_w": b_norm_w.reshape(nb, 1, -1),
        "b_w_out": b_w_out.astype(BF16), "c_w_in": c_w_in, "c_w_out": c_w_out.astype(BF16),
    }
    assert bp == 1, "the prompt group is one stream"
    yp, newp = _run_stream(x_prompt.reshape(seq, d), seq, min(PROJ_TM, seq), par)
    ys, news = _run_stream(x_sample.reshape(bs * dseq, d), dseq, min(PROJ_TM, bs * dseq), par,
                           state_conv=state_conv, state_ssm=state_ssm, cache_k=cache_k, cache_v=cache_v)
    st = jnp.stack
    return (
        yp.reshape(bp, seq, d),
        ys.reshape(bs, dseq, d),
        st([v.reshape(bs, dseq, -1) for v in news["v"]]),
        st(newp["ssm"]), st(newp["conv"]), st(news["ssm"]), st(news["conv"]),
        st(newp["k"]), st(newp["v_attn"]), st(news["k"]), st(news["v_attn"]),
    )
```

```python
import functools

import jax
import jax.numpy as jnp
from jax import lax
from jax.experimental import pallas as pl
from jax.experimental.pallas import tpu as pltpu

F32 = jnp.float32
BF16 = jnp.bfloat16

NORM_EPS = 1e-6
D_MODEL = 2048
CHUNK = 64
A_GROUPS = 16
A_CHUNK = 128
B_HEADS = 64
B_HEAD_DIM = 64
B_GROUPS = 8
B_STATE = 128
B_CONV = 4
C_HEADS = 16
C_HEAD_DIM = 128

LANES = 128
MIB = 1024 * 1024


def _cparams(sem, vmem_mib):
    return pltpu.CompilerParams(dimension_semantics=sem, vmem_limit_bytes=vmem_mib * MIB)


def _rms_kernel(x_ref, w_ref, h_ref):
    x = x_ref[...]
    ms = jnp.mean(x * x, axis=-1, keepdims=True)
    h_ref[...] = (x * lax.rsqrt(ms + NORM_EPS) * w_ref[...]).astype(h_ref.dtype)


def rms_norm_bf16(x, w, *, tm):
    m, d = x.shape
    return pl.pallas_call(
        _rms_kernel,
        grid=(m // tm,),
        in_specs=[pl.BlockSpec((tm, d), lambda i: (i, 0)), pl.BlockSpec((1, d), lambda i: (0, 0))],
        out_specs=pl.BlockSpec((tm, d), lambda i: (i, 0)),
        out_shape=jax.ShapeDtypeStruct((m, d), BF16),
        compiler_params=_cparams(("parallel",), 32),
        name="rms_norm",
    )(x, w.reshape(1, d))


def _mm_kernel(x_ref, w_ref, o_ref, wb_scr, *, head_major):
    @pl.when(pl.program_id(1) == 0)
    def _():
        wb_scr[...] = w_ref[...].astype(BF16)

    acc = jnp.dot(x_ref[...], wb_scr[...], preferred_element_type=F32)
    if head_major:
        for h in range(o_ref.shape[0]):
            o_ref[h] = acc[:, h * LANES:(h + 1) * LANES].astype(o_ref.dtype)
    else:
        o_ref[...] = acc.astype(o_ref.dtype)


def matmul(x, w, layer, *, tm, tn, n=None, col0=0, head_major=False, out_dtype=F32, name="proj"):
    m, k = x.shape
    n = w.shape[2] if n is None else n
    c0 = col0 // tn
    if head_major:
        hb = tn // LANES
        out_specs = pl.BlockSpec((hb, tm, LANES), lambda j, i: (j, i, 0))
        out_shape = jax.ShapeDtypeStruct((n // LANES, m, LANES), out_dtype)
    else:
        out_specs = pl.BlockSpec((tm, tn), lambda j, i: (i, j))
        out_shape = jax.ShapeDtypeStruct((m, n), out_dtype)
    return pl.pallas_call(
        functools.partial(_mm_kernel, head_major=head_major),
        grid=(n // tn, m // tm),
        in_specs=[pl.BlockSpec((tm, k), lambda j, i: (i, 0)),
                  pl.BlockSpec((None, k, tn), lambda j, i: (layer, 0, c0 + j))],
        out_specs=out_specs,
        out_shape=out_shape,
        scratch_shapes=[pltpu.VMEM((k, tn), BF16)],
        compiler_params=_cparams(("arbitrary", "arbitrary"), 48),
        name=name,
    )(x, w)


def _mm_res_norm_kernel(y_ref, w_ref, x_ref, nw_ref, *out_refs, final):
    xn = x_ref[...] + jnp.dot(y_ref[...], w_ref[...], preferred_element_type=F32)
    ms = jnp.mean(xn * xn, axis=-1, keepdims=True)
    h = xn * lax.rsqrt(ms + NORM_EPS) * nw_ref[...]
    if final:
        out_refs[0][...] = h
    else:
        out_refs[0][...] = xn
        out_refs[1][...] = h.astype(BF16)


def matmul_residual_norm(y, w, layer, x, nw, *, tm, final=False, name="out_proj"):
    m, k = y.shape
    d = w.shape[2]
    row = lambda i: (i, 0)
    if final:
        out_shape = jax.ShapeDtypeStruct((m, d), F32)
        out_specs = pl.BlockSpec((tm, d), row)
    else:
        out_shape = (jax.ShapeDtypeStruct((m, d), F32), jax.ShapeDtypeStruct((m, d), BF16))
        out_specs = (pl.BlockSpec((tm, d), row), pl.BlockSpec((tm, d), row))
    return pl.pallas_call(
        functools.partial(_mm_res_norm_kernel, final=final),
        grid=(m // tm,),
        in_specs=[
            pl.BlockSpec((tm, k), row),
            pl.BlockSpec((None, k, d), lambda i: (layer, 0, 0), pipeline_mode=pl.Buffered(1)),
            pl.BlockSpec((tm, d), row),
            pl.BlockSpec((1, d), lambda i: (0, 0)),
        ],
        out_specs=out_specs,
        out_shape=out_shape,
        compiler_params=_cparams(("parallel",), 56),
        name=name,
    )(y, w, x, nw.reshape(1, d))


def _gelu2(x):
    return x * (1.0 + lax.erf(x * (2.0 ** -0.5)))


def _silu(x):
    return x * (1.0 / (1.0 + jnp.exp(-x)))


A_GW = 256


def _gmlp_kernel(u_ref, v_ref, z_ref, wpos_ref, bias_ref, g_ref, b_ref, *refs, emit_v):
    if emit_v:
        y_ref, vout_ref, gv_scr, vn_scr = refs
    else:
        y_ref, gv_scr, vn_scr = refs
    rows, width = gv_scr.shape
    ngroups = width // A_GW
    acc = jnp.zeros((rows, LANES), F32)
    for g in range(ngroups):
        sl = slice(g * A_GW, (g + 1) * A_GW)
        gv = _gelu2(v_ref[:, sl].astype(F32))
        gv_scr[:, sl] = gv
        acc = acc + gv[:, :LANES] + gv[:, LANES:]
    mean = jnp.sum(acc, axis=-1, keepdims=True) * (1.0 / width)
    acc = jnp.zeros((rows, LANES), F32)
    for g in range(ngroups):
        sl = slice(g * A_GW, (g + 1) * A_GW)
        vc = gv_scr[:, sl] - mean
        sq = vc * vc
        acc = acc + sq[:, :LANES] + sq[:, LANES:]
    var = jnp.sum(acc, axis=-1, keepdims=True) * (1.0 / width)
    rstd = lax.rsqrt(var + 4.0 * NORM_EPS)
    for g in range(ngroups):
        sl = slice(g * A_GW, (g + 1) * A_GW)
        vn = (gv_scr[:, sl] - mean) * rstd * g_ref[:, sl] + b_ref[:, sl]
        if emit_v:
            vout_ref[:, sl] = vn
        vn_scr[:, sl] = vn.astype(BF16)
    for g in range(ngroups):
        sl = slice(g * A_GW, (g + 1) * A_GW)
        s = jnp.dot(wpos_ref[g], vn_scr[:, sl], preferred_element_type=F32) + bias_ref[:, sl]
        y_ref[:, sl] = (_gelu2(u_ref[:, sl].astype(F32)) * s * _silu(z_ref[:, sl].astype(F32))).astype(BF16)


def gmlp_core(p, wpos, bias, ln_g, ln_b, *, emit_v):
    m, w3 = p.shape
    w = w3 // 3
    t = A_CHUNK
    col = lambda c: pl.BlockSpec((t, w), lambda i, c=c: (i, c))
    const2 = lambda shape: pl.BlockSpec(shape, lambda i: (0, 0))
    out_shape = [jax.ShapeDtypeStruct((m, w), BF16)]
    out_specs = [pl.BlockSpec((t, w), lambda i: (i, 0))]
    if emit_v:
        out_shape.append(jax.ShapeDtypeStruct((m, w), F32))
        out_specs.append(pl.BlockSpec((t, w), lambda i: (i, 0)))
    res = pl.pallas_call(
        functools.partial(_gmlp_kernel, emit_v=emit_v),
        grid=(m // t,),
        in_specs=[
            col(0), col(1), col(2),
            pl.BlockSpec(wpos.shape, lambda i: (0, 0, 0)),
            const2((t, w)), const2((1, w)), const2((1, w)),
        ],
        out_specs=out_specs,
        out_shape=out_shape,
        scratch_shapes=[pltpu.VMEM((t, w), F32), pltpu.VMEM((t, w), BF16)],
        compiler_params=_cparams(("parallel",), 40),
        name="gmlp_core",
    )(p, p, p, wpos, bias, ln_g.reshape(1, w), ln_b.reshape(1, w))
    return res if emit_v else res[0]


def _gmlp_pos_params(w_s, b_s, lc):
    pos = jnp.arange(lc)
    mask = (pos[None, :] // CHUNK) <= (pos[:, None] // CHUNK)
    wp = jnp.where(mask[None], w_s[:, :lc, :lc], 0.0)
    reps = A_CHUNK // lc
    if reps > 1:
        eye = jnp.eye(reps, dtype=wp.dtype)
        wp = jnp.einsum("ab,gts->gatbs", eye, wp).reshape(w_s.shape[0], A_CHUNK, A_CHUNK)
    bias_t = jnp.tile(b_s[:, :lc].T, (reps, 1))
    bias = jnp.repeat(bias_t, A_GW, axis=1)
    return (0.5 * wp).astype(BF16), (0.5 * bias).astype(F32)


C_TK = 128
C_SW = 256
C_TQ = 512


def _suffix_sum_matrix():
    j = jnp.arange(C_SW)
    return (j[:, None] >= j[None, :]).astype(BF16)


LOG2E = 1.4426950408889634
C_QSCALE = (C_HEAD_DIM ** -0.5) * LOG2E
C_ZMAX = 126.0


def _sb_scores(q, kw):
    return lax.dot_general(q, kw, (((1,), (1,)), ((), ())), preferred_element_type=F32)


def _sb_weights(z, carry, tmat, vis):
    t, n = z.shape
    sp = jnp.maximum(z, jnp.log2(1.0 + jnp.exp2(jnp.minimum(z, C_ZMAX))))
    if vis is not None:
        sp = jnp.where(vis, sp, 0.0)
    sp = sp.astype(BF16)
    bounds = list(range(0, n, C_SW)) + [n]
    groups = list(zip(bounds[:-1], bounds[1:]))
    sums = {}
    for lo, hi in reversed(groups):
        sums[lo] = jnp.dot(sp[:, lo:hi], tmat[:hi - lo, :hi - lo], preferred_element_type=F32)
    ws = []
    for lo, hi in reversed(groups):
        s = sums[lo]
        for c0 in range(hi - lo - C_TK, -1, -C_TK):
            ws.append(jnp.exp2(z[:, lo + c0:lo + c0 + C_TK] - s[:, c0:c0 + C_TK] - carry))
        carry = carry + jnp.broadcast_to(s[:, 0:1], (t, C_TK))
    w = ws[0] if len(ws) == 1 else jnp.concatenate(ws[::-1], axis=1)
    if vis is not None:
        w = jnp.where(vis, w, 0.0)
    return w.astype(BF16), carry


def _sb_wide(q, kw, vw, carry, tmat, vis):
    w, carry = _sb_weights(_sb_scores(q, kw), carry, tmat, vis)
    return jnp.dot(w, vw, preferred_element_type=F32), carry


def _attn_prompt_kernel(q_ref, k_ref, v_ref, zg_ref, tmat_ref, o_ref,
                        kb_scr, vb_scr, acc_scr, carry_scr, z_scr, w_scr, *, tq):
    i = pl.program_id(1)

    def rows(j):
        return pl.ds(pl.multiple_of(jnp.maximum(j, 0) * tq, tq), tq)

    @pl.when(i == 0)
    def _():
        kb_scr[...] = k_ref[...].astype(BF16)
        vb_scr[...] = v_ref[...].astype(BF16)

    q = (q_ref[...] * C_QSCALE).astype(BF16)
    tmat = tmat_ref[...]

    def stage(j, cur):
        nxt = 1 - cur
        z_scr[nxt] = _sb_scores(q, kb_scr[rows(j - 1), :])
        w_scr[nxt], carry_scr[...] = _sb_weights(z_scr[cur], carry_scr[...], tmat, None)
        acc_scr[...] += jnp.dot(w_scr[cur], vb_scr[rows(j + 1), :], preferred_element_type=F32)

    def finish(cur):
        acc = acc_scr[...] + jnp.dot(w_scr[cur], vb_scr[rows(0), :], preferred_element_type=F32)
        o_ref[...] = (acc * _silu(zg_ref[...])).astype(o_ref.dtype)

    qpos = lax.broadcasted_iota(jnp.int32, (tq, tq), 0)
    kpos = lax.broadcasted_iota(jnp.int32, (tq, tq), 1)
    w_scr[0], carry_scr[...] = _sb_weights(
        _sb_scores(q, kb_scr[rows(i), :]), jnp.zeros((tq, C_TK), F32), tmat, kpos < qpos)
    z_scr[0] = _sb_scores(q, kb_scr[rows(i - 1), :])
    acc_scr[...] = jnp.zeros_like(acc_scr)

    def body(it, c):
        j = i - 1 - 2 * it
        stage(j, 0)
        stage(j - 1, 1)
        return c

    lax.fori_loop(0, i // 2, body, 0)

    @pl.when(i % 2 == 1)
    def _():
        stage(0, 0)
        finish(1)

    @pl.when(i % 2 == 0)
    def _():
        finish(0)


def attn_prompt(q, k, v, zg, *, tq):
    nh, seq, d = q.shape
    hblk = pl.BlockSpec((None, tq, d), lambda h, i: (h, i, 0))
    head = pl.BlockSpec((None, seq, d), lambda h, i: (h, 0, 0))
    tmat = _suffix_sum_matrix()
    return pl.pallas_call(
        functools.partial(_attn_prompt_kernel, tq=tq),
        grid=(nh, seq // tq),
        in_specs=[hblk, head, head, pl.BlockSpec((tq, d), lambda h, i: (i, h)),
                  pl.BlockSpec(tmat.shape, lambda h, i: (0, 0))],
        out_specs=pl.BlockSpec((tq, d), lambda h, i: (i, h)),
        out_shape=jax.ShapeDtypeStruct((seq, nh * d), BF16),
        scratch_shapes=[
            pltpu.VMEM((seq, d), BF16), pltpu.VMEM((seq, d), BF16),
            pltpu.VMEM((tq, d), F32), pltpu.VMEM((tq, C_TK), F32),
            pltpu.VMEM((2, tq, tq), F32), pltpu.VMEM((2, tq, tq), BF16),
        ],
        compiler_params=_cparams(("arbitrary", "arbitrary"), 56),
        name="attn_prompt",
    )(q, k, v, zg, tmat)


def _attn_sample_kernel(q_ref, kn_ref, vn_ref, zg_ref, kc_ref, vc_ref, tmat_ref, o_ref, ko_ref, vo_ref, *, hb, past):
    t = q_ref.shape[0]
    d = C_HEAD_DIM
    tmat = tmat_ref[...]
    qpos = lax.broadcasted_iota(jnp.int32, (t, C_TK), 0)
    kpos = lax.broadcasted_iota(jnp.int32, (t, C_TK), 1)
    vis_new = kpos < qpos
    pad = jnp.zeros((C_TK - t, d), BF16)
    for h in range(hb):
        sl = slice(h * d, (h + 1) * d)
        kn = kn_ref[:, sl]
        vn = vn_ref[:, sl]
        ko_ref[h] = kn
        vo_ref[h] = vn
        q = (q_ref[:, sl] * C_QSCALE).astype(BF16)
        carry = jnp.zeros((t, C_TK), F32)
        acc, carry = _sb_wide(q, jnp.concatenate([kn.astype(BF16), pad], axis=0),
                              jnp.concatenate([vn.astype(BF16), pad], axis=0), carry, tmat, vis_new)
        pv, carry = _sb_wide(q, kc_ref[h].astype(BF16), vc_ref[h].astype(BF16), carry, tmat, None)
        acc = acc + pv
        o_ref[:, sl] = (acc * _silu(zg_ref[:, sl])).astype(o_ref.dtype)


def attn_sample(p, cache_k, cache_v, *, hb):
    bsz, nh, past, d = cache_k.shape
    t = p.shape[0] // bsz
    w = nh * d
    ng = nh // hb
    blk = lambda c: pl.BlockSpec((t, hb * d), lambda b, g, c=c: (b, c * ng + g))
    cspec = pl.BlockSpec((None, hb, past, d), lambda b, g: (b, g, 0, 0))
    nspec = pl.BlockSpec((None, hb, t, d), lambda b, g: (b, g, 0, 0))
    tmat = _suffix_sum_matrix()
    return pl.pallas_call(
        functools.partial(_attn_sample_kernel, hb=hb, past=past),
        grid=(bsz, ng),
        in_specs=[blk(0), blk(1), blk(2), blk(3), cspec, cspec, pl.BlockSpec(tmat.shape, lambda b, g: (0, 0))],
        out_specs=[pl.BlockSpec((t, hb * d), lambda b, g: (b, g)), nspec, nspec],
        out_shape=[
            jax.ShapeDtypeStruct((bsz * t, w), BF16),
            jax.ShapeDtypeStruct((bsz, nh, t, d), F32),
            jax.ShapeDtypeStruct((bsz, nh, t, d), F32),
        ],
        compiler_params=_cparams(("parallel", "parallel"), 40),
        name="attn_sample",
    )(p, p, p, p, cache_k, cache_v, tmat)


SSD_L = 128
SSD_HG = 32
SSD_GP = B_HEADS // B_GROUPS // 2
SSD_NW = B_HEADS * B_HEAD_DIM // B_GROUPS
SSD_PAIRS = SSD_HG // 2
SSD_W = SSD_HG * B_HEAD_DIM
SSD_GB = SSD_HG // (B_HEADS // B_GROUPS) * B_STATE
HALO = 8


def _split3(x):
    hi = x.astype(BF16)
    r = x - hi.astype(F32)
    mid = r.astype(BF16)
    lo = (r - mid.astype(F32)).astype(BF16)
    return hi, mid, lo


def _dot_x01(x, m):
    hi, mid, lo = _split3(x)
    d = lambda a: jnp.dot(a, m, preferred_element_type=F32)
    return (d(hi) + d(mid)) + d(lo)


def _dot_01x(m, x):
    hi, mid, lo = _split3(x)
    d = lambda a: jnp.dot(m, a, preferred_element_type=F32)
    return (d(hi) + d(mid)) + d(lo)


def _softplus(x):
    return jnp.maximum(x, 0.0) + jnp.log1p(jnp.exp(-jnp.abs(x)))


def _ssd_kernel(*refs, nseq, sample):
    (z_ref, x_ref, b_ref, c_ref, dt_ref, cwx_ref, cwb_ref, cwc_ref, cbx_ref, cbb_ref, cbc_ref,
     dtb_ref, alog_ref, dl_ref, nw_ref, tri_ref, same_ref, e64_ref, el_ref) = refs[:19]
    refs = refs[19:]
    if sample:
        hx_ref, hb_ref, hc_ref, s0_ref = refs[:4]
        refs = refs[4:]
    (y_ref, so_ref, extx, extb, extc, st_scr, xs_scr, bs_scr, cs_scr, ecx_scr, wex_scr, dcx_scr,
     cc_scr, cumt_scr, dtt_scr, yacc_scr) = refs
    L = SSD_L
    lseq = L // nseq
    c = pl.program_id(1)
    nchunks = pl.num_programs(1)

    if sample:
        for s in range(nseq):
            extx[s, HALO - 3:HALO, :] = hx_ref[s]
            extb[s, HALO - 3:HALO, :] = hb_ref[s]
            extc[s, HALO - 3:HALO, :] = hc_ref[s]
            for pp in range(SSD_PAIRS):
                st_scr[s, pp] = s0_ref[s, pp].T
    else:
        @pl.when(c == 0)
        def _():
            extx[0, 0:HALO, :] = jnp.zeros((HALO, SSD_W), F32)
            extb[0, 0:HALO, :] = jnp.zeros((HALO, SSD_GB), F32)
            extc[0, 0:HALO, :] = jnp.zeros((HALO, SSD_GB), F32)
            st_scr[...] = jnp.zeros_like(st_scr)

    def conv(ext, src_ref, cw_ref, cb_ref, dst):
        for s in range(nseq):
            ext[s, HALO:HALO + lseq, :] = src_ref[s * lseq:(s + 1) * lseq, :]
        for s in range(nseq):
            acc = cb_ref[...]
            for k in range(B_CONV):
                acc = acc + ext[s, HALO - 3 + k:HALO - 3 + k + lseq, :] * cw_ref[k:k + 1, :]
            dst[s * lseq:(s + 1) * lseq, :] = _silu(acc).astype(dst.dtype)
        if not sample:
            ext[0, 0:HALO, :] = ext[0, lseq:lseq + HALO, :]

    conv(extx, x_ref, cwx_ref, cbx_ref, xs_scr)
    conv(extb, b_ref, cwb_ref, cbb_ref, bs_scr)
    conv(extc, c_ref, cwc_ref, cbc_ref, cs_scr)
    yacc_scr[...] = _silu(z_ref[...])

    dt = _softplus(dt_ref[...] + dtb_ref[...])
    dta = dt * (-jnp.exp(alog_ref[...]))
    cum = _dot_01x(tri_ref[...], dta)
    if nseq == 1:
        ctot = jnp.broadcast_to(cum[L - 1:L, :], (L, LANES))
    else:
        ctot = _dot_01x(same_ref[...], dta)
    cumt_scr[...] = cum.T
    dtt_scr[...] = dt.T
    e64 = e64_ref[...]
    expand = lambda a: jnp.dot(a.astype(BF16), e64, preferred_element_type=F32)
    ecx_scr[...] = expand(jnp.exp(cum))
    wex_scr[...] = expand(jnp.exp(ctot - cum) * dt)
    drows = HALO if nseq == 1 else L
    dcx_scr[0:drows, :] = _dot_x01(jnp.exp(ctot[0:drows]), e64)
    cc_scr[...] = _dot_x01(cum, el_ref[...])
    mask = tri_ref[...] > 0
    lane = lax.broadcasted_iota(jnp.int32, (L, LANES), 1)
    first = lane < B_HEAD_DIM

    for g2 in range(SSD_GB // B_STATE):
        bg = bs_scr[:, g2 * B_STATE:(g2 + 1) * B_STATE]
        cg = cs_scr[:, g2 * B_STATE:(g2 + 1) * B_STATE]
        cb = lax.dot_general(cg, bg, (((1,), (1,)), ((), ())), preferred_element_type=F32)
        for p in range(SSD_GP):
            pp = g2 * SSD_GP + p
            lanes = slice(pp * LANES, (pp + 1) * LANES)
            ms = []
            for r in (2 * pp, 2 * pp + 1):
                seg = cc_scr[:, r * L:(r + 1) * L] - cumt_scr[r:r + 1, :]
                ms.append((cb * jnp.exp(jnp.where(mask, seg, -jnp.inf)) * dtt_scr[r:r + 1, :]).astype(BF16))
            xp = xs_scr[:, lanes]
            xa = jnp.where(first, xp, 0.0).astype(BF16)
            xb = jnp.where(first, 0.0, xp).astype(BF16)
            y = jnp.dot(jnp.concatenate(ms, axis=1), jnp.concatenate([xa, xb], axis=0), preferred_element_type=F32)
            xw = (xp * wex_scr[:, lanes]).astype(BF16)
            ys = []
            for s in range(nseq):
                rows = slice(s * lseq, (s + 1) * lseq)
                st = st_scr[s, pp]
                ys.append(jnp.dot(cg[rows], st.astype(BF16), preferred_element_type=F32))
                upd = lax.dot_general(bg[rows], xw[rows], (((0,), (0,)), ((), ())), preferred_element_type=F32)
                st_scr[s, pp] = st * dcx_scr[s * lseq:s * lseq + 1, lanes] + upd
            ystate = ys[0] if nseq == 1 else jnp.concatenate(ys, axis=0)
            y = y + ystate * ecx_scr[:, lanes] + dl_ref[:, lanes] * xp
            yacc_scr[:, lanes] = y * yacc_scr[:, lanes]

    gw = SSD_NW
    for g2 in range(SSD_W // gw):
        sl = slice(g2 * gw, (g2 + 1) * gw)
        yg = yacc_scr[:, sl]
        ms_ = jnp.mean(yg * yg, axis=-1, keepdims=True)
        y_ref[:, sl] = (yg * lax.rsqrt(ms_ + NORM_EPS) * nw_ref[:, sl]).astype(y_ref.dtype)

    if sample:
        for s in range(nseq):
            for pp in range(SSD_PAIRS):
                so_ref[s, pp] = st_scr[s, pp].T
    else:
        @pl.when(c == nchunks - 1)
        def _():
            for pp in range(SSD_PAIRS):
                so_ref[0, pp] = st_scr[0, pp].T


def _ssd_consts(lseq):
    L = SSD_L
    t = jnp.arange(L)
    same = (t[:, None] // lseq) == (t[None, :] // lseq)
    tri = same & (t[None, :] <= t[:, None])
    r = jnp.arange(LANES)
    e64 = (r[:, None] == (jnp.arange(SSD_W)[None, :] // B_HEAD_DIM)) & (r[:, None] < SSD_HG)
    el = (r[:, None] == (jnp.arange(SSD_HG * L)[None, :] // L)) & (r[:, None] < SSD_HG)
    return tri.astype(BF16), same.astype(BF16), e64.astype(BF16), el.astype(BF16)


def ssd_core(pz, dtp, conv_w, conv_b, dtb, alog, dlane, nw, *, lseq, conv_state=None, ssm_state=None):
    m = pz.shape[0]
    sample = conv_state is not None
    nseq = SSD_L // lseq
    nchunks = m // SSD_L
    nslices = B_HEADS // SSD_HG
    tri, same, e64, el = _ssd_consts(lseq)
    nx = 4096 // SSD_W
    rowblk = lambda w, off: pl.BlockSpec((SSD_L, w), lambda j, c, off=off: (c, off + j))
    parblk = lambda r, w, off: pl.BlockSpec((r, w), lambda j, c, off=off: (0, off + j))
    const = lambda a: pl.BlockSpec(a.shape, lambda j, c: (0, 0))
    ob = 8192 // SSD_GB
    oc = 9216 // SSD_GB
    in_specs = [
        rowblk(SSD_W, 0), rowblk(SSD_W, nx), rowblk(SSD_GB, ob), rowblk(SSD_GB, oc), rowblk(LANES, 0),
        parblk(B_CONV, SSD_W, 0), parblk(B_CONV, SSD_GB, 4096 // SSD_GB), parblk(B_CONV, SSD_GB, 5120 // SSD_GB),
        parblk(1, SSD_W, 0), parblk(1, SSD_GB, 4096 // SSD_GB), parblk(1, SSD_GB, 5120 // SSD_GB),
        parblk(1, LANES, 0), parblk(1, LANES, 0), parblk(1, SSD_W, 0), parblk(1, SSD_W, 0),
        const(tri), const(same), const(e64), const(el),
    ]
    args = [pz, pz, pz, pz, dtp, conv_w, conv_w, conv_w, conv_b, conv_b, conv_b, dtb, alog, dlane, nw, tri, same, e64, el]
    if sample:
        hblk = lambda w, off: pl.BlockSpec((nseq, B_CONV - 1, w), lambda j, c, off=off: (c, 0, off + j))
        in_specs += [hblk(SSD_W, 0), hblk(SSD_GB, 4096 // SSD_GB), hblk(SSD_GB, 5120 // SSD_GB),
                     pl.BlockSpec((nseq, SSD_PAIRS, LANES, B_STATE), lambda j, c: (c, j, 0, 0),
                                  pipeline_mode=pl.Buffered(1))]
        args += [conv_state, conv_state, conv_state, ssm_state]
        nstates = m // lseq
        so_spec = pl.BlockSpec((nseq, SSD_PAIRS, LANES, B_STATE), lambda j, c: (c, j, 0, 0))
    else:
        nstates = 1
        so_spec = pl.BlockSpec((1, SSD_PAIRS, LANES, B_STATE), lambda j, c: (0, j, 0, 0))
    L = SSD_L
    scratch = [
        pltpu.VMEM((nseq, HALO + lseq, SSD_W), F32), pltpu.VMEM((nseq, HALO + lseq, SSD_GB), F32),
        pltpu.VMEM((nseq, HALO + lseq, SSD_GB), F32),
        pltpu.VMEM((nseq, SSD_PAIRS, B_STATE, LANES), F32),
        pltpu.VMEM((L, SSD_W), F32), pltpu.VMEM((L, SSD_GB), BF16), pltpu.VMEM((L, SSD_GB), BF16),
        pltpu.VMEM((L, SSD_W), F32), pltpu.VMEM((L, SSD_W), F32), pltpu.VMEM((L, SSD_W), F32),
        pltpu.VMEM((L, SSD_HG * L), F32), pltpu.VMEM((LANES, L), F32), pltpu.VMEM((LANES, L), F32),
        pltpu.VMEM((L, SSD_W), F32),
    ]
    return pl.pallas_call(
        functools.partial(_ssd_kernel, nseq=nseq, sample=sample),
        grid=(nslices, nchunks),
        in_specs=in_specs,
        out_specs=[pl.BlockSpec((SSD_L, SSD_W), lambda j, c: (c, j)), so_spec],
        out_shape=[jax.ShapeDtypeStruct((m, B_HEADS * B_HEAD_DIM), BF16),
                   jax.ShapeDtypeStruct((nstates, B_HEADS // 2, LANES, B_STATE), F32)],
        scratch_shapes=scratch,
        compiler_params=_cparams(("arbitrary", "arbitrary"), 60),
        name="ssd_core",
    )(*args)


def _group_heads(v, fill=0.0):
    lead = v.shape[:-1]
    g = v.reshape(*lead, B_HEADS // SSD_HG, SSD_HG)
    g = jnp.pad(g, [(0, 0)] * (len(lead) + 1) + [(0, LANES - SSD_HG)], constant_values=fill)
    return g.reshape(*lead, (B_HEADS // SSD_HG) * LANES)


N_MIXERS = 3
PROJ_TM = 1024
PROJ_TN = 1024
OUT_TM = 512


def _run_stream(x, lseq, tm, par, state_conv=None, state_ssm=None, cache_k=None, cache_v=None):
    sample = state_conv is not None
    depth = par["norm_w"].shape[0]
    new = {"v": [], "ssm": [], "conv": [], "k": [], "v_attn": []}
    h = rms_norm_bf16(x, par["norm_w"][0], tm=tm)
    for i in range(depth):
        kind, j = i % N_MIXERS, i // N_MIXERS
        if kind == 0:
            p = matmul(h, par["a_w_in"], j, tm=tm, tn=PROJ_TN, out_dtype=BF16, name="a_in_proj")
            wpos, bias = _gmlp_pos_params(par["a_w_s"][j], par["a_b_s"][j], min(lseq, A_CHUNK))
            r = gmlp_core(p, wpos, bias, par["a_ln_g"][j], par["a_ln_b"][j], emit_v=sample)
            if sample:
                y, v = r
                new["v"].append(v)
            else:
                y = r
            w_out = par["a_w_out"]
        elif kind == 1:
            pz = matmul(h, par["b_w_in"], j, tm=tm, tn=PROJ_TN, n=par["b_zx"], name="b_in_proj")
            dtp = matmul(h, par["b_w_dt"], j, tm=tm, tn=par["b_w_dt"].shape[2], name="b_dt_proj")
            y, s = ssd_core(
                pz, dtp, par["b_conv_w"][j], par["b_conv_b"][j], par["b_dt_bias"][j], par["b_a_log"][j],
                par["b_d_lane"][j], par["b_norm_w"][j], lseq=min(lseq, SSD_L),
                conv_state=state_conv[j] if sample else None,
                ssm_state=state_ssm[j].reshape(-1, B_HEADS // 2, LANES, B_STATE) if sample else None)
            new["ssm"].append(s.reshape(-1, B_HEADS, B_HEAD_DIM, B_STATE))
            new["conv"].append(pz.reshape(-1, lseq, pz.shape[1])[:, lseq - (B_CONV - 1):, 4096:])
            w_out = par["b_w_out"]
        else:
            if sample:
                p = matmul(h, par["c_w_in"], j, tm=tm, tn=PROJ_TN, name="c_in_proj")
                y, k, v = attn_sample(p, cache_k[j], cache_v[j], hb=8)
            else:
                cw = par["c_w_in"].shape[2] // 4
                part = lambda c, hm: matmul(h, par["c_w_in"], j, tm=tm, tn=PROJ_TN, n=cw, col0=c * cw,
                                            head_major=hm, name="c_in_proj")
                q, k, v, zg = part(0, True), part(1, True), part(2, True), part(3, False)
                y = attn_prompt(q, k, v, zg, tq=min(C_TQ, x.shape[0]))
                k, v = k[None], v[None]
            new["k"].append(k)
            new["v_attn"].append(v)
            w_out = par["c_w_out"]
        last = i == depth - 1
        nw = par["final_norm_w"] if last else par["norm_w"][i + 1]
        r = matmul_residual_norm(y, w_out, j, x, nw, tm=OUT_TM, final=last)
        if last:
            return r, new
        x, h = r


def kernel(x_prompt, x_sample, state_ssm, state_conv, cache_k, cache_v, norm_w, final_norm_w, a_w_in, a_ln_g, a_ln_b, a_w_s, a_b_s, a_w_out, b_w_in, b_conv_w, b_conv_b, b_dt_bias, b_a_log, b_d, b_norm_w, b_w_out, c_w_in, c_w_out):
    bp, seq, d = x_prompt.shape
    bs, dseq, _ = x_sample.shape
    nb = b_w_in.shape[0]
    zx = b_w_in.shape[2] - B_HEADS
    par = {
        "norm_w": norm_w, "final_norm_w": final_norm_w,
        "a_w_in": a_w_in, "a_ln_g": a_ln_g, "a_ln_b": a_ln_b, "a_w_s": a_w_s, "a_b_s": a_b_s,
        "a_w_out": a_w_out.astype(BF16),
        "b_w_in": b_w_in, "b_zx": zx, "b_w_dt": _group_heads(b_w_in[:, :, zx:]),
        "b_conv_w": b_conv_w, "b_conv_b": b_conv_b.reshape(nb, 1, -1),
        "b_dt_bias": _group_heads(b_dt_bias).reshape(nb, 1, -1), "b_a_log": _group_heads(b_a_log).reshape(nb, 1, -1),
        "b_d_lane": jnp.repeat(b_d, B_HEAD_DIM, axis=-1).reshape(nb, 1, -1), "b_norm_w": b_norm_w.reshape(nb, 1, -1),
        "b_w_out": b_w_out.astype(BF16), "c_w_in": c_w_in, "c_w_out": c_w_out.astype(BF16),
    }
    assert bp == 1, "the prompt group is one stream"
    yp, newp = _run_stream(x_prompt.reshape(seq, d), seq, min(PROJ_TM, seq), par)
    ys, news = _run_stream(x_sample.reshape(bs * dseq, d), dseq, min(PROJ_TM, bs * dseq), par,
                           state_conv=state_conv, state_ssm=state_ssm, cache_k=cache_k, cache_v=cache_v)
    st = jnp.stack
    return (
        yp.reshape(bp, seq, d),
        ys.reshape(bs, dseq, d),
        st([v.reshape(bs, dseq, -1) for v in news["v"]]),
        st(newp["ssm"]), st(newp["conv"]), st(news["ssm"]), st(news["conv"]),
        st(newp["k"]), st(newp["v_attn"]), st(news["k"]), st(news["v_attn"]),
    )
```

```python
import functools

import jax
import jax.numpy as jnp
from jax import lax
from jax.experimental import pallas as pl
from jax.experimental.pallas import tpu as pltpu

F32 = jnp.float32
BF16 = jnp.bfloat16

NORM_EPS = 1e-6
D_MODEL = 2048
CHUNK = 64
A_GROUPS = 16
A_CHUNK = 128
B_HEADS = 64
B_HEAD_DIM = 64
B_GROUPS = 8
B_STATE = 128
B_CONV = 4
C_HEADS = 16
C_HEAD_DIM = 128

LANES = 128
MIB = 1024 * 1024


def _cparams(sem, vmem_mib):
    return pltpu.CompilerParams(dimension_semantics=sem, vmem_limit_bytes=vmem_mib * MIB)


def _rms_kernel(x_ref, w_ref, h_ref):
    x = x_ref[...]
    ms = jnp.mean(x * x, axis=-1, keepdims=True)
    h_ref[...] = (x * lax.rsqrt(ms + NORM_EPS) * w_ref[...]).astype(h_ref.dtype)


def rms_norm_bf16(x, w, *, tm):
    m, d = x.shape
    return pl.pallas_call(
        _rms_kernel,
        grid=(m // tm,),
        in_specs=[pl.BlockSpec((tm, d), lambda i: (i, 0)), pl.BlockSpec((1, d), lambda i: (0, 0))],
        out_specs=pl.BlockSpec((tm, d), lambda i: (i, 0)),
        out_shape=jax.ShapeDtypeStruct((m, d), BF16),
        compiler_params=_cparams(("parallel",), 32),
        name="rms_norm",
    )(x, w.reshape(1, d))


def _mm_kernel(x_ref, w_ref, o_ref, wb_scr, *, head_major):
    @pl.when(pl.program_id(1) == 0)
    def _():
        wb_scr[...] = w_ref[...].astype(BF16)

    acc = jnp.dot(x_ref[...], wb_scr[...], preferred_element_type=F32)
    if head_major:
        for h in range(o_ref.shape[0]):
            o_ref[h] = acc[:, h * LANES:(h + 1) * LANES].astype(o_ref.dtype)
    else:
        o_ref[...] = acc.astype(o_ref.dtype)


def matmul(x, w, layer, *, tm, tn, n=None, col0=0, head_major=False, out_dtype=F32, name="proj"):
    m, k = x.shape
    n = w.shape[2] if n is None else n
    c0 = col0 // tn
    if head_major:
        hb = tn // LANES
        out_specs = pl.BlockSpec((hb, tm, LANES), lambda j, i: (j, i, 0))
        out_shape = jax.ShapeDtypeStruct((n // LANES, m, LANES), out_dtype)
    else:
        out_specs = pl.BlockSpec((tm, tn), lambda j, i: (i, j))
        out_shape = jax.ShapeDtypeStruct((m, n), out_dtype)
    return pl.pallas_call(
        functools.partial(_mm_kernel, head_major=head_major),
        grid=(n // tn, m // tm),
        in_specs=[pl.BlockSpec((tm, k), lambda j, i: (i, 0)),
                  pl.BlockSpec((None, k, tn), lambda j, i: (layer, 0, c0 + j))],
        out_specs=out_specs,
        out_shape=out_shape,
        scratch_shapes=[pltpu.VMEM((k, tn), BF16)],
        compiler_params=_cparams(("arbitrary", "arbitrary"), 48),
        name=name,
    )(x, w)


def _mm_res_norm_kernel(y_ref, w_ref, x_ref, nw_ref, *out_refs, final):
    xn = x_ref[...] + jnp.dot(y_ref[...], w_ref[...], preferred_element_type=F32)
    ms = jnp.mean(xn * xn, axis=-1, keepdims=True)
    h = xn * lax.rsqrt(ms + NORM_EPS) * nw_ref[...]
    if final:
        out_refs[0][...] = h
    else:
        out_refs[0][...] = xn
        out_refs[1][...] = h.astype(BF16)


def matmul_residual_norm(y, w, layer, x, nw, *, tm, final=False, name="out_proj"):
    m, k = y.shape
    d = w.shape[2]
    row = lambda i: (i, 0)
    if final:
        out_shape = jax.ShapeDtypeStruct((m, d), F32)
        out_specs = pl.BlockSpec((tm, d), row)
    else:
        out_shape = (jax.ShapeDtypeStruct((m, d), F32), jax.ShapeDtypeStruct((m, d), BF16))
        out_specs = (pl.BlockSpec((tm, d), row), pl.BlockSpec((tm, d), row))
    return pl.pallas_call(
        functools.partial(_mm_res_norm_kernel, final=final),
        grid=(m // tm,),
        in_specs=[
            pl.BlockSpec((tm, k), row),
            pl.BlockSpec((None, k, d), lambda i: (layer, 0, 0), pipeline_mode=pl.Buffered(1)),
            pl.BlockSpec((tm, d), row),
            pl.BlockSpec((1, d), lambda i: (0, 0)),
        ],
        out_specs=out_specs,
        out_shape=out_shape,
        compiler_params=_cparams(("parallel",), 56),
        name=name,
    )(y, w, x, nw.reshape(1, d))


def _gelu2(x):
    return x * (1.0 + lax.erf(x * (2.0 ** -0.5)))


def _silu(x):
    return x * (1.0 / (1.0 + jnp.exp2(x * (-LOG2E))))


A_GW = 256


def _gmlp_kernel(u_ref, v_ref, z_ref, wpos_ref, bias_ref, g_ref, b_ref, *refs, emit_v):
    if emit_v:
        y_ref, vout_ref, gv_scr, vn_scr = refs
    else:
        y_ref, gv_scr, vn_scr = refs
    rows, width = gv_scr.shape
    ngroups = width // A_GW
    acc = jnp.zeros((rows, LANES), F32)
    for g in range(ngroups):
        sl = slice(g * A_GW, (g + 1) * A_GW)
        gv = _gelu2(v_ref[:, sl].astype(F32))
        gv_scr[:, sl] = gv
        acc = acc + gv[:, :LANES] + gv[:, LANES:]
    mean = jnp.sum(acc, axis=-1, keepdims=True) * (1.0 / width)
    acc = jnp.zeros((rows, LANES), F32)
    for g in range(ngroups):
        sl = slice(g * A_GW, (g + 1) * A_GW)
        vc = gv_scr[:, sl] - mean
        sq = vc * vc
        acc = acc + sq[:, :LANES] + sq[:, LANES:]
    var = jnp.sum(acc, axis=-1, keepdims=True) * (1.0 / width)
    rstd = lax.rsqrt(var + 4.0 * NORM_EPS)
    for g in range(ngroups):
        sl = slice(g * A_GW, (g + 1) * A_GW)
        vn = (gv_scr[:, sl] - mean) * rstd * g_ref[:, sl] + b_ref[:, sl]
        if emit_v:
            vout_ref[:, sl] = vn
        vn_scr[:, sl] = vn.astype(BF16)
    for g in range(ngroups):
        sl = slice(g * A_GW, (g + 1) * A_GW)
        s = jnp.dot(wpos_ref[g], vn_scr[:, sl], preferred_element_type=F32) + bias_ref[:, sl]
        y_ref[:, sl] = (_gelu2(u_ref[:, sl].astype(F32)) * s * _silu(z_ref[:, sl].astype(F32))).astype(BF16)


def gmlp_core(p, wpos, bias, ln_g, ln_b, *, emit_v):
    m, w3 = p.shape
    w = w3 // 3
    t = A_CHUNK
    col = lambda c: pl.BlockSpec((t, w), lambda i, c=c: (i, c))
    const2 = lambda shape: pl.BlockSpec(shape, lambda i: (0, 0))
    out_shape = [jax.ShapeDtypeStruct((m, w), BF16)]
    out_specs = [pl.BlockSpec((t, w), lambda i: (i, 0))]
    if emit_v:
        out_shape.append(jax.ShapeDtypeStruct((m, w), F32))
        out_specs.append(pl.BlockSpec((t, w), lambda i: (i, 0)))
    res = pl.pallas_call(
        functools.partial(_gmlp_kernel, emit_v=emit_v),
        grid=(m // t,),
        in_specs=[
            col(0), col(1), col(2),
            pl.BlockSpec(wpos.shape, lambda i: (0, 0, 0)),
            const2((t, w)), const2((1, w)), const2((1, w)),
        ],
        out_specs=out_specs,
        out_shape=out_shape,
        scratch_shapes=[pltpu.VMEM((t, w), F32), pltpu.VMEM((t, w), BF16)],
        compiler_params=_cparams(("parallel",), 40),
        name="gmlp_core",
    )(p, p, p, wpos, bias, ln_g.reshape(1, w), ln_b.reshape(1, w))
    return res if emit_v else res[0]


def _gmlp_pos_params(w_s, b_s, lc):
    pos = jnp.arange(lc)
    mask = (pos[None, :] // CHUNK) <= (pos[:, None] // CHUNK)
    wp = jnp.where(mask[None], w_s[:, :lc, :lc], 0.0)
    reps = A_CHUNK // lc
    if reps > 1:
        eye = jnp.eye(reps, dtype=wp.dtype)
        wp = jnp.einsum("ab,gts->gatbs", eye, wp).reshape(w_s.shape[0], A_CHUNK, A_CHUNK)
    bias_t = jnp.tile(b_s[:, :lc].T, (reps, 1))
    bias = jnp.repeat(bias_t, A_GW, axis=1)
    return (0.5 * wp).astype(BF16), (0.5 * bias).astype(F32)


C_TK = 128
C_SW = 256
C_TQ = 512


def _suffix_sum_matrix():
    j = jnp.arange(C_SW)
    return (j[:, None] >= j[None, :]).astype(BF16)


LOG2E = 1.4426950408889634
C_QSCALE = (C_HEAD_DIM ** -0.5) * LOG2E
C_ZMAX = 126.0


def _sb_scores(q, kw):
    return lax.dot_general(q, kw, (((1,), (1,)), ((), ())), preferred_element_type=F32)


def _sb_weights(z, carry, tmat, vis):
    t, n = z.shape
    sp = jnp.maximum(z, jnp.log2(1.0 + jnp.exp2(jnp.minimum(z, C_ZMAX))))
    if vis is not None:
        sp = jnp.where(vis, sp, 0.0)
    sp = sp.astype(BF16)
    bounds = list(range(0, n, C_SW)) + [n]
    groups = list(zip(bounds[:-1], bounds[1:]))
    sums = {}
    for lo, hi in reversed(groups):
        sums[lo] = jnp.dot(sp[:, lo:hi], tmat[:hi - lo, :hi - lo], preferred_element_type=F32)
    ws = []
    for lo, hi in reversed(groups):
        s = sums[lo]
        for c0 in range(hi - lo - C_TK, -1, -C_TK):
            ws.append(jnp.exp2(z[:, lo + c0:lo + c0 + C_TK] - s[:, c0:c0 + C_TK] - carry))
        carry = carry + jnp.broadcast_to(s[:, 0:1], (t, C_TK))
    w = ws[0] if len(ws) == 1 else jnp.concatenate(ws[::-1], axis=1)
    if vis is not None:
        w = jnp.where(vis, w, 0.0)
    return w.astype(BF16), carry


def _sb_wide(q, kw, vw, carry, tmat, vis):
    w, carry = _sb_weights(_sb_scores(q, kw), carry, tmat, vis)
    return jnp.dot(w, vw, preferred_element_type=F32), carry


def _attn_prompt_kernel(q_ref, k_ref, v_ref, zg_ref, tmat_ref, o_ref,
                        kb_scr, vb_scr, acc_scr, carry_scr, z_scr, w_scr, *, tq):
    i = pl.program_id(1)

    def rows(j):
        return pl.ds(pl.multiple_of(jnp.maximum(j, 0) * tq, tq), tq)

    @pl.when(i == 0)
    def _():
        kb_scr[...] = k_ref[...].astype(BF16)
        vb_scr[...] = v_ref[...].astype(BF16)

    q = (q_ref[...] * C_QSCALE).astype(BF16)
    tmat = tmat_ref[...]

    def stage(j, cur):
        nxt = 1 - cur
        z_scr[nxt] = _sb_scores(q, kb_scr[rows(j - 1), :])
        w_scr[nxt], carry_scr[...] = _sb_weights(z_scr[cur], carry_scr[...], tmat, None)
        acc_scr[...] += jnp.dot(w_scr[cur], vb_scr[rows(j + 1), :], preferred_element_type=F32)

    def finish(cur):
        acc = acc_scr[...] + jnp.dot(w_scr[cur], vb_scr[rows(0), :], preferred_element_type=F32)
        o_ref[...] = (acc * _silu(zg_ref[...])).astype(o_ref.dtype)

    qpos = lax.broadcasted_iota(jnp.int32, (tq, tq), 0)
    kpos = lax.broadcasted_iota(jnp.int32, (tq, tq), 1)
    w_scr[0], carry_scr[...] = _sb_weights(
        _sb_scores(q, kb_scr[rows(i), :]), jnp.zeros((tq, C_TK), F32), tmat, kpos < qpos)
    z_scr[0] = _sb_scores(q, kb_scr[rows(i - 1), :])
    acc_scr[...] = jnp.zeros_like(acc_scr)

    def body(it, c):
        j = i - 1 - 2 * it
        stage(j, 0)
        stage(j - 1, 1)
        return c

    lax.fori_loop(0, i // 2, body, 0)

    @pl.when(i % 2 == 1)
    def _():
        stage(0, 0)
        finish(1)

    @pl.when(i % 2 == 0)
    def _():
        finish(0)


def attn_prompt(q, k, v, zg, *, tq):
    nh, seq, d = q.shape
    hblk = pl.BlockSpec((None, tq, d), lambda h, i: (h, i, 0))
    head = pl.BlockSpec((None, seq, d), lambda h, i: (h, 0, 0))
    tmat = _suffix_sum_matrix()
    return pl.pallas_call(
        functools.partial(_attn_prompt_kernel, tq=tq),
        grid=(nh, seq // tq),
        in_specs=[hblk, head, head, pl.BlockSpec((tq, d), lambda h, i: (i, h)),
                  pl.BlockSpec(tmat.shape, lambda h, i: (0, 0))],
        out_specs=pl.BlockSpec((tq, d), lambda h, i: (i, h)),
        out_shape=jax.ShapeDtypeStruct((seq, nh * d), BF16),
        scratch_shapes=[
            pltpu.VMEM((seq, d), BF16), pltpu.VMEM((seq, d), BF16),
            pltpu.VMEM((tq, d), F32), pltpu.VMEM((tq, C_TK), F32),
            pltpu.VMEM((2, tq, tq), F32), pltpu.VMEM((2, tq, tq), BF16),
        ],
        compiler_params=_cparams(("arbitrary", "arbitrary"), 56),
        name="attn_prompt",
    )(q, k, v, zg, tmat)


def _attn_sample_kernel(q_ref, kn_ref, vn_ref, zg_ref, kc_ref, vc_ref, tmat_ref, o_ref, ko_ref, vo_ref, *, hb, past):
    t = q_ref.shape[0]
    d = C_HEAD_DIM
    tmat = tmat_ref[...]
    qpos = lax.broadcasted_iota(jnp.int32, (t, C_TK), 0)
    kpos = lax.broadcasted_iota(jnp.int32, (t, C_TK), 1)
    vis_new = kpos < qpos
    pad = jnp.zeros((C_TK - t, d), BF16)
    for h in range(hb):
        sl = slice(h * d, (h + 1) * d)
        kn = kn_ref[:, sl]
        vn = vn_ref[:, sl]
        ko_ref[h] = kn
        vo_ref[h] = vn
        q = (q_ref[:, sl] * C_QSCALE).astype(BF16)
        carry = jnp.zeros((t, C_TK), F32)
        acc, carry = _sb_wide(q, jnp.concatenate([kn.astype(BF16), pad], axis=0),
                              jnp.concatenate([vn.astype(BF16), pad], axis=0), carry, tmat, vis_new)
        pv, carry = _sb_wide(q, kc_ref[h].astype(BF16), vc_ref[h].astype(BF16), carry, tmat, None)
        acc = acc + pv
        o_ref[:, sl] = (acc * _silu(zg_ref[:, sl])).astype(o_ref.dtype)


def attn_sample(p, cache_k, cache_v, *, hb):
    bsz, nh, past, d = cache_k.shape
    t = p.shape[0] // bsz
    w = nh * d
    ng = nh // hb
    blk = lambda c: pl.BlockSpec((t, hb * d), lambda b, g, c=c: (b, c * ng + g))
    cspec = pl.BlockSpec((None, hb, past, d), lambda b, g: (b, g, 0, 0))
    nspec = pl.BlockSpec((None, hb, t, d), lambda b, g: (b, g, 0, 0))
    tmat = _suffix_sum_matrix()
    return pl.pallas_call(
        functools.partial(_attn_sample_kernel, hb=hb, past=past),
        grid=(bsz, ng),
        in_specs=[blk(0), blk(1), blk(2), blk(3), cspec, cspec, pl.BlockSpec(tmat.shape, lambda b, g: (0, 0))],
        out_specs=[pl.BlockSpec((t, hb * d), lambda b, g: (b, g)), nspec, nspec],
        out_shape=[
            jax.ShapeDtypeStruct((bsz * t, w), BF16),
            jax.ShapeDtypeStruct((bsz, nh, t, d), F32),
            jax.ShapeDtypeStruct((bsz, nh, t, d), F32),
        ],
        compiler_params=_cparams(("parallel", "parallel"), 40),
        name="attn_sample",
    )(p, p, p, p, cache_k, cache_v, tmat)


SSD_L = 128
SSD_HG = 32
SSD_GP = B_HEADS // B_GROUPS // 2
SSD_NW = B_HEADS * B_HEAD_DIM // B_GROUPS
SSD_PAIRS = SSD_HG // 2
SSD_W = SSD_HG * B_HEAD_DIM
SSD_GB = SSD_HG // (B_HEADS // B_GROUPS) * B_STATE
HALO = 8


def _split3(x):
    hi = x.astype(BF16)
    r = x - hi.astype(F32)
    mid = r.astype(BF16)
    lo = (r - mid.astype(F32)).astype(BF16)
    return hi, mid, lo


def _dot_x01(x, m):
    hi, mid, lo = _split3(x)
    d = lambda a: jnp.dot(a, m, preferred_element_type=F32)
    return (d(hi) + d(mid)) + d(lo)


def _dot_01x(m, x):
    hi, mid, lo = _split3(x)
    d = lambda a: jnp.dot(m, a, preferred_element_type=F32)
    return (d(hi) + d(mid)) + d(lo)


def _softplus(x):
    return jnp.maximum(x, 0.0) + jnp.log1p(jnp.exp(-jnp.abs(x)))


def _ssd_kernel(*refs, nseq, sample):
    (z_ref, x_ref, b_ref, c_ref, dt_ref, cwx_ref, cwb_ref, cwc_ref, cbx_ref, cbb_ref, cbc_ref,
     dtb_ref, alog_ref, dl_ref, nw_ref, tri_ref, same_ref, e64_ref, el_ref) = refs[:19]
    refs = refs[19:]
    if sample:
        hx_ref, hb_ref, hc_ref, s0_ref = refs[:4]
        refs = refs[4:]
    (y_ref, so_ref, extx, extb, extc, st_scr, xs_scr, bs_scr, cs_scr, ecx_scr, wex_scr, dcx_scr,
     cc_scr, cumt_scr, dtt_scr, yacc_scr) = refs
    L = SSD_L
    lseq = L // nseq
    c = pl.program_id(1)
    nchunks = pl.num_programs(1)

    if sample:
        for s in range(nseq):
            extx[s, HALO - 3:HALO, :] = hx_ref[s]
            extb[s, HALO - 3:HALO, :] = hb_ref[s]
            extc[s, HALO - 3:HALO, :] = hc_ref[s]
            for pp in range(SSD_PAIRS):
                st_scr[s, pp] = s0_ref[s, pp].T
    else:
        @pl.when(c == 0)
        def _():
            extx[0, 0:HALO, :] = jnp.zeros((HALO, SSD_W), F32)
            extb[0, 0:HALO, :] = jnp.zeros((HALO, SSD_GB), F32)
            extc[0, 0:HALO, :] = jnp.zeros((HALO, SSD_GB), F32)
            st_scr[...] = jnp.zeros_like(st_scr)

    def conv(ext, src_ref, cw_ref, cb_ref, dst):
        for s in range(nseq):
            ext[s, HALO:HALO + lseq, :] = src_ref[s * lseq:(s + 1) * lseq, :]
        for s in range(nseq):
            acc = cb_ref[...]
            for k in range(B_CONV):
                acc = acc + ext[s, HALO - 3 + k:HALO - 3 + k + lseq, :] * cw_ref[k:k + 1, :]
            dst[s * lseq:(s + 1) * lseq, :] = _silu(acc).astype(dst.dtype)
        if not sample:
            ext[0, 0:HALO, :] = ext[0, lseq:lseq + HALO, :]

    conv(extx, x_ref, cwx_ref, cbx_ref, xs_scr)
    conv(extb, b_ref, cwb_ref, cbb_ref, bs_scr)
    conv(extc, c_ref, cwc_ref, cbc_ref, cs_scr)
    yacc_scr[...] = _silu(z_ref[...])

    dt = _softplus(dt_ref[...] + dtb_ref[...])
    dta = dt * (-jnp.exp(alog_ref[...]))
    cum = _dot_01x(tri_ref[...], dta)
    if nseq == 1:
        ctot = jnp.broadcast_to(cum[L - 1:L, :], (L, LANES))
    else:
        ctot = _dot_01x(same_ref[...], dta)
    cumt_scr[...] = cum.T
    dtt_scr[...] = dt.T
    e64 = e64_ref[...]
    expand = lambda a: jnp.dot(a.astype(BF16), e64, preferred_element_type=F32)
    ecx_scr[...] = expand(jnp.exp(cum))
    wex_scr[...] = expand(jnp.exp(ctot - cum) * dt)
    drows = HALO if nseq == 1 else L
    dcx_scr[0:drows, :] = _dot_x01(jnp.exp(ctot[0:drows]), e64)
    cc_scr[...] = _dot_x01(cum, el_ref[...])
    mask = tri_ref[...] > 0
    lane = lax.broadcasted_iota(jnp.int32, (L, LANES), 1)
    first = lane < B_HEAD_DIM

    for g2 in range(SSD_GB // B_STATE):
        bg = bs_scr[:, g2 * B_STATE:(g2 + 1) * B_STATE]
        cg = cs_scr[:, g2 * B_STATE:(g2 + 1) * B_STATE]
        cb = lax.dot_general(cg, bg, (((1,), (1,)), ((), ())), preferred_element_type=F32)
        for p in range(SSD_GP):
            pp = g2 * SSD_GP + p
            lanes = slice(pp * LANES, (pp + 1) * LANES)
            ms = []
            for r in (2 * pp, 2 * pp + 1):
                seg = cc_scr[:, r * L:(r + 1) * L] - cumt_scr[r:r + 1, :]
                ms.append((cb * jnp.exp(jnp.where(mask, seg, -jnp.inf)) * dtt_scr[r:r + 1, :]).astype(BF16))
            xp = xs_scr[:, lanes]
            xa = jnp.where(first, xp, 0.0).astype(BF16)
            xb = jnp.where(first, 0.0, xp).astype(BF16)
            y = jnp.dot(jnp.concatenate(ms, axis=1), jnp.concatenate([xa, xb], axis=0), preferred_element_type=F32)
            xw = (xp * wex_scr[:, lanes]).astype(BF16)
            ys = []
            for s in range(nseq):
                rows = slice(s * lseq, (s + 1) * lseq)
                st = st_scr[s, pp]
                ys.append(jnp.dot(cg[rows], st.astype(BF16), preferred_element_type=F32))
                upd = lax.dot_general(bg[rows], xw[rows], (((0,), (0,)), ((), ())), preferred_element_type=F32)
                st_scr[s, pp] = st * dcx_scr[s * lseq:s * lseq + 1, lanes] + upd
            ystate = ys[0] if nseq == 1 else jnp.concatenate(ys, axis=0)
            y = y + ystate * ecx_scr[:, lanes] + dl_ref[:, lanes] * xp
            yacc_scr[:, lanes] = y * yacc_scr[:, lanes]

    gw = SSD_NW
    for g2 in range(SSD_W // gw):
        sl = slice(g2 * gw, (g2 + 1) * gw)
        yg = yacc_scr[:, sl]
        ms_ = jnp.mean(yg * yg, axis=-1, keepdims=True)
        y_ref[:, sl] = (yg * lax.rsqrt(ms_ + NORM_EPS) * nw_ref[:, sl]).astype(y_ref.dtype)

    if sample:
        for s in range(nseq):
            for pp in range(SSD_PAIRS):
                so_ref[s, pp] = st_scr[s, pp].T
    else:
        @pl.when(c == nchunks - 1)
        def _():
            for pp in range(SSD_PAIRS):
                so_ref[0, pp] = st_scr[0, pp].T


def _ssd_consts(lseq):
    L = SSD_L
    t = jnp.arange(L)
    same = (t[:, None] // lseq) == (t[None, :] // lseq)
    tri = same & (t[None, :] <= t[:, None])
    r = jnp.arange(LANES)
    e64 = (r[:, None] == (jnp.arange(SSD_W)[None, :] // B_HEAD_DIM)) & (r[:, None] < SSD_HG)
    el = (r[:, None] == (jnp.arange(SSD_HG * L)[None, :] // L)) & (r[:, None] < SSD_HG)
    return tri.astype(BF16), same.astype(BF16), e64.astype(BF16), el.astype(BF16)


def ssd_core(pz, dtp, conv_w, conv_b, dtb, alog, dlane, nw, *, lseq, conv_state=None, ssm_state=None):
    m = pz.shape[0]
    sample = conv_state is not None
    nseq = SSD_L // lseq
    nchunks = m // SSD_L
    nslices = B_HEADS // SSD_HG
    tri, same, e64, el = _ssd_consts(lseq)
    nx = 4096 // SSD_W
    rowblk = lambda w, off: pl.BlockSpec((SSD_L, w), lambda j, c, off=off: (c, off + j))
    parblk = lambda r, w, off: pl.BlockSpec((r, w), lambda j, c, off=off: (0, off + j))
    const = lambda a: pl.BlockSpec(a.shape, lambda j, c: (0, 0))
    ob = 8192 // SSD_GB
    oc = 9216 // SSD_GB
    in_specs = [
        rowblk(SSD_W, 0), rowblk(SSD_W, nx), rowblk(SSD_GB, ob), rowblk(SSD_GB, oc), rowblk(LANES, 0),
        parblk(B_CONV, SSD_W, 0), parblk(B_CONV, SSD_GB, 4096 // SSD_GB), parblk(B_CONV, SSD_GB, 5120 // SSD_GB),
        parblk(1, SSD_W, 0), parblk(1, SSD_GB, 4096 // SSD_GB), parblk(1, SSD_GB, 5120 // SSD_GB),
        parblk(1, LANES, 0), parblk(1, LANES, 0), parblk(1, SSD_W, 0), parblk(1, SSD_W, 0),
        const(tri), const(same), const(e64), const(el),
    ]
    args = [pz, pz, pz, pz, dtp, conv_w, conv_w, conv_w, conv_b, conv_b, conv_b, dtb, alog, dlane, nw, tri, same, e64, el]
    if sample:
        hblk = lambda w, off: pl.BlockSpec((nseq, B_CONV - 1, w), lambda j, c, off=off: (c, 0, off + j))
        in_specs += [hblk(SSD_W, 0), hblk(SSD_GB, 4096 // SSD_GB), hblk(SSD_GB, 5120 // SSD_GB),
                     pl.BlockSpec((nseq, SSD_PAIRS, LANES, B_STATE), lambda j, c: (c, j, 0, 0),
                                  pipeline_mode=pl.Buffered(1))]
        args += [conv_state, conv_state, conv_state, ssm_state]
        nstates = m // lseq
        so_spec = pl.BlockSpec((nseq, SSD_PAIRS, LANES, B_STATE), lambda j, c: (c, j, 0, 0))
    else:
        nstates = 1
        so_spec = pl.BlockSpec((1, SSD_PAIRS, LANES, B_STATE), lambda j, c: (0, j, 0, 0))
    L = SSD_L
    scratch = [
        pltpu.VMEM((nseq, HALO + lseq, SSD_W), F32), pltpu.VMEM((nseq, HALO + lseq, SSD_GB), F32),
        pltpu.VMEM((nseq, HALO + lseq, SSD_GB), F32),
        pltpu.VMEM((nseq, SSD_PAIRS, B_STATE, LANES), F32),
        pltpu.VMEM((L, SSD_W), F32), pltpu.VMEM((L, SSD_GB), BF16), pltpu.VMEM((L, SSD_GB), BF16),
        pltpu.VMEM((L, SSD_W), F32), pltpu.VMEM((L, SSD_W), F32), pltpu.VMEM((L, SSD_W), F32),
        pltpu.VMEM((L, SSD_HG * L), F32), pltpu.VMEM((LANES, L), F32), pltpu.VMEM((LANES, L), F32),
        pltpu.VMEM((L, SSD_W), F32),
    ]
    return pl.pallas_call(
        functools.partial(_ssd_kernel, nseq=nseq, sample=sample),
        grid=(nslices, nchunks),
        in_specs=in_specs,
        out_specs=[pl.BlockSpec((SSD_L, SSD_W), lambda j, c: (c, j)), so_spec],
        out_shape=[jax.ShapeDtypeStruct((m, B_HEADS * B_HEAD_DIM), BF16),
                   jax.ShapeDtypeStruct((nstates, B_HEADS // 2, LANES, B_STATE), F32)],
        scratch_shapes=scratch,
        compiler_params=_cparams(("arbitrary", "arbitrary"), 60),
        name="ssd_core",
    )(*args)


def _group_heads(v, fill=0.0):
    lead = v.shape[:-1]
    g = v.reshape(*lead, B_HEADS // SSD_HG, SSD_HG)
    g = jnp.pad(g, [(0, 0)] * (len(lead) + 1) + [(0, LANES - SSD_HG)], constant_values=fill)
    return g.reshape(*lead, (B_HEADS // SSD_HG) * LANES)


N_MIXERS = 3
PROJ_TM = 1024
PROJ_TN = 1024
OUT_TM = 512


def _run_stream(x, lseq, tm, par, state_conv=None, state_ssm=None, cache_k=None, cache_v=None):
    sample = state_conv is not None
    depth = par["norm_w"].shape[0]
    new = {"v": [], "ssm": [], "conv": [], "k": [], "v_attn": []}
    h = rms_norm_bf16(x, par["norm_w"][0], tm=tm)
    for i in range(depth):
        kind, j = i % N_MIXERS, i // N_MIXERS
        if kind == 0:
            p = matmul(h, par["a_w_in"], j, tm=tm, tn=PROJ_TN, out_dtype=BF16, name="a_in_proj")
            wpos, bias = _gmlp_pos_params(par["a_w_s"][j], par["a_b_s"][j], min(lseq, A_CHUNK))
            r = gmlp_core(p, wpos, bias, par["a_ln_g"][j], par["a_ln_b"][j], emit_v=sample)
            if sample:
                y, v = r
                new["v"].append(v)
            else:
                y = r
            w_out = par["a_w_out"]
        elif kind == 1:
            pz = matmul(h, par["b_w_in"], j, tm=tm, tn=PROJ_TN, n=par["b_zx"], name="b_in_proj")
            dtp = matmul(h, par["b_w_dt"], j, tm=tm, tn=par["b_w_dt"].shape[2], name="b_dt_proj")
            y, s = ssd_core(
                pz, dtp, par["b_conv_w"][j], par["b_conv_b"][j], par["b_dt_bias"][j], par["b_a_log"][j],
                par["b_d_lane"][j], par["b_norm_w"][j], lseq=min(lseq, SSD_L),
                conv_state=state_conv[j] if sample else None,
                ssm_state=state_ssm[j].reshape(-1, B_HEADS // 2, LANES, B_STATE) if sample else None)
            new["ssm"].append(s.reshape(-1, B_HEADS, B_HEAD_DIM, B_STATE))
            new["conv"].append(pz.reshape(-1, lseq, pz.shape[1])[:, lseq - (B_CONV - 1):, 4096:])
            w_out = par["b_w_out"]
        else:
            if sample:
                p = matmul(h, par["c_w_in"], j, tm=tm, tn=PROJ_TN, name="c_in_proj")
                y, k, v = attn_sample(p, cache_k[j], cache_v[j], hb=8)
            else:
                cw = par["c_w_in"].shape[2] // 4
                part = lambda c, hm: matmul(h, par["c_w_in"], j, tm=tm, tn=PROJ_TN, n=cw, col0=c * cw,
                                            head_major=hm, name="c_in_proj")
                q, k, v, zg = part(0, True), part(1, True), part(2, True), part(3, False)
                y = attn_prompt(q, k, v, zg, tq=min(C_TQ, x.shape[0]))
                k, v = k[None], v[None]
            new["k"].append(k)
            new["v_attn"].append(v)
            w_out = par["c_w_out"]
        last = i == depth - 1
        nw = par["final_norm_w"] if last else par["norm_w"][i + 1]
        r = matmul_residual_norm(y, w_out, j, x, nw, tm=OUT_TM, final=last)
        if last:
            return r, new
        x, h = r


def kernel(x_prompt, x_sample, state_ssm, state_conv, cache_k, cache_v, norm_w, final_norm_w, a_w_in, a_ln_g, a_ln_b, a_w_s, a_b_s, a_w_out, b_w_in, b_conv_w, b_conv_b, b_dt_bias, b_a_log, b_d, b_norm_w, b_w_out, c_w_in, c_w_out):
    bp, seq, d = x_prompt.shape
    bs, dseq, _ = x_sample.shape
    nb = b_w_in.shape[0]
    zx = b_w_in.shape[2] - B_HEADS
    par = {
        "norm_w": norm_w, "final_norm_w": final_norm_w,
        "a_w_in": a_w_in, "a_ln_g": a_ln_g, "a_ln_b": a_ln_b, "a_w_s": a_w_s, "a_b_s": a_b_s,
        "a_w_out": a_w_out.astype(BF16),
        "b_w_in": b_w_in, "b_zx": zx, "b_w_dt": _group_heads(b_w_in[:, :, zx:]),
        "b_conv_w": b_conv_w, "b_conv_b": b_conv_b.reshape(nb, 1, -1),
        "b_dt_bias": _group_heads(b_dt_bias).reshape(nb, 1, -1), "b_a_log": _group_heads(b_a_log).reshape(nb, 1, -1),
        "b_d_lane": jnp.repeat(b_d, B_HEAD_DIM, axis=-1).reshape(nb, 1, -1), "b_norm_w": b_norm_w.reshape(nb, 1, -1),
        "b_w_out": b_w_out.astype(BF16), "c_w_in": c_w_in, "c_w_out": c_w_out.astype(BF16),
    }
    assert bp == 1, "the prompt group is one stream"
    yp, newp = _run_stream(x_prompt.reshape(seq, d), seq, min(PROJ_TM, seq), par)
    ys, news = _run_stream(x_sample.reshape(bs * dseq, d), dseq, min(PROJ_TM, bs * dseq), par,
                           state_conv=state_conv, state_ssm=state_ssm, cache_k=cache_k, cache_v=cache_v)
    st = jnp.stack
    return (
        yp.reshape(bp, seq, d),
        ys.reshape(bs, dseq, d),
        st([v.reshape(bs, dseq, -1) for v in news["v"]]),
        st(newp["ssm"]), st(newp["conv"]), st(news["ssm"]), st(news["conv"]),
        st(newp["k"]), st(newp["v_attn"]), st(news["k"]), st(news["v_attn"]),
    )
```

```python
import functools

import jax
import jax.numpy as jnp
from jax import lax
from jax.experimental import pallas as pl
from jax.experimental.pallas import tpu as pltpu

F32 = jnp.float32
BF16 = jnp.bfloat16

NORM_EPS = 1e-6
D_MODEL = 2048
CHUNK = 64
A_GROUPS = 16
A_CHUNK = 128
B_HEADS = 64
B_HEAD_DIM = 64
B_GROUPS = 8
B_STATE = 128
B_CONV = 4
C_HEADS = 16
C_HEAD_DIM = 128

LANES = 128
MIB = 1024 * 1024


def _cparams(sem, vmem_mib):
    return pltpu.CompilerParams(dimension_semantics=sem, vmem_limit_bytes=vmem_mib * MIB)


def _rms_kernel(x_ref, w_ref, h_ref):
    x = x_ref[...]
    ms = jnp.mean(x * x, axis=-1, keepdims=True)
    h_ref[...] = (x * lax.rsqrt(ms + NORM_EPS) * w_ref[...]).astype(h_ref.dtype)


def rms_norm_bf16(x, w, *, tm):
    m, d = x.shape
    return pl.pallas_call(
        _rms_kernel,
        grid=(m // tm,),
        in_specs=[pl.BlockSpec((tm, d), lambda i: (i, 0)), pl.BlockSpec((1, d), lambda i: (0, 0))],
        out_specs=pl.BlockSpec((tm, d), lambda i: (i, 0)),
        out_shape=jax.ShapeDtypeStruct((m, d), BF16),
        compiler_params=_cparams(("parallel",), 32),
        name="rms_norm",
    )(x, w.reshape(1, d))


def _mm_kernel(x_ref, w_ref, o_ref, wb_scr, *, head_major):
    @pl.when(pl.program_id(1) == 0)
    def _():
        wb_scr[...] = w_ref[...].astype(BF16)

    acc = jnp.dot(x_ref[...], wb_scr[...], preferred_element_type=F32)
    if head_major:
        for h in range(o_ref.shape[0]):
            o_ref[h] = acc[:, h * LANES:(h + 1) * LANES].astype(o_ref.dtype)
    else:
        o_ref[...] = acc.astype(o_ref.dtype)


def matmul(x, w, layer, *, tm, tn, n=None, col0=0, head_major=False, out_dtype=F32, name="proj"):
    m, k = x.shape
    n = w.shape[2] if n is None else n
    c0 = col0 // tn
    if head_major:
        hb = tn // LANES
        out_specs = pl.BlockSpec((hb, tm, LANES), lambda j, i: (j, i, 0))
        out_shape = jax.ShapeDtypeStruct((n // LANES, m, LANES), out_dtype)
    else:
        out_specs = pl.BlockSpec((tm, tn), lambda j, i: (i, j))
        out_shape = jax.ShapeDtypeStruct((m, n), out_dtype)
    return pl.pallas_call(
        functools.partial(_mm_kernel, head_major=head_major),
        grid=(n // tn, m // tm),
        in_specs=[pl.BlockSpec((tm, k), lambda j, i: (i, 0)),
                  pl.BlockSpec((None, k, tn), lambda j, i: (layer, 0, c0 + j))],
        out_specs=out_specs,
        out_shape=out_shape,
        scratch_shapes=[pltpu.VMEM((k, tn), BF16)],
        compiler_params=_cparams(("arbitrary", "arbitrary"), 48),
        name=name,
    )(x, w)


def _mm_res_norm_kernel(y_ref, w_ref, x_ref, nw_ref, *out_refs, final):
    xn = x_ref[...] + jnp.dot(y_ref[...], w_ref[...], preferred_element_type=F32)
    ms = jnp.mean(xn * xn, axis=-1, keepdims=True)
    h = xn * lax.rsqrt(ms + NORM_EPS) * nw_ref[...]
    if final:
        out_refs[0][...] = h
    else:
        out_refs[0][...] = xn
        out_refs[1][...] = h.astype(BF16)


def matmul_residual_norm(y, w, layer, x, nw, *, tm, final=False, name="out_proj"):
    m, k = y.shape
    d = w.shape[2]
    row = lambda i: (i, 0)
    if final:
        out_shape = jax.ShapeDtypeStruct((m, d), F32)
        out_specs = pl.BlockSpec((tm, d), row)
    else:
        out_shape = (jax.ShapeDtypeStruct((m, d), F32), jax.ShapeDtypeStruct((m, d), BF16))
        out_specs = (pl.BlockSpec((tm, d), row), pl.BlockSpec((tm, d), row))
    return pl.pallas_call(
        functools.partial(_mm_res_norm_kernel, final=final),
        grid=(m // tm,),
        in_specs=[
            pl.BlockSpec((tm, k), row),
            pl.BlockSpec((None, k, d), lambda i: (layer, 0, 0), pipeline_mode=pl.Buffered(1)),
            pl.BlockSpec((tm, d), row),
            pl.BlockSpec((1, d), lambda i: (0, 0)),
        ],
        out_specs=out_specs,
        out_shape=out_shape,
        compiler_params=_cparams(("parallel",), 56),
        name=name,
    )(y, w, x, nw.reshape(1, d))


def _gelu2(x):
    return x * (1.0 + lax.erf(x * (2.0 ** -0.5)))


def _silu(x):
    return x * (1.0 / (1.0 + jnp.exp(-x)))


A_GW = 256


def _gmlp_kernel(u_ref, v_ref, z_ref, wpos_ref, bias_ref, g_ref, b_ref, *refs, emit_v):
    if emit_v:
        y_ref, vout_ref, gv_scr, vn_scr = refs
    else:
        y_ref, gv_scr, vn_scr = refs
    rows, width = gv_scr.shape
    ngroups = width // A_GW
    acc = jnp.zeros((rows, LANES), F32)
    for g in range(ngroups):
        sl = slice(g * A_GW, (g + 1) * A_GW)
        gv = _gelu2(v_ref[:, sl].astype(F32))
        gv_scr[:, sl] = gv
        acc = acc + gv[:, :LANES] + gv[:, LANES:]
    mean = jnp.sum(acc, axis=-1, keepdims=True) * (1.0 / width)
    acc = jnp.zeros((rows, LANES), F32)
    for g in range(ngroups):
        sl = slice(g * A_GW, (g + 1) * A_GW)
        vc = gv_scr[:, sl] - mean
        sq = vc * vc
        acc = acc + sq[:, :LANES] + sq[:, LANES:]
    var = jnp.sum(acc, axis=-1, keepdims=True) * (1.0 / width)
    rstd = lax.rsqrt(var + 4.0 * NORM_EPS)
    for g in range(ngroups):
        sl = slice(g * A_GW, (g + 1) * A_GW)
        vn = (gv_scr[:, sl] - mean) * rstd * g_ref[:, sl] + b_ref[:, sl]
        if emit_v:
            vout_ref[:, sl] = vn
        vn_scr[:, sl] = vn.astype(BF16)
    for g in range(ngroups):
        sl = slice(g * A_GW, (g + 1) * A_GW)
        s = jnp.dot(wpos_ref[g], vn_scr[:, sl], preferred_element_type=F32) + bias_ref[:, sl]
        y_ref[:, sl] = (_gelu2(u_ref[:, sl].astype(F32)) * s * _silu(z_ref[:, sl].astype(F32))).astype(BF16)


def gmlp_core(p, wpos, bias, ln_g, ln_b, *, emit_v):
    m, w3 = p.shape
    w = w3 // 3
    t = A_CHUNK
    col = lambda c: pl.BlockSpec((t, w), lambda i, c=c: (i, c))
    const2 = lambda shape: pl.BlockSpec(shape, lambda i: (0, 0))
    out_shape = [jax.ShapeDtypeStruct((m, w), BF16)]
    out_specs = [pl.BlockSpec((t, w), lambda i: (i, 0))]
    if emit_v:
        out_shape.append(jax.ShapeDtypeStruct((m, w), F32))
        out_specs.append(pl.BlockSpec((t, w), lambda i: (i, 0)))
    res = pl.pallas_call(
        functools.partial(_gmlp_kernel, emit_v=emit_v),
        grid=(m // t,),
        in_specs=[
            col(0), col(1), col(2),
            pl.BlockSpec(wpos.shape, lambda i: (0, 0, 0)),
            const2((t, w)), const2((1, w)), const2((1, w)),
        ],
        out_specs=out_specs,
        out_shape=out_shape,
        scratch_shapes=[pltpu.VMEM((t, w), F32), pltpu.VMEM((t, w), BF16)],
        compiler_params=_cparams(("parallel",), 40),
        name="gmlp_core",
    )(p, p, p, wpos, bias, ln_g.reshape(1, w), ln_b.reshape(1, w))
    return res if emit_v else res[0]


def _gmlp_pos_params(w_s, b_s, lc):
    pos = jnp.arange(lc)
    mask = (pos[None, :] // CHUNK) <= (pos[:, None] // CHUNK)
    wp = jnp.where(mask[None], w_s[:, :lc, :lc], 0.0)
    reps = A_CHUNK // lc
    if reps > 1:
        eye = jnp.eye(reps, dtype=wp.dtype)
        wp = jnp.einsum("ab,gts->gatbs", eye, wp).reshape(w_s.shape[0], A_CHUNK, A_CHUNK)
    bias_t = jnp.tile(b_s[:, :lc].T, (reps, 1))
    bias = jnp.repeat(bias_t, A_GW, axis=1)
    return (0.5 * wp).astype(BF16), (0.5 * bias).astype(F32)


C_TK = 128
C_SW = 256
C_TQ = 1024


def _suffix_sum_matrix():
    j = jnp.arange(C_SW)
    return (j[:, None] >= j[None, :]).astype(BF16)


LOG2E = 1.4426950408889634
C_QSCALE = (C_HEAD_DIM ** -0.5) * LOG2E
C_ZMAX = 126.0


def _sb_scores(q, kw):
    return lax.dot_general(q, kw, (((1,), (1,)), ((), ())), preferred_element_type=F32)


def _sb_weights(z, carry, tmat, vis):
    t, n = z.shape
    sp = jnp.maximum(z, jnp.log2(1.0 + jnp.exp2(jnp.minimum(z, C_ZMAX))))
    if vis is not None:
        sp = jnp.where(vis, sp, 0.0)
    sp = sp.astype(BF16)
    bounds = list(range(0, n, C_SW)) + [n]
    groups = list(zip(bounds[:-1], bounds[1:]))
    sums = {}
    for lo, hi in reversed(groups):
        sums[lo] = jnp.dot(sp[:, lo:hi], tmat[:hi - lo, :hi - lo], preferred_element_type=F32)
    ws = []
    for lo, hi in reversed(groups):
        s = sums[lo]
        for c0 in range(hi - lo - C_TK, -1, -C_TK):
            ws.append(jnp.exp2(z[:, lo + c0:lo + c0 + C_TK] - s[:, c0:c0 + C_TK] - carry))
        carry = carry + jnp.broadcast_to(s[:, 0:1], (t, C_TK))
    w = ws[0] if len(ws) == 1 else jnp.concatenate(ws[::-1], axis=1)
    if vis is not None:
        w = jnp.where(vis, w, 0.0)
    return w.astype(BF16), carry


def _sb_wide(q, kw, vw, carry, tmat, vis):
    w, carry = _sb_weights(_sb_scores(q, kw), carry, tmat, vis)
    return jnp.dot(w, vw, preferred_element_type=F32), carry


def _attn_prompt_kernel(q_ref, k_ref, v_ref, zg_ref, tmat_ref, o_ref,
                        kb_scr, vb_scr, acc_scr, carry_scr, z_scr, w_scr, *, tq):
    i = pl.program_id(1)
    kr = tq // 2

    def rows(j):
        return pl.ds(pl.multiple_of(jnp.maximum(j, 0) * kr, kr), kr)

    @pl.when(i == 0)
    def _():
        kb_scr[...] = k_ref[...].astype(BF16)
        vb_scr[...] = v_ref[...].astype(BF16)

    q = (q_ref[...] * C_QSCALE).astype(BF16)
    tmat = tmat_ref[...]

    def stage(j, cur, vis=None):
        nxt = 1 - cur
        z_scr[nxt] = _sb_scores(q, kb_scr[rows(j - 1), :])
        w_scr[nxt], carry_scr[...] = _sb_weights(z_scr[cur], carry_scr[...], tmat, vis)
        acc_scr[...] += jnp.dot(w_scr[cur], vb_scr[rows(j + 1), :], preferred_element_type=F32)

    qpos = lax.broadcasted_iota(jnp.int32, (tq, kr), 0)
    kpos = lax.broadcasted_iota(jnp.int32, (tq, kr), 1)
    top = 2 * i + 1
    w_scr[0], carry_scr[...] = _sb_weights(
        _sb_scores(q, kb_scr[rows(top), :]), jnp.zeros((tq, C_TK), F32), tmat, kpos + kr < qpos)
    z_scr[0] = _sb_scores(q, kb_scr[rows(top - 1), :])
    acc_scr[...] = jnp.zeros_like(acc_scr)
    stage(top - 1, 0, kpos < qpos)

    def body(it, c):
        j = 2 * i - 1 - 2 * it
        stage(j, 1)
        stage(j - 1, 0)
        return c

    lax.fori_loop(0, i, body, 0)
    acc = acc_scr[...] + jnp.dot(w_scr[1], vb_scr[rows(0), :], preferred_element_type=F32)
    o_ref[...] = (acc * _silu(zg_ref[...])).astype(o_ref.dtype)


def attn_prompt(q, k, v, zg, *, tq):
    nh, seq, d = q.shape
    hblk = pl.BlockSpec((None, tq, d), lambda h, i: (h, i, 0))
    head = pl.BlockSpec((None, seq, d), lambda h, i: (h, 0, 0))
    tmat = _suffix_sum_matrix()
    return pl.pallas_call(
        functools.partial(_attn_prompt_kernel, tq=tq),
        grid=(nh, seq // tq),
        in_specs=[hblk, head, head, pl.BlockSpec((tq, d), lambda h, i: (i, h)),
                  pl.BlockSpec(tmat.shape, lambda h, i: (0, 0))],
        out_specs=pl.BlockSpec((tq, d), lambda h, i: (i, h)),
        out_shape=jax.ShapeDtypeStruct((seq, nh * d), BF16),
        scratch_shapes=[
            pltpu.VMEM((seq, d), BF16), pltpu.VMEM((seq, d), BF16),
            pltpu.VMEM((tq, d), F32), pltpu.VMEM((tq, C_TK), F32),
            pltpu.VMEM((2, tq, tq // 2), F32), pltpu.VMEM((2, tq, tq // 2), BF16),
        ],
        compiler_params=_cparams(("arbitrary", "arbitrary"), 56),
        name="attn_prompt",
    )(q, k, v, zg, tmat)


def _attn_sample_kernel(q_ref, kn_ref, vn_ref, zg_ref, kc_ref, vc_ref, tmat_ref, o_ref, ko_ref, vo_ref, *, hb, past):
    t = q_ref.shape[0]
    d = C_HEAD_DIM
    tmat = tmat_ref[...]
    qpos = lax.broadcasted_iota(jnp.int32, (t, C_TK), 0)
    kpos = lax.broadcasted_iota(jnp.int32, (t, C_TK), 1)
    vis_new = kpos < qpos
    pad = jnp.zeros((C_TK - t, d), BF16)
    for h in range(hb):
        sl = slice(h * d, (h + 1) * d)
        kn = kn_ref[:, sl]
        vn = vn_ref[:, sl]
        ko_ref[h] = kn
        vo_ref[h] = vn
        q = (q_ref[:, sl] * C_QSCALE).astype(BF16)
        carry = jnp.zeros((t, C_TK), F32)
        acc, carry = _sb_wide(q, jnp.concatenate([kn.astype(BF16), pad], axis=0),
                              jnp.concatenate([vn.astype(BF16), pad], axis=0), carry, tmat, vis_new)
        pv, carry = _sb_wide(q, kc_ref[h].astype(BF16), vc_ref[h].astype(BF16), carry, tmat, None)
        acc = acc + pv
        o_ref[:, sl] = (acc * _silu(zg_ref[:, sl])).astype(o_ref.dtype)


def attn_sample(p, cache_k, cache_v, *, hb):
    bsz, nh, past, d = cache_k.shape
    t = p.shape[0] // bsz
    w = nh * d
    ng = nh // hb
    blk = lambda c: pl.BlockSpec((t, hb * d), lambda b, g, c=c: (b, c * ng + g))
    cspec = pl.BlockSpec((None, hb, past, d), lambda b, g: (b, g, 0, 0))
    nspec = pl.BlockSpec((None, hb, t, d), lambda b, g: (b, g, 0, 0))
    tmat = _suffix_sum_matrix()
    return pl.pallas_call(
        functools.partial(_attn_sample_kernel, hb=hb, past=past),
        grid=(bsz, ng),
        in_specs=[blk(0), blk(1), blk(2), blk(3), cspec, cspec, pl.BlockSpec(tmat.shape, lambda b, g: (0, 0))],
        out_specs=[pl.BlockSpec((t, hb * d), lambda b, g: (b, g)), nspec, nspec],
        out_shape=[
            jax.ShapeDtypeStruct((bsz * t, w), BF16),
            jax.ShapeDtypeStruct((bsz, nh, t, d), F32),
            jax.ShapeDtypeStruct((bsz, nh, t, d), F32),
        ],
        compiler_params=_cparams(("parallel", "parallel"), 40),
        name="attn_sample",
    )(p, p, p, p, cache_k, cache_v, tmat)


SSD_L = 128
SSD_HG = 32
SSD_GP = B_HEADS // B_GROUPS // 2
SSD_NW = B_HEADS * B_HEAD_DIM // B_GROUPS
SSD_PAIRS = SSD_HG // 2
SSD_W = SSD_HG * B_HEAD_DIM
SSD_GB = SSD_HG // (B_HEADS // B_GROUPS) * B_STATE
HALO = 8


def _split3(x):
    hi = x.astype(BF16)
    r = x - hi.astype(F32)
    mid = r.astype(BF16)
    lo = (r - mid.astype(F32)).astype(BF16)
    return hi, mid, lo


def _dot_x01(x, m):
    hi, mid, lo = _split3(x)
    d = lambda a: jnp.dot(a, m, preferred_element_type=F32)
    return (d(hi) + d(mid)) + d(lo)


def _dot_01x(m, x):
    hi, mid, lo = _split3(x)
    d = lambda a: jnp.dot(m, a, preferred_element_type=F32)
    return (d(hi) + d(mid)) + d(lo)


def _softplus(x):
    return jnp.maximum(x, 0.0) + jnp.log1p(jnp.exp(-jnp.abs(x)))


def _ssd_kernel(*refs, nseq, sample):
    (z_ref, x_ref, b_ref, c_ref, dt_ref, cwx_ref, cwb_ref, cwc_ref, cbx_ref, cbb_ref, cbc_ref,
     dtb_ref, alog_ref, dl_ref, nw_ref, tri_ref, same_ref, e64_ref, el_ref) = refs[:19]
    refs = refs[19:]
    if sample:
        hx_ref, hb_ref, hc_ref, s0_ref = refs[:4]
        refs = refs[4:]
    (y_ref, so_ref, extx, extb, extc, st_scr, xs_scr, bs_scr, cs_scr, ecx_scr, wex_scr, dcx_scr,
     cc_scr, cumt_scr, dtt_scr, yacc_scr) = refs
    L = SSD_L
    lseq = L // nseq
    c = pl.program_id(1)
    nchunks = pl.num_programs(1)

    if sample:
        for s in range(nseq):
            extx[s, HALO - 3:HALO, :] = hx_ref[s]
            extb[s, HALO - 3:HALO, :] = hb_ref[s]
            extc[s, HALO - 3:HALO, :] = hc_ref[s]
            for pp in range(SSD_PAIRS):
                st_scr[s, pp] = s0_ref[s, pp].T
    else:
        @pl.when(c == 0)
        def _():
            extx[0, 0:HALO, :] = jnp.zeros((HALO, SSD_W), F32)
            extb[0, 0:HALO, :] = jnp.zeros((HALO, SSD_GB), F32)
            extc[0, 0:HALO, :] = jnp.zeros((HALO, SSD_GB), F32)
            st_scr[...] = jnp.zeros_like(st_scr)

    def conv(ext, src_ref, cw_ref, cb_ref, dst):
        for s in range(nseq):
            ext[s, HALO:HALO + lseq, :] = src_ref[s * lseq:(s + 1) * lseq, :]
        for s in range(nseq):
            acc = cb_ref[...]
            for k in range(B_CONV):
                acc = acc + ext[s, HALO - 3 + k:HALO - 3 + k + lseq, :] * cw_ref[k:k + 1, :]
            dst[s * lseq:(s + 1) * lseq, :] = _silu(acc).astype(dst.dtype)
        if not sample:
            ext[0, 0:HALO, :] = ext[0, lseq:lseq + HALO, :]

    conv(extx, x_ref, cwx_ref, cbx_ref, xs_scr)
    conv(extb, b_ref, cwb_ref, cbb_ref, bs_scr)
    conv(extc, c_ref, cwc_ref, cbc_ref, cs_scr)
    yacc_scr[...] = _silu(z_ref[...])

    dt = _softplus(dt_ref[...] + dtb_ref[...])
    dta = dt * (-jnp.exp(alog_ref[...]))
    cum = _dot_01x(tri_ref[...], dta)
    if nseq == 1:
        ctot = jnp.broadcast_to(cum[L - 1:L, :], (L, LANES))
    else:
        ctot = _dot_01x(same_ref[...], dta)
    cumt_scr[...] = cum.T
    dtt_scr[...] = dt.T
    e64 = e64_ref[...]
    expand = lambda a: jnp.dot(a.astype(BF16), e64, preferred_element_type=F32)
    ecx_scr[...] = expand(jnp.exp(cum))
    wex_scr[...] = expand(jnp.exp(ctot - cum) * dt)
    drows = HALO if nseq == 1 else L
    dcx_scr[0:drows, :] = _dot_x01(jnp.exp(ctot[0:drows]), e64)
    cc_scr[...] = _dot_x01(cum, el_ref[...])
    mask = tri_ref[...] > 0
    lane = lax.broadcasted_iota(jnp.int32, (L, LANES), 1)
    first = lane < B_HEAD_DIM

    for g2 in range(SSD_GB // B_STATE):
        bg = bs_scr[:, g2 * B_STATE:(g2 + 1) * B_STATE]
        cg = cs_scr[:, g2 * B_STATE:(g2 + 1) * B_STATE]
        cb = lax.dot_general(cg, bg, (((1,), (1,)), ((), ())), preferred_element_type=F32)
        for p in range(SSD_GP):
            pp = g2 * SSD_GP + p
            lanes = slice(pp * LANES, (pp + 1) * LANES)
            ms = []
            for r in (2 * pp, 2 * pp + 1):
                seg = cc_scr[:, r * L:(r + 1) * L] - cumt_scr[r:r + 1, :]
                ms.append((cb * jnp.exp(jnp.where(mask, seg, -jnp.inf)) * dtt_scr[r:r + 1, :]).astype(BF16))
            xp = xs_scr[:, lanes]
            xa = jnp.where(first, xp, 0.0).astype(BF16)
            xb = jnp.where(first, 0.0, xp).astype(BF16)
            y = jnp.dot(jnp.concatenate(ms, axis=1), jnp.concatenate([xa, xb], axis=0), preferred_element_type=F32)
            xw = (xp * wex_scr[:, lanes]).astype(BF16)
            ys = []
            for s in range(nseq):
                rows = slice(s * lseq, (s + 1) * lseq)
                st = st_scr[s, pp]
                ys.append(jnp.dot(cg[rows], st.astype(BF16), preferred_element_type=F32))
                upd = lax.dot_general(bg[rows], xw[rows], (((0,), (0,)), ((), ())), preferred_element_type=F32)
                st_scr[s, pp] = st * dcx_scr[s * lseq:s * lseq + 1, lanes] + upd
            ystate = ys[0] if nseq == 1 else jnp.concatenate(ys, axis=0)
            y = y + ystate * ecx_scr[:, lanes] + dl_ref[:, lanes] * xp
            yacc_scr[:, lanes] = y * yacc_scr[:, lanes]

    gw = SSD_NW
    for g2 in range(SSD_W // gw):
        sl = slice(g2 * gw, (g2 + 1) * gw)
        yg = yacc_scr[:, sl]
        ms_ = jnp.mean(yg * yg, axis=-1, keepdims=True)
        y_ref[:, sl] = (yg * lax.rsqrt(ms_ + NORM_EPS) * nw_ref[:, sl]).astype(y_ref.dtype)

    if sample:
        for s in range(nseq):
            for pp in range(SSD_PAIRS):
                so_ref[s, pp] = st_scr[s, pp].T
    else:
        @pl.when(c == nchunks - 1)
        def _():
            for pp in range(SSD_PAIRS):
                so_ref[0, pp] = st_scr[0, pp].T


def _ssd_consts(lseq):
    L = SSD_L
    t = jnp.arange(L)
    same = (t[:, None] // lseq) == (t[None, :] // lseq)
    tri = same & (t[None, :] <= t[:, None])
    r = jnp.arange(LANES)
    e64 = (r[:, None] == (jnp.arange(SSD_W)[None, :] // B_HEAD_DIM)) & (r[:, None] < SSD_HG)
    el = (r[:, None] == (jnp.arange(SSD_HG * L)[None, :] // L)) & (r[:, None] < SSD_HG)
    return tri.astype(BF16), same.astype(BF16), e64.astype(BF16), el.astype(BF16)


def ssd_core(pz, dtp, conv_w, conv_b, dtb, alog, dlane, nw, *, lseq, conv_state=None, ssm_state=None):
    m = pz.shape[0]
    sample = conv_state is not None
    nseq = SSD_L // lseq
    nchunks = m // SSD_L
    nslices = B_HEADS // SSD_HG
    tri, same, e64, el = _ssd_consts(lseq)
    nx = 4096 // SSD_W
    rowblk = lambda w, off: pl.BlockSpec((SSD_L, w), lambda j, c, off=off: (c, off + j))
    parblk = lambda r, w, off: pl.BlockSpec((r, w), lambda j, c, off=off: (0, off + j))
    const = lambda a: pl.BlockSpec(a.shape, lambda j, c: (0, 0))
    ob = 8192 // SSD_GB
    oc = 9216 // SSD_GB
    in_specs = [
        rowblk(SSD_W, 0), rowblk(SSD_W, nx), rowblk(SSD_GB, ob), rowblk(SSD_GB, oc), rowblk(LANES, 0),
        parblk(B_CONV, SSD_W, 0), parblk(B_CONV, SSD_GB, 4096 // SSD_GB), parblk(B_CONV, SSD_GB, 5120 // SSD_GB),
        parblk(1, SSD_W, 0), parblk(1, SSD_GB, 4096 // SSD_GB), parblk(1, SSD_GB, 5120 // SSD_GB),
        parblk(1, LANES, 0), parblk(1, LANES, 0), parblk(1, SSD_W, 0), parblk(1, SSD_W, 0),
        const(tri), const(same), const(e64), const(el),
    ]
    args = [pz, pz, pz, pz, dtp, conv_w, conv_w, conv_w, conv_b, conv_b, conv_b, dtb, alog, dlane, nw, tri, same, e64, el]
    if sample:
        hblk = lambda w, off: pl.BlockSpec((nseq, B_CONV - 1, w), lambda j, c, off=off: (c, 0, off + j))
        in_specs += [hblk(SSD_W, 0), hblk(SSD_GB, 4096 // SSD_GB), hblk(SSD_GB, 5120 // SSD_GB),
                     pl.BlockSpec((nseq, SSD_PAIRS, LANES, B_STATE), lambda j, c: (c, j, 0, 0),
                                  pipeline_mode=pl.Buffered(1))]
        args += [conv_state, conv_state, conv_state, ssm_state]
        nstates = m // lseq
        so_spec = pl.BlockSpec((nseq, SSD_PAIRS, LANES, B_STATE), lambda j, c: (c, j, 0, 0))
    else:
        nstates = 1
        so_spec = pl.BlockSpec((1, SSD_PAIRS, LANES, B_STATE), lambda j, c: (0, j, 0, 0))
    L = SSD_L
    scratch = [
        pltpu.VMEM((nseq, HALO + lseq, SSD_W), F32), pltpu.VMEM((nseq, HALO + lseq, SSD_GB), F32),
        pltpu.VMEM((nseq, HALO + lseq, SSD_GB), F32),
        pltpu.VMEM((nseq, SSD_PAIRS, B_STATE, LANES), F32),
        pltpu.VMEM((L, SSD_W), F32), pltpu.VMEM((L, SSD_GB), BF16), pltpu.VMEM((L, SSD_GB), BF16),
        pltpu.VMEM((L, SSD_W), F32), pltpu.VMEM((L, SSD_W), F32), pltpu.VMEM((L, SSD_W), F32),
        pltpu.VMEM((L, SSD_HG * L), F32), pltpu.VMEM((LANES, L), F32), pltpu.VMEM((LANES, L), F32),
        pltpu.VMEM((L, SSD_W), F32),
    ]
    return pl.pallas_call(
        functools.partial(_ssd_kernel, nseq=nseq, sample=sample),
        grid=(nslices, nchunks),
        in_specs=in_specs,
        out_specs=[pl.BlockSpec((SSD_L, SSD_W), lambda j, c: (c, j)), so_spec],
        out_shape=[jax.ShapeDtypeStruct((m, B_HEADS * B_HEAD_DIM), BF16),
                   jax.ShapeDtypeStruct((nstates, B_HEADS // 2, LANES, B_STATE), F32)],
        scratch_shapes=scratch,
        compiler_params=_cparams(("arbitrary", "arbitrary"), 60),
        name="ssd_core",
    )(*args)


def _group_heads(v, fill=0.0):
    lead = v.shape[:-1]
    g = v.reshape(*lead, B_HEADS // SSD_HG, SSD_HG)
    g = jnp.pad(g, [(0, 0)] * (len(lead) + 1) + [(0, LANES - SSD_HG)], constant_values=fill)
    return g.reshape(*lead, (B_HEADS // SSD_HG) * LANES)


N_MIXERS = 3
PROJ_TM = 1024
PROJ_TN = 1024
OUT_TM = 512


def _run_stream(x, lseq, tm, par, state_conv=None, state_ssm=None, cache_k=None, cache_v=None):
    sample = state_conv is not None
    depth = par["norm_w"].shape[0]
    new = {"v": [], "ssm": [], "conv": [], "k": [], "v_attn": []}
    h = rms_norm_bf16(x, par["norm_w"][0], tm=tm)
    for i in range(depth):
        kind, j = i % N_MIXERS, i // N_MIXERS
        if kind == 0:
            p = matmul(h, par["a_w_in"], j, tm=tm, tn=PROJ_TN, out_dtype=BF16, name="a_in_proj")
            wpos, bias = _gmlp_pos_params(par["a_w_s"][j], par["a_b_s"][j], min(lseq, A_CHUNK))
            r = gmlp_core(p, wpos, bias, par["a_ln_g"][j], par["a_ln_b"][j], emit_v=sample)
            if sample:
                y, v = r
                new["v"].append(v)
            else:
                y = r
            w_out = par["a_w_out"]
        elif kind == 1:
            pz = matmul(h, par["b_w_in"], j, tm=tm, tn=PROJ_TN, n=par["b_zx"], name="b_in_proj")
            dtp = matmul(h, par["b_w_dt"], j, tm=tm, tn=par["b_w_dt"].shape[2], name="b_dt_proj")
            y, s = ssd_core(
                pz, dtp, par["b_conv_w"][j], par["b_conv_b"][j], par["b_dt_bias"][j], par["b_a_log"][j],
                par["b_d_lane"][j], par["b_norm_w"][j], lseq=min(lseq, SSD_L),
                conv_state=state_conv[j] if sample else None,
                ssm_state=state_ssm[j].reshape(-1, B_HEADS // 2, LANES, B_STATE) if sample else None)
            new["ssm"].append(s.reshape(-1, B_HEADS, B_HEAD_DIM, B_STATE))
            new["conv"].append(pz.reshape(-1, lseq, pz.shape[1])[:, lseq - (B_CONV - 1):, 4096:])
            w_out = par["b_w_out"]
        else:
            if sample:
                p = matmul(h, par["c_w_in"], j, tm=tm, tn=PROJ_TN, name="c_in_proj")
                y, k, v = attn_sample(p, cache_k[j], cache_v[j], hb=8)
            else:
                cw = par["c_w_in"].shape[2] // 4
                part = lambda c, hm: matmul(h, par["c_w_in"], j, tm=tm, tn=PROJ_TN, n=cw, col0=c * cw,
                                            head_major=hm, name="c_in_proj")
                q, k, v, zg = part(0, True), part(1, True), part(2, True), part(3, False)
                y = attn_prompt(q, k, v, zg, tq=min(C_TQ, x.shape[0]))
                k, v = k[None], v[None]
            new["k"].append(k)
            new["v_attn"].append(v)
            w_out = par["c_w_out"]
        last = i == depth - 1
        nw = par["final_norm_w"] if last else par["norm_w"][i + 1]
        r = matmul_residual_norm(y, w_out, j, x, nw, tm=OUT_TM, final=last)
        if last:
            return r, new
        x, h = r


def kernel(x_prompt, x_sample, state_ssm, state_conv, cache_k, cache_v, norm_w, final_norm_w, a_w_in, a_ln_g, a_ln_b, a_w_s, a_b_s, a_w_out, b_w_in, b_conv_w, b_conv_b, b_dt_bias, b_a_log, b_d, b_norm_w, b_w_out, c_w_in, c_w_out):
    bp, seq, d = x_prompt.shape
    bs, dseq, _ = x_sample.shape
    nb = b_w_in.shape[0]
    zx = b_w_in.shape[2] - B_HEADS
    par = {
        "norm_w": norm_w, "final_norm_w": final_norm_w,
        "a_w_in": a_w_in, "a_ln_g": a_ln_g, "a_ln_b": a_ln_b, "a_w_s": a_w_s, "a_b_s": a_b_s,
        "a_w_out": a_w_out.astype(BF16),
        "b_w_in": b_w_in, "b_zx": zx, "b_w_dt": _group_heads(b_w_in[:, :, zx:]),
        "b_conv_w": b_conv_w, "b_conv_b": b_conv_b.reshape(nb, 1, -1),
        "b_dt_bias": _group_heads(b_dt_bias).reshape(nb, 1, -1), "b_a_log": _group_heads(b_a_log).reshape(nb, 1, -1),
        "b_d_lane": jnp.repeat(b_d, B_HEAD_DIM, axis=-1).reshape(nb, 1, -1), "b_norm_w": b_norm_w.reshape(nb, 1, -1),
        "b_w_out": b_w_out.astype(BF16), "c_w_in": c_w_in, "c_w_out": c_w_out.astype(BF16),
    }
    assert bp == 1, "the prompt group is one stream"
    yp, newp = _run_stream(x_prompt.reshape(seq, d), seq, min(PROJ_TM, seq), par)
    ys, news = _run_stream(x_sample.reshape(bs * dseq, d), dseq, min(PROJ_TM, bs * dseq), par,
                           state_conv=state_conv, state_ssm=state_ssm, cache_k=cache_k, cache_v=cache_v)
    st = jnp.stack
    return (
        yp.reshape(bp, seq, d),
        ys.reshape(bs, dseq, d),
        st([v.reshape(bs, dseq, -1) for v in news["v"]]),
        st(newp["ssm"]), st(newp["conv"]), st(news["ssm"]), st(news["conv"]),
        st(newp["k"]), st(newp["v_attn"]), st(news["k"]), st(news["v_attn"]),
    )
```
